```python
import math
import jax
import jax.numpy as jnp
from jax import lax
import numpy as np

D_MODEL = 1024
BATCH = 32
SEQ = 256
DEPTH = 2
DEC_BATCH = 4
DEC_SEQ = 1024
PAST_LEN = 512

GRID_W = 64
ROPE_BASE = 10000.0
EPS = 1e-6
CHUNK = 64
Q_BLOCK = 128
CONV_W = 3
N_BRANCH = 4
MIX_W = 512

MLA_HEADS = 8
MLA_NOPE = 64
MLA_ROPE = 32
MLA_V = 64
MLA_Q_RANK = 256
MLA_KV_RANK = 128

DN_HEADS = 4
DN_DK = 128
DN_DV = 128

RET_HEADS = 4
RET_DK = 128
RET_DV = 128

SC_W = MIX_W

N_EXPERTS = 64
TOP_K = 8
N_GROUPS = 8
TOPK_GROUPS = 4
D_EXPERT = 256
D_SHARED = 256
ROUTED_SCALE = 2.5
MOE_BLOCK = 128

DN_QK = DN_HEADS * DN_DK
DN_VW = DN_HEADS * DN_DV
RET_QK = RET_HEADS * RET_DK
RET_VW = RET_HEADS * RET_DV
IN_WIDTHS = (MLA_Q_RANK, MLA_KV_RANK + MLA_ROPE, 2 * DN_QK + DN_VW, DN_VW, 4 * DN_HEADS,
             2 * RET_QK + 2 * RET_VW, 3 * SC_W, N_BRANCH * D_MODEL)
D_IN = sum(IN_WIDTHS)

kernel_name = 'hybrid_mla_deltanet_retention_shortconv_moe_prefix_dit_step'


def split_cols(x, widths):
    cuts = [int(v) for v in np.cumsum(widths)[:-1]]
    return jnp.split(x, cuts, axis=-1)


def flip_seq(t):
    return jnp.flip(t, axis=1)


def rmsnorm(x, g):
    xf = x.astype(jnp.float32)
    y = xf * lax.rsqrt(jnp.mean(xf * xf, axis=-1, keepdims=True) + EPS)
    return (y * g.astype(jnp.float32)).astype(x.dtype)


def l2norm(x):
    xf = x.astype(jnp.float32)
    return (xf * lax.rsqrt(jnp.sum(xf * xf, axis=-1, keepdims=True) + EPS)).astype(x.dtype)


def head_rmsnorm(o, g):
    return o * lax.rsqrt(jnp.mean(o * o, axis=-1, keepdims=True) + EPS) * g.astype(jnp.float32)


def head_groupnorm(o, g):
    d = o - jnp.mean(o, axis=-1, keepdims=True)
    y = d * lax.rsqrt(jnp.mean(d * d, axis=-1, keepdims=True) + EPS)
    return y.reshape(o.shape[:2] + (-1,)) * g.astype(jnp.float32)


def dwconv_centred(x, w):
    pad = CONV_W // 2
    s = x.shape[1]
    xp = jnp.pad(x, ((0, 0), (pad, pad), (0, 0)))
    return sum(xp[:, j:j + s] * w[j] for j in range(CONV_W))


def axial_rope(n_tok, dim):
    rows = n_tok // GRID_W
    nf = dim // 4
    inv = ROPE_BASE ** (-jnp.arange(nf, dtype=jnp.float32) / nf)
    r = jnp.repeat(jnp.arange(rows, dtype=jnp.float32), GRID_W)
    cc = jnp.tile(jnp.arange(GRID_W, dtype=jnp.float32), rows)
    ang = jnp.concatenate([r[:, None] * inv, cc[:, None] * inv], axis=-1)
    return jnp.cos(ang), jnp.sin(ang)


def apply_rope(x, cos, sin):
    half = x.shape[-1] // 2
    shape = (1, x.shape[1]) + (1,) * (x.ndim - 3) + (half,)
    cs, sn = cos.reshape(shape), sin.reshape(shape)
    xf = x.astype(jnp.float32)
    x1, x2 = xf[..., :half], xf[..., half:]
    return jnp.concatenate([x1 * cs - x2 * sn, x1 * sn + x2 * cs], axis=-1).astype(x.dtype)


def to_chunks(t):
    b, s, h = t.shape[:3]
    t = t.reshape((b, s // CHUNK, CHUNK, h) + t.shape[3:])
    return jnp.moveaxis(t, (1, 3), (0, 2))


def from_chunks(t):
    n, b, h, c, d = t.shape
    return jnp.moveaxis(t, (0, 2), (1, 3)).reshape(b, n * c, h, d)


def gated_delta_chunked(q, k, v, log_a, beta, s0):
    f32 = jnp.float32
    q, k, v = (to_chunks(t.astype(f32)) for t in (q, k, v))
    g = jnp.cumsum(to_chunks(log_a.astype(f32)), axis=-1)
    beta = to_chunks(beta.astype(f32))
    causal = jnp.tril(jnp.ones((CHUNK, CHUNK), dtype=bool))
    diff = g[..., :, None] - g[..., None, :]
    decay = jnp.where(causal, jnp.exp(jnp.where(causal, diff, 0.0)), 0.0)
    kk = jnp.einsum('nbhtk,nbhsk->nbhts', k, k)
    a_mat = jnp.eye(CHUNK, dtype=f32) + jnp.tril(beta[..., :, None] * decay * kk, -1)
    u_v = lax.linalg.triangular_solve(a_mat, beta[..., None] * v, left_side=True, lower=True, unit_diagonal=True)
    w_k = lax.linalg.triangular_solve(a_mat, (beta * jnp.exp(g))[..., None] * k, left_side=True, lower=True,
                                      unit_diagonal=True)
    qk = jnp.einsum('nbhtk,nbhsk->nbhts', q, k) * decay
    q_dec = jnp.exp(g)[..., None] * q
    k_dec = jnp.exp(g[..., -1:] - g)[..., None] * k
    g_tot = jnp.exp(g[..., -1])

    def step(s, xs):
        uv_n, wk_n, qk_n, qd_n, kd_n, gt_n = xs
        u = uv_n - jnp.einsum('bhtk,bhkv->bhtv', wk_n, s)
        o = jnp.einsum('bhtk,bhkv->bhtv', qd_n, s) + jnp.einsum('bhts,bhsv->bhtv', qk_n, u)
        s = gt_n[..., None, None] * s + jnp.einsum('bhsk,bhsv->bhkv', kd_n, u)
        return s, o

    s_fin, o = lax.scan(step, s0.astype(f32), (u_v, w_k, qk, q_dec, k_dec, g_tot))
    return from_chunks(o), s_fin


def retention_chunked(q, k, v, log_gamma, s0):
    f32 = jnp.float32
    q, k, v = (to_chunks(t.astype(f32)) for t in (q, k, v))
    lg = log_gamma.astype(f32)
    pos = jnp.arange(CHUNK, dtype=f32)
    diff = pos[:, None] - pos[None, :]
    decay = jnp.where(diff >= 0, jnp.exp(jnp.maximum(diff, 0.0) * lg[:, None, None]), 0.0)
    o_intra = jnp.einsum('nbhts,nbhsv->nbhtv', jnp.einsum('nbhtk,nbhsk->nbhts', q, k) * decay, v)
    q_dec = q * jnp.exp((pos + 1.0)[None, :] * lg[:, None])[..., None]
    k_dec = k * jnp.exp((CHUNK - 1.0 - pos)[None, :] * lg[:, None])[..., None]
    g_tot = jnp.exp(CHUNK * lg)

    def step(s, xs):
        oi_n, qd_n, kd_n, v_n = xs
        o = oi_n + jnp.einsum('bhtk,bhkv->bhtv', qd_n, s)
        s = g_tot[:, None, None] * s + jnp.einsum('bhsk,bhsv->bhkv', kd_n, v_n)
        return s, o

    s_fin, o = lax.scan(step, s0.astype(f32), (o_intra, q_dec, k_dec, v))
    return from_chunks(o), s_fin


def mla_attend(q_nope, q_pe, k_nope, k_pe, v):
    b, s, h, _ = q_nope.shape
    nb = s // Q_BLOCK
    scale = (MLA_NOPE + MLA_ROPE) ** -0.5

    def blocks(t):
        return jnp.moveaxis(t.reshape((b, nb, Q_BLOCK) + t.shape[2:]), 1, 0)

    def one_block(args):
        qn, qp = args
        sc = jnp.einsum('bqhd,bkhd->bhqk', qn, k_nope) + jnp.einsum('bqhr,bkr->bhqk', qp, k_pe)
        p = jax.nn.softmax(sc.astype(jnp.float32) * scale, axis=-1)
        return jnp.einsum('bhqk,bkhd->bqhd', p.astype(v.dtype), v)

    o = lax.map(one_block, (blocks(q_nope), blocks(q_pe)))
    return jnp.moveaxis(o, 0, 1).reshape(b, s, h * MLA_V)


def token_mixer(h, p, rope_mla, rope_ret, ctx):
    b, s, _ = h.shape
    latent = ctx is not None
    proj = h @ p['w_in']
    c_q, kv_lat, dn_qkv, dn_z, dn_ab, ret_qkvg, sc_in, gate_logits = split_cols(proj, IN_WIDTHS)

    q = (rmsnorm(c_q, p['mla_q_norm']) @ p['mla_w_uq']).reshape(b, s, MLA_HEADS, MLA_NOPE + MLA_ROPE)
    q_nope, q_pe = q[..., :MLA_NOPE], q[..., MLA_NOPE:]
    ckv = rmsnorm(kv_lat[..., :MLA_KV_RANK], p['mla_kv_norm'])
    kpe = kv_lat[..., MLA_KV_RANK:]
    if latent:
        q_pe = apply_rope(q_pe, *rope_mla)
        ckv_all = jnp.concatenate([ctx[0], ckv], axis=1)
        kpe_all = jnp.concatenate([ctx[1], apply_rope(kpe, *rope_mla)], axis=1)
    else:
        ckv_all, kpe_all = ckv, kpe
    kv = (ckv_all @ p['mla_w_ukv']).reshape(b, -1, MLA_HEADS, MLA_NOPE + MLA_V)
    y_mla = mla_attend(q_nope, q_pe, kv[..., :MLA_NOPE], kpe_all, kv[..., MLA_NOPE:])

    dn_qkv = jax.nn.silu(dwconv_centred(dn_qkv, p['dn_conv']))
    dq, dk, dv = split_cols(dn_qkv, (DN_QK, DN_QK, DN_VW))
    dq = l2norm(dq.reshape(b, s, DN_HEADS, DN_DK)) * (DN_DK ** -0.5)
    dk = l2norm(dk.reshape(b, s, DN_HEADS, DN_DK))
    dv = dv.reshape(b, s, DN_HEADS, DN_DV)
    ab = dn_ab.reshape(b, s, 2, 2, DN_HEADS)
    log_a = (-jnp.exp(p['dn_A_log'].astype(jnp.float32))
             * jax.nn.softplus((ab[:, :, 0] + p['dn_dt_bias']).astype(jnp.float32)))
    beta = jax.nn.sigmoid(ab[:, :, 1].astype(jnp.float32))
    s0_dn = ctx[2] if latent else jnp.zeros((b, 2, DN_HEADS, DN_DK, DN_DV), jnp.float32)
    o_f, dn_f = gated_delta_chunked(dq, dk, dv, log_a[:, :, 0], beta[:, :, 0], s0_dn[:, 0])
    o_b, dn_b = gated_delta_chunked(flip_seq(dq), flip_seq(dk), flip_seq(dv), flip_seq(log_a[:, :, 1]),
                                    flip_seq(beta[:, :, 1]), s0_dn[:, 1])
    o_dn = head_rmsnorm(o_f + flip_seq(o_b), p['dn_norm']).reshape(b, s, DN_VW)
    y_dn = (o_dn * jax.nn.silu(dn_z.astype(jnp.float32))).astype(h.dtype)

    rq, rk, rv, rg = split_cols(ret_qkvg, (RET_QK, RET_QK, RET_VW, RET_VW))
    rq = rq.reshape(b, s, RET_HEADS, RET_DK)
    rk = rk.reshape(b, s, RET_HEADS, RET_DK) * (RET_DK ** -0.5)
    rv = rv.reshape(b, s, RET_HEADS, RET_DV)
    if latent:
        rq = apply_rope(rq, *rope_ret)
        rk = apply_rope(rk, *rope_ret)
    log_gamma = -jnp.exp(p['ret_decay'].astype(jnp.float32))
    s0_ret = ctx[3] if latent else jnp.zeros((b, 2, RET_HEADS, RET_DK, RET_DV), jnp.float32)
    r_f, ret_f = retention_chunked(rq, rk, rv, log_gamma[0], s0_ret[:, 0])
    r_b, ret_b = retention_chunked(flip_seq(rq), flip_seq(rk), flip_seq(rv), log_gamma[1], s0_ret[:, 1])
    o_ret = head_groupnorm(r_f + flip_seq(r_b), p['ret_gn'])
    y_ret = (jax.nn.silu(rg.astype(jnp.float32)) * o_ret).astype(h.dtype)

    sb, sc, sx = split_cols(sc_in, (SC_W, SC_W, SC_W))
    y_sc = sb * dwconv_centred(sc * sx, p['sc_conv'])

    branches = jnp.stack([y_mla, y_dn, y_ret, y_sc], axis=2)
    proj_br = jnp.einsum('bsnw,nwd->bsnd', branches, p['w_branch'])
    gates = jax.nn.sigmoid(gate_logits.reshape(b, s, N_BRANCH, D_MODEL))
    y = jnp.sum(gates * proj_br, axis=2) @ p['w_out']
    if latent:
        return y, None
    s_dn = jnp.stack([dn_f, dn_b], axis=1)
    s_ret = jnp.stack([ret_f, ret_b], axis=1)
    return y, (ckv, kpe, s_dn, s_ret)


def route(h, w_router, bias):
    t = h.shape[0]
    scores = jax.nn.sigmoid((h @ w_router).astype(jnp.float32))
    choice = scores + bias.astype(jnp.float32)
    grp = choice.reshape(t, N_GROUPS, N_EXPERTS // N_GROUPS)
    grp_score = jnp.sum(lax.top_k(grp, 2)[0], axis=-1)
    _, gidx = lax.top_k(grp_score, TOPK_GROUPS)
    gmask = jnp.any(gidx[..., None] == jnp.arange(N_GROUPS), axis=1)
    emask = jnp.repeat(gmask, N_EXPERTS // N_GROUPS, axis=1)
    _, idx = lax.top_k(jnp.where(emask, choice, -jnp.inf), TOP_K)
    w = jnp.take_along_axis(scores, idx, axis=-1)
    w = w / jnp.sum(w, axis=-1, keepdims=True) * ROUTED_SCALE
    return idx, w


def routed_experts(h, idx, w, w_g, w_u, w_d):
    t, d = h.shape
    n_assign = t * TOP_K
    n_blocks = n_assign // MOE_BLOCK + N_EXPERTS
    flat_e = idx.reshape(-1)
    flat_tok = jnp.repeat(jnp.arange(t, dtype=jnp.int32), TOP_K)
    flat_w = w.reshape(-1)
    order = jnp.argsort(flat_e)
    e_sorted = flat_e[order]
    counts = jnp.bincount(flat_e, length=N_EXPERTS)
    padded = (counts + MOE_BLOCK - 1) // MOE_BLOCK * MOE_BLOCK
    pad_end = jnp.cumsum(padded)
    start = jnp.cumsum(counts) - counts
    dest = (pad_end - padded)[e_sorted] + jnp.arange(n_assign) - start[e_sorted]
    slot_tok = jnp.full((n_blocks * MOE_BLOCK,), t, jnp.int32).at[dest].set(flat_tok[order])
    slot_w = jnp.zeros((n_blocks * MOE_BLOCK,), flat_w.dtype).at[dest].set(flat_w[order])
    block_e = jnp.minimum(jnp.searchsorted(pad_end, jnp.arange(n_blocks) * MOE_BLOCK, side='right'),
                          N_EXPERTS - 1)
    h_pad = jnp.concatenate([h, jnp.zeros((1, d), h.dtype)], axis=0)

    def one_block(args):
        tok, e = args
        xb = h_pad[tok]
        return (jax.nn.silu(xb @ w_g[e]) * (xb @ w_u[e])) @ w_d[e]

    y = lax.map(one_block, (slot_tok.reshape(n_blocks, MOE_BLOCK), block_e))
    y = y.reshape(-1, d) * slot_w[:, None].astype(h.dtype)
    return jax.ops.segment_sum(y, slot_tok, num_segments=t + 1)[:t]


def moe_ffn(h, p):
    b, s, d = h.shape
    hf = h.reshape(-1, d)
    idx, w = route(hf, p['router'], p['router_bias'])
    routed = routed_experts(hf, idx, w, p['w_eg'], p['w_eu'], p['w_ed'])
    shared = (jax.nn.silu(hf @ p['w_sg']) * (hf @ p['w_su'])) @ p['w_sd']
    return (routed + shared).reshape(b, s, d)


def trunk_layer(x, mod, p, rope_mla, rope_ret, ctx):
    shift_m, scale_m, gate_m, shift_f, scale_f, gate_f = jnp.split(mod, 6, axis=-1)
    h = rmsnorm(x, p['norm_mix']) * (1.0 + scale_m) + shift_m
    y, ctx_out = token_mixer(h, p, rope_mla, rope_ret, ctx)
    x = x + gate_m * y
    h = rmsnorm(x, p['norm_ffn']) * (1.0 + scale_f) + shift_f
    x = x + gate_f * moe_ffn(h, p)
    return x, ctx_out


def setup_inputs(seed: int = 0) -> dict:
    key = jax.random.key(seed)
    keys = iter(jax.random.split(key, 64))

    def nrm(shape, scale):
        return jax.random.normal(next(keys), shape, jnp.float32) * scale

    def gain(shape):
        return 1.0 + nrm(shape, 0.02)

    def unif(shape, lo, hi):
        return jax.random.uniform(next(keys), shape, jnp.float32, lo, hi)

    sched = 1.0 - jnp.exp(jnp.linspace(math.log(1.0 / 32), math.log(1.0 / 512), RET_HEADS))
    dt = jnp.exp(unif((DEPTH, 2, DN_HEADS), math.log(1e-3), math.log(1e-1)))
    return {
        'x_prompt': nrm((BATCH, SEQ, D_MODEL), 1.0),
        'x_sample': nrm((DEC_BATCH, DEC_SEQ, D_MODEL), 1.0),
        'c': nrm((DEC_BATCH, D_MODEL), 1.0),
        'cache_ckv': nrm((DEC_BATCH, DEPTH, PAST_LEN, MLA_KV_RANK), 1.0),
        'cache_kpe': nrm((DEC_BATCH, DEPTH, PAST_LEN, MLA_ROPE), 1.0),
        'state_dn': nrm((DEC_BATCH, DEPTH, 2, DN_HEADS, DN_DK, DN_DV), DN_DK ** -0.5),
        'state_ret': nrm((DEC_BATCH, DEPTH, 2, RET_HEADS, RET_DK, RET_DV), 0.1),
        'c_ctx': nrm((D_MODEL,), 1.0),
        'w_ada': nrm((DEPTH, D_MODEL, 6 * D_MODEL), 0.5 * D_MODEL ** -0.5),
        'b_ada': nrm((DEPTH, 6 * D_MODEL), 0.02),
        'norm_mix': gain((DEPTH, D_MODEL)),
        'norm_ffn': gain((DEPTH, D_MODEL)),
        'w_in': nrm((DEPTH, D_MODEL, D_IN), D_MODEL ** -0.5),
        'mla_q_norm': gain((DEPTH, MLA_Q_RANK)),
        'mla_w_uq': nrm((DEPTH, MLA_Q_RANK, MLA_HEADS * (MLA_NOPE + MLA_ROPE)), MLA_Q_RANK ** -0.5),
        'mla_kv_norm': gain((DEPTH, MLA_KV_RANK)),
        'mla_w_ukv': nrm((DEPTH, MLA_KV_RANK, MLA_HEADS * (MLA_NOPE + MLA_V)), MLA_KV_RANK ** -0.5),
        'dn_conv': nrm((DEPTH, CONV_W, 2 * DN_QK + DN_VW), 0.5),
        'dn_A_log': jnp.log(unif((DEPTH, 2, DN_HEADS), 1.0, 16.0)),
        'dn_dt_bias': dt + jnp.log(-jnp.expm1(-dt)),
        'dn_norm': gain((DEPTH, DN_DV)),
        'ret_decay': jnp.log(-jnp.log(sched))[None, None, :] + nrm((DEPTH, 2, RET_HEADS), 0.05),
        'ret_gn': gain((DEPTH, RET_VW)),
        'sc_conv': nrm((DEPTH, CONV_W, SC_W), 0.5),
        'w_branch': nrm((DEPTH, N_BRANCH, MIX_W, D_MODEL), MIX_W ** -0.5),
        'w_out': nrm((DEPTH, D_MODEL, D_MODEL), D_MODEL ** -0.5),
        'router': nrm((DEPTH, D_MODEL, N_EXPERTS), D_MODEL ** -0.5),
        'router_bias': nrm((DEPTH, N_EXPERTS), 0.01),
        'w_eg': nrm((DEPTH, N_EXPERTS, D_MODEL, D_EXPERT), D_MODEL ** -0.5),
        'w_eu': nrm((DEPTH, N_EXPERTS, D_MODEL, D_EXPERT), D_MODEL ** -0.5),
        'w_ed': nrm((DEPTH, N_EXPERTS, D_EXPERT, D_MODEL), D_EXPERT ** -0.5),
        'w_sg': nrm((DEPTH, D_MODEL, D_SHARED), D_MODEL ** -0.5),
        'w_su': nrm((DEPTH, D_MODEL, D_SHARED), D_MODEL ** -0.5),
        'w_sd': nrm((DEPTH, D_SHARED, D_MODEL), D_SHARED ** -0.5),
        'final_norm': gain((D_MODEL,)),
    }


def reference(x_prompt, x_sample, c, cache_ckv, cache_kpe, state_dn, state_ret, c_ctx,
              w_ada, b_ada, norm_mix, norm_ffn, w_in, mla_q_norm, mla_w_uq, mla_kv_norm, mla_w_ukv,
              dn_conv, dn_A_log, dn_dt_bias, dn_norm, ret_decay, ret_gn, sc_conv,
              w_branch, w_out, router, router_bias, w_eg, w_eu, w_ed, w_sg, w_su, w_sd, final_norm):
    def layer_params(l):
        return {
            'norm_mix': norm_mix[l], 'norm_ffn': norm_ffn[l], 'w_in': w_in[l],
            'mla_q_norm': mla_q_norm[l], 'mla_w_uq': mla_w_uq[l],
            'mla_kv_norm': mla_kv_norm[l], 'mla_w_ukv': mla_w_ukv[l],
            'dn_conv': dn_conv[l], 'dn_A_log': dn_A_log[l], 'dn_dt_bias': dn_dt_bias[l], 'dn_norm': dn_norm[l],
            'ret_decay': ret_decay[l], 'ret_gn': ret_gn[l], 'sc_conv': sc_conv[l],
            'w_branch': w_branch[l], 'w_out': w_out[l],
            'router': router[l], 'router_bias': router_bias[l],
            'w_eg': w_eg[l], 'w_eu': w_eu[l], 'w_ed': w_ed[l],
            'w_sg': w_sg[l], 'w_su': w_su[l], 'w_sd': w_sd[l],
        }

    xc = x_prompt
    ckv_l, kpe_l, dn_l, ret_l = [], [], [], []
    for l in range(DEPTH):
        mod = (jax.nn.silu(c_ctx) @ w_ada[l] + b_ada[l])[None, None, :]
        xc, ctx_out = trunk_layer(xc, mod, layer_params(l), None, None, None)
        ckv_l.append(ctx_out[0])
        kpe_l.append(ctx_out[1])
        dn_l.append(ctx_out[2])
        ret_l.append(ctx_out[3])
    y_prompt = rmsnorm(xc, final_norm)
    new_ckv = jnp.stack(ckv_l, axis=1).astype(x_prompt.dtype)
    new_kpe = jnp.stack(kpe_l, axis=1).astype(x_prompt.dtype)
    new_dn = jnp.stack(dn_l, axis=1).astype(x_prompt.dtype)
    new_ret = jnp.stack(ret_l, axis=1).astype(x_prompt.dtype)

    n_lat = x_sample.shape[1]
    rope_mla = axial_rope(n_lat, MLA_ROPE)
    rope_ret = axial_rope(n_lat, RET_DK)
    xl = x_sample
    for l in range(DEPTH):
        mod = (jax.nn.silu(c) @ w_ada[l] + b_ada[l])[:, None, :]
        ctx = (cache_ckv[:, l], cache_kpe[:, l], state_dn[:, l], state_ret[:, l])
        xl, _ = trunk_layer(xl, mod, layer_params(l), rope_mla, rope_ret, ctx)
    y_sample = rmsnorm(xl, final_norm)
    return (y_prompt, y_sample, new_ckv, new_kpe, new_dn, new_ret)
```

```python
import functools
import math

import jax
import jax.numpy as jnp
import numpy as np
from jax import lax
from jax.experimental import pallas as pl
from jax.experimental.pallas import tpu as pltpu

F32 = jnp.float32
BF16 = jnp.bfloat16
I32 = jnp.int32

D = 1024
EPS = 1e-6
CHUNK = 64
GRID_W = 64
ROPE_BASE = 10000.0

MLA_H, MLA_NOPE, MLA_ROPE, MLA_V, MLA_QR, MLA_KVR = 8, 64, 32, 64, 256, 128
DN_H, DN_DK = 4, 128
RET_H, RET_DK = 4, 128
MIX_W = 512
N_EXP, TOP_K, N_GRP, TOPK_GRP, D_EXP = 64, 8, 8, 4, 256
ROUTED_SCALE = 2.5

LANE = 128
COL_CQ = 0
COL_KV = 256
COL_DNQKV = 512
COL_DNZ = 2048
COL_RET = 2560
COL_SC = 4608
COL_GATE = 6144
N_PROJ = 10240
AB_LANE = 160

MOE_BM = 256
VMEM_LIMIT = 56 * 1024 * 1024


def _cparams(sem, vmem=None):
    return pltpu.CompilerParams(dimension_semantics=sem, vmem_limit_bytes=vmem or VMEM_LIMIT)


def _dot(a, b):
    return jnp.dot(a, b, preferred_element_type=F32)


def _dot_nt(a, b):
    return lax.dot_general(a, b, (((1,), (1,)), ((), ())), preferred_element_type=F32)


def _dot_tn(a, b):
    return lax.dot_general(a, b, (((0,), (0,)), ((), ())), preferred_element_type=F32)


def _split(a):
    hi = a.astype(BF16)
    lo = (a - hi.astype(F32)).astype(BF16)
    return hi, lo


def _dot3(a, b):
    ah, al = _split(a)
    bh, bl = _split(b)
    return _dot(ah, bh) + (_dot(ah, bl) + _dot(al, bh))


def _dot3_nt(a, b):
    ah, al = _split(a)
    bh, bl = _split(b)
    return _dot_nt(ah, bh) + (_dot_nt(ah, bl) + _dot_nt(al, bh))


def _sigmoid(x):
    return 1.0 / (1.0 + jnp.exp(-x))


def _silu(x):
    return x * _sigmoid(x)


def _rms(x, g):
    return x * lax.rsqrt(jnp.mean(x * x, axis=-1, keepdims=True) + EPS) * g


def _ada_kernel(c_ref, w_ref, b_ref, o_ref):
    o_ref[...] = _dot3(_silu(c_ref[...]), w_ref[...]) + b_ref[...]


def _ada(cvec, w, b):
    n = w.shape[1]
    tn = 1024
    return pl.pallas_call(
        _ada_kernel,
        out_shape=jax.ShapeDtypeStruct((cvec.shape[0], n), F32),
        grid=(n // tn,),
        in_specs=[pl.BlockSpec(cvec.shape, lambda j: (0, 0)),
                  pl.BlockSpec((D, tn), lambda j: (0, j)),
                  pl.BlockSpec((1, tn), lambda j: (0, j))],
        out_specs=pl.BlockSpec((cvec.shape[0], tn), lambda j: (0, j)),
        compiler_params=_cparams(("arbitrary",)),
    )(cvec, w, b.reshape(1, n))


def _in_proj_kernel(x_ref, mod_ref, g_ref, w_ref, o_ref, h_scr):
    @pl.when(pl.program_id(1) == 0)
    def _():
        m = mod_ref[0]
        y = _rms(x_ref[...], g_ref[...])
        h_scr[...] = (y * (1.0 + m[:, D:2 * D]) + m[:, 0:D]).astype(BF16)

    o_ref[...] = _dot(h_scr[...], w_ref[...])


def _in_proj(x, mod3, g, w, grp_of_tile, tm, tn):
    t = x.shape[0]
    return pl.pallas_call(
        _in_proj_kernel,
        out_shape=jax.ShapeDtypeStruct((t, N_PROJ), F32),
        grid=(t // tm, N_PROJ // tn),
        in_specs=[pl.BlockSpec((tm, D), lambda i, j: (i, 0)),
                  pl.BlockSpec((1, 1, 6 * D), lambda i, j: (grp_of_tile(tm)(i), 0, 0)),
                  pl.BlockSpec((1, D), lambda i, j: (0, 0)),
                  pl.BlockSpec((D, tn), lambda i, j: (0, j))],
        out_specs=pl.BlockSpec((tm, tn), lambda i, j: (i, j)),
        scratch_shapes=[pltpu.VMEM((tm, D), BF16)],
        compiler_params=_cparams(("arbitrary", "arbitrary")),
    )(x, mod3, g.reshape(1, D), w)


def _rope3(x, c, s1, s2, width):
    return x * c + pltpu.roll(x, width - 16, 1) * s1 + pltpu.roll(x, 16, 1) * s2


def _mla_q_kernel(*refs, rope):
    if rope:
        p_ref, g_ref, w_ref, c_ref, s1_ref, s2_ref, o_ref = refs
    else:
        p_ref, g_ref, w_ref, o_ref = refs
    y = _rms(p_ref[...], g_ref[...])
    q = _dot(y.astype(BF16), w_ref[...])
    if rope:
        tile = lambda r: jnp.concatenate([r[...]] * MLA_H, axis=1)
        q = _rope3(q, tile(c_ref), tile(s1_ref), tile(s2_ref), MLA_H * LANE)
    o_ref[...] = q.astype(BF16)


def _mla_q(proj, g, w, row_off, rows, tm, rope_tabs, seq):
    nrow = rows // tm
    r0 = row_off // tm
    in_specs = [pl.BlockSpec((tm, MLA_QR), lambda i: (r0 + i, COL_CQ // MLA_QR)),
                pl.BlockSpec((1, MLA_QR), lambda i: (0, 0)),
                pl.BlockSpec((MLA_QR, MLA_H * LANE), lambda i: (0, 0))]
    args = [proj, g.reshape(1, MLA_QR), w]
    if rope_tabs is not None:
        per = seq // tm
        in_specs += [pl.BlockSpec((tm, LANE), lambda i: (i % per, 0))] * 3
        args += list(rope_tabs)
    return pl.pallas_call(
        functools.partial(_mla_q_kernel, rope=rope_tabs is not None),
        out_shape=jax.ShapeDtypeStruct((rows, MLA_H * LANE), BF16),
        grid=(nrow,),
        in_specs=in_specs,
        out_specs=pl.BlockSpec((tm, MLA_H * LANE), lambda i: (i, 0)),
        compiler_params=_cparams(("arbitrary",)),
    )(*args)


def _mla_kv_kernel(*refs, norm, rope):
    if rope:
        p_ref, g_ref, wk_ref, wv_ref, c_ref, s1_ref, s2_ref, ckv_ref, kpe_ref, k_ref, v_ref = refs
    else:
        p_ref, g_ref, wk_ref, wv_ref, ckv_ref, kpe_ref, k_ref, v_ref = refs
    blk = p_ref[...]
    ckv = blk[:, :MLA_KVR]
    if norm:
        ckv = _rms(ckv, g_ref[...])
    kp = blk[:, MLA_KVR:]
    ckv_ref[...] = ckv
    kpe_ref[...] = kp[:, :MLA_ROPE]
    if rope:
        kp = _rope3(kp, c_ref[...], s1_ref[...], s2_ref[...], LANE)
    a = jnp.concatenate([ckv, kp], axis=1).astype(BF16)
    k_ref[...] = _dot(a, wk_ref[...]).astype(BF16)
    v_ref[...] = _dot(ckv.astype(BF16), wv_ref[...]).astype(BF16)


def _mla_kv(src, col_blk, g, wk, wv, row_off, rows, tm, norm, rope_tabs, seq):
    nrow = rows // tm
    r0 = row_off // tm
    in_specs = [pl.BlockSpec((tm, 2 * LANE), lambda i: (r0 + i, col_blk)),
                pl.BlockSpec((1, MLA_KVR), lambda i: (0, 0)),
                pl.BlockSpec((2 * LANE, MLA_H * LANE), lambda i: (0, 0)),
                pl.BlockSpec((MLA_KVR, MLA_H * MLA_V), lambda i: (0, 0))]
    args = [src, g.reshape(1, MLA_KVR), wk, wv]
    if rope_tabs is not None:
        per = seq // tm
        in_specs += [pl.BlockSpec((tm, LANE), lambda i: (i % per, 0))] * 3
        args += list(rope_tabs)
    return pl.pallas_call(
        functools.partial(_mla_kv_kernel, norm=norm, rope=rope_tabs is not None),
        out_shape=(jax.ShapeDtypeStruct((rows, MLA_KVR), F32),
                   jax.ShapeDtypeStruct((rows, MLA_ROPE), F32),
                   jax.ShapeDtypeStruct((rows, MLA_H * LANE), BF16),
                   jax.ShapeDtypeStruct((rows, MLA_H * MLA_V), BF16)),
        grid=(nrow,),
        in_specs=in_specs,
        out_specs=(pl.BlockSpec((tm, MLA_KVR), lambda i: (i, 0)),
                   pl.BlockSpec((tm, MLA_ROPE), lambda i: (i, 0)),
                   pl.BlockSpec((tm, MLA_H * LANE), lambda i: (i, 0)),
                   pl.BlockSpec((tm, MLA_H * MLA_V), lambda i: (i, 0))),
        compiler_params=_cparams(("arbitrary",)),
    )(*args)


def _attn_kernel(*refs, nseg):
    q_ref = refs[0]
    k_refs = refs[1:1 + nseg]
    v_refs = refs[1 + nseg:1 + 2 * nseg]
    o_ref = refs[1 + 2 * nseg]
    scale = (MLA_NOPE + MLA_ROPE) ** -0.5
    outs = []
    for h in range(MLA_H):
        qh = q_ref[0, :, h * LANE:(h + 1) * LANE]
        ss = [_dot_nt(qh, k_ref[0, :, h * LANE:(h + 1) * LANE]) * scale for k_ref in k_refs]
        m = functools.reduce(jnp.maximum, [jnp.max(s, axis=-1, keepdims=True) for s in ss])
        es = [jnp.exp(s - m) for s in ss]
        inv = 1.0 / functools.reduce(jnp.add, [jnp.sum(e, axis=-1, keepdims=True) for e in es])
        o = functools.reduce(jnp.add, [
            _dot((e * inv).astype(BF16), v_ref[0, :, h * MLA_V:(h + 1) * MLA_V])
            for e, v_ref in zip(es, v_refs)])
        outs.append(o)
    o_ref[0] = jnp.concatenate(outs, axis=1)


def _attn(q, ks, vs, tq):
    b, s, _ = q.shape
    nseg = len(ks)
    in_specs = [pl.BlockSpec((1, tq, MLA_H * LANE), lambda i, j: (i, j, 0))]
    in_specs += [pl.BlockSpec((1,) + k.shape[1:], lambda i, j: (i, 0, 0)) for k in ks]
    in_specs += [pl.BlockSpec((1,) + v.shape[1:], lambda i, j: (i, 0, 0)) for v in vs]
    return pl.pallas_call(
        functools.partial(_attn_kernel, nseg=nseg),
        out_shape=jax.ShapeDtypeStruct((b, s, MLA_H * MLA_V), F32),
        grid=(b, s // tq),
        in_specs=in_specs,
        out_specs=pl.BlockSpec((1, tq, MLA_H * MLA_V), lambda i, j: (i, j, 0)),
        compiler_params=_cparams(("arbitrary", "arbitrary")),
    )(q, *ks, *vs)


def _conv3(x, w):
    s = x.shape[0]
    row = lax.broadcasted_iota(I32, x.shape, 0)
    prev = jnp.where(row == 0, 0.0, pltpu.roll(x, 1, 0))
    nxt = jnp.where(row == s - 1, 0.0, pltpu.roll(x, s - 1, 0))
    return prev * w[0:1] + x * w[1:2] + nxt * w[2:3]


def _sconv_kernel(b_ref, c_ref, x_ref, w_ref, o_ref):
    o_ref[...] = b_ref[...] * _conv3(c_ref[...] * x_ref[...], w_ref[...])


def _sconv(proj, w, row_off, nseq, seq):
    r0 = row_off // seq
    cb = COL_SC // LANE
    nc = MIX_W // LANE
    spec = lambda off: pl.BlockSpec((seq, LANE), lambda b, j: (r0 + b, cb + off + j))
    return pl.pallas_call(
        _sconv_kernel,
        out_shape=jax.ShapeDtypeStruct((nseq * seq, MIX_W), F32),
        grid=(nseq, nc),
        in_specs=[spec(0), spec(nc), spec(2 * nc), pl.BlockSpec((3, LANE), lambda b, j: (0, j))],
        out_specs=pl.BlockSpec((seq, LANE), lambda b, j: (b, j)),
        compiler_params=_cparams(("arbitrary", "arbitrary")),
    )(proj, proj, proj, w)


def _ret_kernel(*refs, seq, rope, has_s0):
    it = iter(refs)
    q_ref, k_ref, v_ref, g_ref, dec_ref, gn_ref = (next(it) for _ in range(6))
    if rope:
        c_ref, s_ref = next(it), next(it)
    if has_s0:
        s0_ref = next(it)
    y_ref, sfin_ref, of_scr, ob_scr = (next(it) for _ in range(4))
    nch = seq // CHUNK

    q = q_ref[...]
    k = k_ref[...] * (RET_DK ** -0.5)
    if rope:
        c, sn = c_ref[...], s_ref[...]
        q = q * c + pltpu.roll(q, RET_DK // 2, 1) * sn
        k = k * c + pltpu.roll(k, RET_DK // 2, 1) * sn
    v = v_ref[...]

    ti = lax.broadcasted_iota(I32, (CHUNK, CHUNK), 0)
    si = lax.broadcasted_iota(I32, (CHUNK, CHUNK), 1)
    pos = lax.broadcasted_iota(I32, (CHUNK, 1), 0).astype(F32)
    for d in range(2):
        lg = -jnp.exp(dec_ref[d, 0, 0:1, 0:1])
        if d == 0:
            dist = (ti - si).astype(F32)
            qpow, kpow = pos + 1.0, (CHUNK - 1.0) - pos
        else:
            dist = (si - ti).astype(F32)
            qpow, kpow = CHUNK - pos, pos
        decay = jnp.where(dist >= 0, jnp.exp(jnp.maximum(dist, 0.0) * lg), 0.0)
        qs = jnp.exp(qpow * lg)
        ks = jnp.exp(kpow * lg)
        gtot = jnp.exp(CHUNK * lg)
        st = s0_ref[0, 0, d, 0] if has_s0 else jnp.zeros((RET_DK, RET_DK), F32)
        o_scr = of_scr if d == 0 else ob_scr
        order = range(nch) if d == 0 else range(nch - 1, -1, -1)
        for n in order:
            sl = slice(n * CHUNK, (n + 1) * CHUNK)
            qn, kn, vn = q[sl], k[sl], v[sl]
            vb = vn.astype(BF16)
            a = _dot_nt(qn.astype(BF16), kn.astype(BF16)) * decay
            o = _dot(a.astype(BF16), vb) + _dot((qn * qs).astype(BF16), st.astype(BF16))
            st = gtot * st + _dot_tn((kn * ks).astype(BF16), vb)
            o_scr[sl, :] = o
        sfin_ref[0, d, 0] = st

    o = of_scr[...] + ob_scr[...]
    dlt = o - jnp.mean(o, axis=-1, keepdims=True)
    y = dlt * lax.rsqrt(jnp.mean(dlt * dlt, axis=-1, keepdims=True) + EPS) * gn_ref[...]
    y_ref[...] = _silu(g_ref[...]) * y


def _retention(proj, dec_b, gn, row_off, nseq, seq, rope_tabs, s0, layer):
    r0 = row_off // seq
    cb = COL_RET // LANE
    spec = lambda off: pl.BlockSpec((seq, LANE), lambda b, h: (r0 + b, cb + off + h))
    in_specs = [spec(0), spec(RET_H), spec(2 * RET_H), spec(3 * RET_H),
                pl.BlockSpec((2, 1, 8, LANE), lambda b, h: (0, h, 0, 0)),
                pl.BlockSpec((1, LANE), lambda b, h: (0, h))]
    args = [proj, proj, proj, proj, dec_b, gn.reshape(1, RET_H * RET_DK)]
    if rope_tabs is not None:
        in_specs += [pl.BlockSpec((seq, LANE), lambda b, h: (0, 0))] * 2
        args += list(rope_tabs)
    if s0 is not None:
        in_specs.append(pl.BlockSpec((1, 1, 2, 1, RET_DK, RET_DK), lambda b, h: (b, layer, 0, h, 0, 0)))
        args.append(s0)
    return pl.pallas_call(
        functools.partial(_ret_kernel, seq=seq, rope=rope_tabs is not None, has_s0=s0 is not None),
        out_shape=(jax.ShapeDtypeStruct((nseq * seq, RET_H * RET_DK), F32),
                   jax.ShapeDtypeStruct((nseq, 2, RET_H, RET_DK, RET_DK), F32)),
        grid=(nseq, RET_H),
        in_specs=in_specs,
        out_specs=(pl.BlockSpec((seq, LANE), lambda b, h: (b, h)),
                   pl.BlockSpec((1, 2, 1, RET_DK, RET_DK), lambda b, h: (b, 0, h, 0, 0))),
        scratch_shapes=[pltpu.VMEM((seq, RET_DK), F32), pltpu.VMEM((seq, RET_DK), F32)],
        compiler_params=_cparams(("arbitrary", "arbitrary")),
    )(*args)


def _softplus(x):
    return jnp.maximum(x, 0.0) + jnp.log1p(jnp.exp(-jnp.abs(x)))


def _l2n(x):
    return x * lax.rsqrt(jnp.sum(x * x, axis=-1, keepdims=True) + EPS)


def _dn_kernel(*refs, seq, has_s0):
    it = iter(refs)
    q_ref, k_ref, v_ref, z_ref, ab_ref, wq_ref, wk_ref, wv_ref, alog_ref, dtb_ref, nrm_ref = (
        next(it) for _ in range(11))
    if has_s0:
        s0_ref = next(it)
    y_ref, sfin_ref = next(it), next(it)
    of_scr, ob_scr, q_scr, k_scr, v_scr, uv_scr, wk_scr, qk_scr, qd_scr, kd_scr, gt_scr = (
        next(it) for _ in range(11))
    nch = seq // CHUNK
    h = pl.program_id(1)

    q_scr[...] = _l2n(_silu(_conv3(q_ref[...], wq_ref[...]))) * (DN_DK ** -0.5)
    k_scr[...] = _l2n(_silu(_conv3(k_ref[...], wk_ref[...])))
    v_scr[...] = _silu(_conv3(v_ref[...], wv_ref[...]))

    ti = lax.broadcasted_iota(I32, (CHUNK, CHUNK), 0)
    si = lax.broadcasted_iota(I32, (CHUNK, CHUNK), 1)
    eye = ti == si
    eye_f = eye.astype(F32)
    lane = lax.broadcasted_iota(I32, (1, 2 * LANE), 1)

    def ab_col(blk, idx):
        return jnp.sum(jnp.where(lane == AB_LANE + idx, blk, 0.0), axis=1, keepdims=True)

    for d in range(2):
        incl = (si <= ti) if d == 0 else (si >= ti)
        strict = (si < ti) if d == 0 else (si > ti)
        incl_t = (ti <= si) if d == 0 else (ti >= si)
        neg_a = -jnp.exp(alog_ref[d, 0, 0:1, 0:1])
        dtb = dtb_ref[d, 0, 0:1, 0:1]

        def prep(n, carry, d=d, incl=incl, strict=strict, incl_t=incl_t, neg_a=neg_a, dtb=dtb):
            r = pl.multiple_of(n * CHUNK, CHUNK)
            rows = pl.ds(r, CHUNK)
            blk = ab_ref[rows, :]
            la = neg_a * _softplus(ab_col(blk, d * DN_H + h) + dtb)
            beta = _sigmoid(ab_col(blk, 2 * DN_H + d * DN_H + h))
            g_row = jnp.sum(jnp.where(incl_t, la, 0.0), axis=0, keepdims=True)
            g_col = jnp.sum(jnp.where(eye, g_row, 0.0), axis=1, keepdims=True)
            g_last = g_row[:, CHUNK - 1:CHUNK] if d == 0 else g_row[:, 0:1]
            decay = jnp.where(incl, jnp.exp(jnp.where(incl, g_col - g_row, 0.0)), 0.0)
            qn, kn, vn = q_scr[rows, :], k_scr[rows, :], v_scr[rows, :]
            kb = kn.astype(BF16)
            kk = _dot_nt(kb, kb)
            nmat = jnp.where(strict, -(beta * decay * kk), 0.0)
            tinv = eye_f + nmat
            pw = nmat
            for _ in range(5):
                pw = _dot3(pw, pw)
                tinv = tinv + _dot3(tinv, pw)
            eg = jnp.exp(g_col)
            uv_scr[d, rows, :] = _dot3(tinv, beta * vn)
            wk_scr[d, rows, :] = _dot3(tinv, (beta * eg) * kn).astype(BF16)
            qk_scr[d, rows, :] = (_dot_nt(qn.astype(BF16), kb) * decay).astype(BF16)
            qd_scr[d, rows, :] = (eg * qn).astype(BF16)
            kd_scr[d, rows, :] = (jnp.exp(g_last - g_col) * kn).astype(BF16)
            gt_scr[d, pl.ds(pl.multiple_of(n * 8, 8), 8), :] = jnp.broadcast_to(jnp.exp(g_last), (8, LANE))
            return carry

        lax.fori_loop(0, nch, prep, 0)

    def step(i, carry):
        sts = list(carry)
        for d in range(2):
            n = i if d == 0 else nch - 1 - i
            rows = pl.ds(pl.multiple_of(n * CHUNK, CHUNK), CHUNK)
            st = sts[d]
            sb = st.astype(BF16)
            u = uv_scr[d, rows, :] - _dot(wk_scr[d, rows, :], sb)
            ub = u.astype(BF16)
            o = _dot(qd_scr[d, rows, :], sb) + _dot(qk_scr[d, rows, :], ub)
            gt = gt_scr[d, pl.ds(pl.multiple_of(n * 8, 8), 8), :][0:1, 0:1]
            sts[d] = gt * st + _dot_tn(kd_scr[d, rows, :], ub)
            if d == 0:
                of_scr[rows, :] = o
            else:
                ob_scr[rows, :] = o
        return tuple(sts)

    if has_s0:
        init = (s0_ref[0, 0, 0, 0], s0_ref[0, 0, 1, 0])
    else:
        init = (jnp.zeros((DN_DK, DN_DK), F32), jnp.zeros((DN_DK, DN_DK), F32))
    fin = lax.fori_loop(0, nch, step, init)
    sfin_ref[0, 0, 0] = fin[0]
    sfin_ref[0, 1, 0] = fin[1]

    o = of_scr[...] + ob_scr[...]
    y = o * lax.rsqrt(jnp.mean(o * o, axis=-1, keepdims=True) + EPS) * nrm_ref[...]
    y_ref[...] = y * _silu(z_ref[...])


def _deltanet(proj, conv_w, alog_b, dtb_b, nrm, row_off, nseq, seq, s0, layer):
    r0 = row_off // seq
    cb = COL_DNQKV // LANE
    zb = COL_DNZ // LANE
    spec = lambda off: pl.BlockSpec((seq, LANE), lambda b, h: (r0 + b, off + h))
    wspec = lambda off: pl.BlockSpec((3, LANE), lambda b, h: (0, off + h))
    bspec = pl.BlockSpec((2, 1, 8, LANE), lambda b, h: (0, h, 0, 0))
    in_specs = [spec(cb), spec(cb + DN_H), spec(cb + 2 * DN_H), spec(zb),
                pl.BlockSpec((seq, 2 * LANE), lambda b, h: (r0 + b, COL_KV // (2 * LANE))),
                wspec(0), wspec(DN_H), wspec(2 * DN_H), bspec, bspec,
                pl.BlockSpec((1, DN_DK), lambda b, h: (0, 0))]
    args = [proj, proj, proj, proj, proj, conv_w, conv_w, conv_w, alog_b, dtb_b, nrm.reshape(1, DN_DK)]
    if s0 is not None:
        in_specs.append(pl.BlockSpec((1, 1, 2, 1, DN_DK, DN_DK), lambda b, h: (b, layer, 0, h, 0, 0)))
        args.append(s0)
    nch = seq // CHUNK
    scratch = [pltpu.VMEM((seq, DN_DK), F32), pltpu.VMEM((seq, DN_DK), F32),
               pltpu.VMEM((seq, DN_DK), F32), pltpu.VMEM((seq, DN_DK), F32), pltpu.VMEM((seq, DN_DK), F32),
               pltpu.VMEM((2, seq, DN_DK), F32), pltpu.VMEM((2, seq, DN_DK), BF16),
               pltpu.VMEM((2, seq, CHUNK), BF16), pltpu.VMEM((2, seq, DN_DK), BF16),
               pltpu.VMEM((2, seq, DN_DK), BF16), pltpu.VMEM((2, nch * 8, LANE), F32)]
    return pl.pallas_call(
        functools.partial(_dn_kernel, seq=seq, has_s0=s0 is not None),
        out_shape=(jax.ShapeDtypeStruct((nseq * seq, DN_H * DN_DK), F32),
                   jax.ShapeDtypeStruct((nseq, 2, DN_H, DN_DK, DN_DK), F32)),
        grid=(nseq, DN_H),
        in_specs=in_specs,
        out_specs=(pl.BlockSpec((seq, LANE), lambda b, h: (b, h)),
                   pl.BlockSpec((1, 2, 1, DN_DK, DN_DK), lambda b, h: (b, 0, h, 0, 0))),
        scratch_shapes=scratch,
        compiler_params=_cparams(("arbitrary", "arbitrary")),
    )(*args)


def _branch_out_kernel(b0, b1, b2, b3, g0, g1, g2, g3, x_ref, mod_ref, nf_ref, wb_ref, wo_ref,
                       xo_ref, h_ref):
    acc = None
    for n, (br, gl) in enumerate(((b0, g0), (b1, g1), (b2, g2), (b3, g3))):
        p = _sigmoid(gl[...]) * _dot(br[...].astype(BF16), wb_ref[n])
        acc = p if acc is None else acc + p
    y = _dot(acc.astype(BF16), wo_ref[...])
    m = mod_ref[0]
    x = x_ref[...] + m[:, 2 * D:3 * D] * y
    xo_ref[...] = x
    h_ref[...] = _rms(x, nf_ref[...]) * (1.0 + m[:, 4 * D:5 * D]) + m[:, 3 * D:4 * D]


def _branch_out(branches, proj, x, mod3, nf, wb, wo, grp_of_tile, tm):
    t = x.shape[0]
    gb = COL_GATE // D
    bspec = pl.BlockSpec((tm, MIX_W), lambda i: (i, 0))
    gspec = lambda n: pl.BlockSpec((tm, D), lambda i: (i, gb + n))
    row = pl.BlockSpec((tm, D), lambda i: (i, 0))
    return pl.pallas_call(
        _branch_out_kernel,
        out_shape=(jax.ShapeDtypeStruct((t, D), F32), jax.ShapeDtypeStruct((t, D), F32)),
        grid=(t // tm,),
        in_specs=[bspec] * 4 + [gspec(n) for n in range(4)] + [
            row,
            pl.BlockSpec((1, 1, 6 * D), lambda i: (grp_of_tile(tm)(i), 0, 0)),
            pl.BlockSpec((1, D), lambda i: (0, 0)),
            pl.BlockSpec((4, MIX_W, D), lambda i: (0, 0, 0)),
            pl.BlockSpec((D, D), lambda i: (0, 0))],
        out_specs=(row, row),
        compiler_params=_cparams(("arbitrary",)),
    )(*branches, proj, proj, proj, proj, x, mod3, nf.reshape(1, D), wb, wo)


def _route_kernel(h_ref, wr_ref, bias_ref, idx_ref, w_ref, rank_ref, cnt_ref, run_scr, *, tm):
    @pl.when(pl.program_id(0) == 0)
    def _():
        run_scr[...] = jnp.zeros_like(run_scr)

    neg = -jnp.inf
    gsz = N_EXP // N_GRP
    scores = _sigmoid(_dot3_nt(wr_ref[...], h_ref[...]))
    choice = scores + bias_ref[:, 0:1]
    row8 = lax.broadcasted_iota(I32, (gsz, tm), 0)
    gscore = []
    for g in range(N_GRP):
        blk = choice[g * gsz:(g + 1) * gsz]
        m1 = jnp.max(blk, axis=0, keepdims=True)
        i1 = jnp.min(jnp.where(blk == m1, row8, gsz), axis=0, keepdims=True)
        m2 = jnp.max(jnp.where(row8 == i1, neg, blk), axis=0, keepdims=True)
        gscore.append(m1 + m2)
    masked = []
    for g in range(N_GRP):
        rank = jnp.zeros((1, tm), I32)
        for g2 in range(N_GRP):
            if g2 == g:
                continue
            ahead = (gscore[g2] >= gscore[g]) if g2 < g else (gscore[g2] > gscore[g])
            rank = rank + ahead.astype(I32)
        masked.append(jnp.where(rank < TOPK_GRP, choice[g * gsz:(g + 1) * gsz], neg))
    cur = jnp.concatenate(masked, axis=0)
    row = lax.broadcasted_iota(I32, (N_EXP, tm), 0)
    sel = jnp.zeros((N_EXP, tm), jnp.bool_)
    idxs, scs = [], []
    for _ in range(TOP_K):
        m = jnp.max(cur, axis=0, keepdims=True)
        ik = jnp.min(jnp.where(cur == m, row, N_EXP), axis=0, keepdims=True)
        hit = row == ik
        scs.append(jnp.sum(jnp.where(hit, scores, 0.0), axis=0, keepdims=True))
        idxs.append(ik)
        cur = jnp.where(hit, neg, cur)
        sel = sel | hit
    tot = functools.reduce(jnp.add, scs)
    self_f = jnp.where(sel, 1.0, 0.0)
    tri = (lax.broadcasted_iota(I32, (tm, tm), 0) <= lax.broadcasted_iota(I32, (tm, tm), 1))
    csum = _dot(self_f.astype(BF16), jnp.where(tri, 1.0, 0.0).astype(BF16))
    run = run_scr[:, 0:1]
    rank_all = run + csum - self_f
    ranks = [jnp.sum(jnp.where(row == ik, rank_all, 0.0), axis=0, keepdims=True) for ik in idxs]
    idx_ref[...] = jnp.concatenate(idxs, axis=0)
    w_ref[...] = jnp.concatenate([s / tot * ROUTED_SCALE for s in scs], axis=0)
    rank_ref[...] = jnp.concatenate(ranks, axis=0).astype(I32)
    new_run = run + csum[:, tm - 1:tm]
    run_scr[...] = jnp.broadcast_to(new_run, run_scr.shape)
    cnt_ref[...] = jnp.broadcast_to(new_run, cnt_ref.shape).astype(I32)


def _route(h, wr_t, bias_b, tm):
    t = h.shape[0]
    tok = pl.BlockSpec((TOP_K, tm), lambda i: (0, i))
    return pl.pallas_call(
        functools.partial(_route_kernel, tm=tm),
        out_shape=(jax.ShapeDtypeStruct((TOP_K, t), I32), jax.ShapeDtypeStruct((TOP_K, t), F32),
                   jax.ShapeDtypeStruct((TOP_K, t), I32), jax.ShapeDtypeStruct((N_EXP, LANE), I32)),
        grid=(t // tm,),
        in_specs=[pl.BlockSpec((tm, D), lambda i: (i, 0)),
                  pl.BlockSpec((N_EXP, D), lambda i: (0, 0)),
                  pl.BlockSpec((N_EXP, LANE), lambda i: (0, 0))],
        out_specs=(tok, tok, tok, pl.BlockSpec((N_EXP, LANE), lambda i: (0, 0))),
        scratch_shapes=[pltpu.VMEM((N_EXP, LANE), F32)],
        compiler_params=_cparams(("arbitrary",)),
    )(h, wr_t, bias_b)


def _dispatch_kernel(pstart_ref, padded_ref, idx_ref, rank_ref, h_ref, xs_ref, zero_scr, sem, zsem, *, tm):
    def zero_copy(e):
        start = pl.multiple_of(pstart_ref[e] + padded_ref[e] - MOE_BM, MOE_BM)
        return pltpu.make_async_copy(zero_scr, xs_ref.at[pl.ds(start, MOE_BM)], zsem)

    @pl.when(pl.program_id(0) == 0)
    def _():
        zero_scr[...] = jnp.zeros_like(zero_scr)

        def zstart(e, c):
            @pl.when(padded_ref[e] > 0)
            def _():
                zero_copy(e).start()
            return c

        def zwait(e, c):
            @pl.when(padded_ref[e] > 0)
            def _():
                zero_copy(e).wait()
            return c

        lax.fori_loop(0, N_EXP, zstart, 0)
        lax.fori_loop(0, N_EXP, zwait, 0)

    def row_copy(t, k):
        pos = pstart_ref[idx_ref[k, t]] + rank_ref[k, t]
        return pltpu.make_async_copy(h_ref.at[pl.ds(t, 1)], xs_ref.at[pl.ds(pos, 1)], sem)

    def issue(t, c):
        for k in range(TOP_K):
            row_copy(t, k).start()
        return c

    def drain(t, c):
        for k in range(TOP_K):
            row_copy(t, k).wait()
        return c

    lax.fori_loop(0, tm, issue, 0)
    lax.fori_loop(0, tm, drain, 0)


def _dispatch(pstart, padded, idx_t, rank_t, h, n_slots, tm):
    t = h.shape[0]
    smem_tok = pl.BlockSpec((TOP_K, tm), lambda i, *_: (0, i), memory_space=pltpu.SMEM)
    return pl.pallas_call(
        functools.partial(_dispatch_kernel, tm=tm),
        out_shape=jax.ShapeDtypeStruct((n_slots, D), F32),
        grid_spec=pltpu.PrefetchScalarGridSpec(
            num_scalar_prefetch=2,
            grid=(t // tm,),
            in_specs=[smem_tok, smem_tok, pl.BlockSpec((tm, D), lambda i, *_: (i, 0))],
            out_specs=pl.BlockSpec(memory_space=pl.ANY),
            scratch_shapes=[pltpu.VMEM((MOE_BM, D), F32), pltpu.SemaphoreType.DMA(()),
                            pltpu.SemaphoreType.DMA(())]),
        compiler_params=_cparams(("arbitrary",)),
    )(pstart, padded, idx_t, rank_t, h)


def _experts_kernel(be_ref, nu_ref, x_ref, wg_ref, wu_ref, wd_ref, y_ref):
    @pl.when(pl.program_id(0) < nu_ref[0])
    def _():
        x = x_ref[...].astype(BF16)
        a = _silu(_dot(x, wg_ref[0].astype(BF16))) * _dot(x, wu_ref[0].astype(BF16))
        y_ref[...] = _dot(a.astype(BF16), wd_ref[0].astype(BF16))


def _experts(block_e, n_used, xs, wg, wu, wd):
    nb = xs.shape[0] // MOE_BM
    blk = lambda b, be, nu: (jnp.minimum(b, nu[0] - 1), 0)
    wsel = lambda b, be, nu: (be[b], 0, 0)
    return pl.pallas_call(
        _experts_kernel,
        out_shape=jax.ShapeDtypeStruct(xs.shape, F32),
        grid_spec=pltpu.PrefetchScalarGridSpec(
            num_scalar_prefetch=2,
            grid=(nb,),
            in_specs=[pl.BlockSpec((MOE_BM, D), blk),
                      pl.BlockSpec((1, D, D_EXP), wsel),
                      pl.BlockSpec((1, D, D_EXP), wsel),
                      pl.BlockSpec((1, D_EXP, D), wsel)],
            out_specs=pl.BlockSpec((MOE_BM, D), blk)),
        compiler_params=_cparams(("arbitrary",)),
    )(block_e, n_used, xs, wg, wu, wd)


def _combine_kernel(*refs, tm, final):
    it = iter(refs)
    pstart_ref, idx_ref, rank_ref, w_ref, h_ref, x_ref, mod_ref, wsg_ref, wsu_ref, wsd_ref = (
        next(it) for _ in range(10))
    if final:
        fn_ref = next(it)
    ys_ref = next(it)
    xo_ref = next(it)
    if final:
        yo_ref = next(it)
    buf, sem = next(it), next(it)

    def row_copy(t, k):
        pos = pstart_ref[idx_ref[k, t]] + rank_ref[k, t]
        return pltpu.make_async_copy(ys_ref.at[pl.ds(pos, 1)], buf.at[k, pl.ds(t, 1)], sem)

    def issue(t, c):
        for k in range(TOP_K):
            row_copy(t, k).start()
        return c

    def drain(t, c):
        for k in range(TOP_K):
            row_copy(t, k).wait()
        return c

    lax.fori_loop(0, tm, issue, 0)
    hb = h_ref[...].astype(BF16)
    a = _silu(_dot(hb, wsg_ref[...])) * _dot(hb, wsu_ref[...])
    shared = _dot(a.astype(BF16), wsd_ref[...])
    lax.fori_loop(0, tm, drain, 0)
    w = w_ref[...]
    routed = buf[0] * w[:, 0:1]
    for k in range(1, TOP_K):
        routed = routed + buf[k] * w[:, k:k + 1]
    m = mod_ref[0]
    x = x_ref[...] + m[:, 5 * D:6 * D] * (routed + shared)
    xo_ref[...] = x
    if final:
        yo_ref[...] = _rms(x, fn_ref[...])


def _combine(pstart, idx_t, rank_t, w_tok, h, x, mod3, wsg, wsu, wsd, fn, ys, grp_of_tile, tm):
    t = h.shape[0]
    final = fn is not None
    smem_tok = pl.BlockSpec((TOP_K, tm), lambda i, *_: (0, i), memory_space=pltpu.SMEM)
    row = pl.BlockSpec((tm, D), lambda i, *_: (i, 0))
    full = lambda shp: pl.BlockSpec(shp, lambda i, *_: (0,) * len(shp))
    in_specs = [smem_tok, smem_tok, pl.BlockSpec((tm, TOP_K), lambda i, *_: (i, 0)), row, row,
                pl.BlockSpec((1, 1, 6 * D), lambda i, *_: (grp_of_tile(tm)(i), 0, 0)),
                full((D, D_EXP)), full((D, D_EXP)), full((D_EXP, D))]
    args = [pstart, idx_t, rank_t, w_tok, h, x, mod3, wsg, wsu, wsd]
    if final:
        in_specs.append(full((1, D)))
        args.append(fn.reshape(1, D))
    in_specs.append(pl.BlockSpec(memory_space=pl.ANY))
    args.append(ys)
    out_shape = [jax.ShapeDtypeStruct((t, D), F32)]
    out_specs = [row]
    if final:
        out_shape.append(jax.ShapeDtypeStruct((t, D), F32))
        out_specs.append(row)
    return pl.pallas_call(
        functools.partial(_combine_kernel, tm=tm, final=final),
        out_shape=tuple(out_shape),
        grid_spec=pltpu.PrefetchScalarGridSpec(
            num_scalar_prefetch=1,
            grid=(t // tm,),
            in_specs=in_specs,
            out_specs=tuple(out_specs),
            scratch_shapes=[pltpu.VMEM((TOP_K, tm, D), F32), pltpu.SemaphoreType.DMA(())]),
        compiler_params=_cparams(("arbitrary",)),
    )(*args)


def _pack_w_in(w):
    cuts = np.cumsum([0, 256, 160, 1536, 512, 16, 2048, 1536, 4096])
    seg = lambda i: w[:, cuts[i]:cuts[i + 1]]
    pad = jnp.zeros((D, 2 * LANE - 160 - 16), w.dtype)
    return jnp.concatenate([seg(0), seg(1), seg(4), pad, seg(2), seg(3), seg(5), seg(6), seg(7)],
                           axis=1).astype(BF16)


def _pack_w_uq(w):
    w = w.reshape(MLA_QR, MLA_H, MLA_NOPE + MLA_ROPE)
    w = jnp.pad(w, ((0, 0), (0, 0), (0, LANE - MLA_NOPE - MLA_ROPE)))
    return w.reshape(MLA_QR, MLA_H * LANE).astype(BF16)


def _pack_w_ukv(w):
    w = w.reshape(MLA_KVR, MLA_H, MLA_NOPE + MLA_V)
    k_nope = jnp.pad(w[:, :, :MLA_NOPE], ((0, 0), (0, 0), (0, LANE - MLA_NOPE)))
    eye = jnp.eye(MLA_ROPE, dtype=w.dtype)[:, None, :]
    k_pe = jnp.pad(jnp.broadcast_to(eye, (MLA_ROPE, MLA_H, MLA_ROPE)),
                   ((0, LANE - MLA_ROPE), (0, 0), (MLA_NOPE, LANE - MLA_NOPE - MLA_ROPE)))
    wk = jnp.concatenate([k_nope, k_pe], axis=0).reshape(2 * LANE, MLA_H * LANE)
    wv = w[:, :, MLA_NOPE:].reshape(MLA_KVR, MLA_H * MLA_V)
    return wk.astype(BF16), wv.astype(BF16)


def _axial_angles(n_tok, dim):
    nf = dim // 4
    inv = ROPE_BASE ** (-jnp.arange(nf, dtype=F32) / nf)
    r = jnp.repeat(jnp.arange(n_tok // GRID_W, dtype=F32), GRID_W)
    cc = jnp.tile(jnp.arange(GRID_W, dtype=F32), n_tok // GRID_W)
    ang = jnp.concatenate([r[:, None] * inv, cc[:, None] * inv], axis=-1)
    return jnp.cos(ang), jnp.sin(ang)


def _rope_tables_mla(n_tok, lane0):
    cos, sin = _axial_angles(n_tok, MLA_ROPE)
    half = MLA_ROPE // 2
    z = lambda w: jnp.zeros((n_tok, w), F32)
    o = lambda w: jnp.ones((n_tok, w), F32)
    rest = LANE - lane0 - MLA_ROPE
    c = jnp.concatenate([o(lane0), cos, cos, o(rest)], axis=1)
    s1 = jnp.concatenate([z(lane0), -sin, z(half), z(rest)], axis=1)
    s2 = jnp.concatenate([z(lane0), z(half), sin, z(rest)], axis=1)
    return c, s1, s2


def _rope_tables_ret(n_tok):
    cos, sin = _axial_angles(n_tok, RET_DK)
    return jnp.concatenate([cos, cos], axis=1), jnp.concatenate([-sin, sin], axis=1)


def _bcast_dh(a):
    return jnp.broadcast_to(a.astype(F32)[:, :, None, None], a.shape + (8, LANE))


def _forward(x_prompt, x_sample, c, cache_ckv, cache_kpe, state_dn, state_ret, c_ctx, P, tiles):
    nb, sc, _ = x_prompt.shape
    nl, sl, _ = x_sample.shape
    past = cache_ckv.shape[2]
    depth = P['w_in'].shape[0]
    t_ctx, t_lat = nb * sc, nl * sl
    t = t_ctx + t_lat
    tm = tiles['tm']
    assert sc % tm == 0 and sl % tm == 0 and t_ctx % sl == 0

    assert sl % tiles['tc'] == 0 and t_ctx % tiles['tc'] == 0

    def grp_of_tile(rows_per_tile):
        nct = t_ctx // rows_per_tile
        return lambda i: jnp.where(i < nct, 0, 1 + (i - nct) // (sl // rows_per_tile))

    x = jnp.concatenate([x_prompt.reshape(t_ctx, D), x_sample.reshape(t_lat, D)], axis=0)
    ngrp = 1 + nl
    cvec = jnp.concatenate([c_ctx[None], c, jnp.zeros((-(ngrp) % 8, D), F32)], axis=0)

    tab_q = _rope_tables_mla(sl, MLA_NOPE)
    tab_k = _rope_tables_mla(sl, 0)
    tab_r = _rope_tables_ret(sl)

    ckv_l, kpe_l, dn_l, ret_l = [], [], [], []
    y_final = None
    for l in range(depth):
        mod3 = _ada(cvec, P['w_ada'][l], P['b_ada'][l]).reshape(cvec.shape[0], 1, 6 * D)
        proj = _in_proj(x, mod3, P['norm_mix'][l], _pack_w_in(P['w_in'][l]), grp_of_tile, tm, tiles['tn'])

        w_uq = _pack_w_uq(P['mla_w_uq'][l])
        wk, wv = _pack_w_ukv(P['mla_w_ukv'][l])
        gq, gkv = P['mla_q_norm'][l], P['mla_kv_norm'][l]
        q_c = _mla_q(proj, gq, w_uq, 0, t_ctx, tm, None, sc)
        q_l = _mla_q(proj, gq, w_uq, t_ctx, t_lat, tm, tab_q, sl)
        kvb = COL_KV // (2 * LANE)
        ckv_c, kpe_c, k_c, v_c = _mla_kv(proj, kvb, gkv, wk, wv, 0, t_ctx, tm, True, None, sc)
        _, _, k_l, v_l = _mla_kv(proj, kvb, gkv, wk, wv, t_ctx, t_lat, tm, True, tab_k, sl)
        cached = jnp.concatenate([cache_ckv[:, l], cache_kpe[:, l],
                                  jnp.zeros((nl, past, 2 * LANE - MLA_KVR - MLA_ROPE), F32)], axis=-1)
        pt = min(tm, past)
        _, _, k_p, v_p = _mla_kv(cached.reshape(nl * past, 2 * LANE), 0, gkv, wk, wv, 0, nl * past, pt,
                                 False, None, past)
        r3 = lambda a, n, s: a.reshape(n, s, a.shape[-1])
        y_mla_c = _attn(r3(q_c, nb, sc), [r3(k_c, nb, sc)], [r3(v_c, nb, sc)], min(sc, tiles['tq']))
        y_mla_l = _attn(r3(q_l, nl, sl), [r3(k_p, nl, past), r3(k_l, nl, sl)],
                        [r3(v_p, nl, past), r3(v_l, nl, sl)], min(sl, tiles['tq']))
        y_mla = jnp.concatenate([y_mla_c.reshape(t_ctx, MIX_W), y_mla_l.reshape(t_lat, MIX_W)], axis=0)

        alog_b, dtb_b = _bcast_dh(P['dn_A_log'][l]), _bcast_dh(P['dn_dt_bias'][l])
        dn_args = (proj, P['dn_conv'][l], alog_b, dtb_b, P['dn_norm'][l])
        y_dn_c, dn_fin = _deltanet(*dn_args, 0, nb, sc, None, l)
        y_dn_l, _ = _deltanet(*dn_args, t_ctx, nl, sl, state_dn, l)
        y_dn = jnp.concatenate([y_dn_c, y_dn_l], axis=0)

        dec_b = _bcast_dh(P['ret_decay'][l])
        y_ret_c, ret_fin = _retention(proj, dec_b, P['ret_gn'][l], 0, nb, sc, None, None, l)
        y_ret_l, _ = _retention(proj, dec_b, P['ret_gn'][l], t_ctx, nl, sl, tab_r, state_ret, l)
        y_ret = jnp.concatenate([y_ret_c, y_ret_l], axis=0)

        y_sc = jnp.concatenate([_sconv(proj, P['sc_conv'][l], 0, nb, sc),
                                _sconv(proj, P['sc_conv'][l], t_ctx, nl, sl)], axis=0)

        x_mid, h2 = _branch_out((y_mla, y_dn, y_ret, y_sc), proj, x, mod3, P['norm_ffn'][l],
                                P['w_branch'][l].astype(BF16), P['w_out'][l].astype(BF16), grp_of_tile, tm)

        bias_b = jnp.broadcast_to(P['router_bias'][l].astype(F32)[:, None], (N_EXP, LANE))
        idx_t, w_t, rank_t, cnt = _route(h2, P['router'][l].T, bias_b, tiles['tr'])
        counts = cnt[:, 0]
        padded = (counts + MOE_BM - 1) // MOE_BM * MOE_BM
        pad_end = jnp.cumsum(padded)
        pstart = (pad_end - padded).astype(I32)
        n_blocks = (t * TOP_K) // MOE_BM + N_EXP
        block_e = jnp.minimum(jnp.searchsorted(pad_end, jnp.arange(n_blocks, dtype=I32) * MOE_BM, side='right'),
                              N_EXP - 1).astype(I32)
        n_used = (pad_end[-1:] // MOE_BM).astype(I32)
        xs = _dispatch(pstart, padded.astype(I32), idx_t, rank_t, h2, n_blocks * MOE_BM, tiles['td'])
        ys = _experts(block_e, n_used, xs, P['w_eg'][l], P['w_eu'][l], P['w_ed'][l])
        fn = P['final_norm'] if l == depth - 1 else None
        outs = _combine(pstart, idx_t, rank_t, w_t.T, h2, x_mid, mod3, P['w_sg'][l].astype(BF16),
                        P['w_su'][l].astype(BF16), P['w_sd'][l].astype(BF16), fn, ys, grp_of_tile, tiles['tc'])
        x = outs[0]
        if fn is not None:
            y_final = outs[1]

        ckv_l.append(ckv_c.reshape(nb, sc, MLA_KVR))
        kpe_l.append(kpe_c.reshape(nb, sc, MLA_ROPE))
        dn_l.append(dn_fin)
        ret_l.append(ret_fin)

    y_prompt = y_final[:t_ctx].reshape(nb, sc, D)
    y_sample = y_final[t_ctx:].reshape(nl, sl, D)
    return (y_prompt, y_sample, jnp.stack(ckv_l, axis=1), jnp.stack(kpe_l, axis=1),
            jnp.stack(dn_l, axis=1), jnp.stack(ret_l, axis=1))


_TILES = dict(tm=256, tn=1024, tq=256, tr=512, td=256, tc=128)


def kernel(x_prompt, x_sample, c, cache_ckv, cache_kpe, state_dn, state_ret, c_ctx, w_ada, b_ada, norm_mix, norm_ffn, w_in, mla_q_norm, mla_w_uq, mla_kv_norm, mla_w_ukv, dn_conv, dn_A_log, dn_dt_bias, dn_norm, ret_decay, ret_gn, sc_conv, w_branch, w_out, router, router_bias, w_eg, w_eu, w_ed, w_sg, w_su, w_sd, final_norm):
    P = dict(w_ada=w_ada, b_ada=b_ada, norm_mix=norm_mix, norm_ffn=norm_ffn, w_in=w_in,
             mla_q_norm=mla_q_norm, mla_w_uq=mla_w_uq, mla_kv_norm=mla_kv_norm, mla_w_ukv=mla_w_ukv,
             dn_conv=dn_conv, dn_A_log=dn_A_log, dn_dt_bias=dn_dt_bias, dn_norm=dn_norm,
             ret_decay=ret_decay, ret_gn=ret_gn, sc_conv=sc_conv, w_branch=w_branch, w_out=w_out,
             router=router, router_bias=router_bias, w_eg=w_eg, w_eu=w_eu, w_ed=w_ed,
             w_sg=w_sg, w_su=w_su, w_sd=w_sd, final_norm=final_norm)
    return _forward(x_prompt, x_sample, c, cache_ckv, cache_kpe, state_dn, state_ret, c_ctx, P, _TILES)
```

```python
import functools
import math

import jax
import jax.numpy as jnp
import numpy as np
from jax import lax
from jax.experimental import pallas as pl
from jax.experimental.pallas import tpu as pltpu

F32 = jnp.float32
BF16 = jnp.bfloat16
I32 = jnp.int32

D = 1024
EPS = 1e-6
CHUNK = 64
PAIR = 2 * CHUNK
GRID_W = 64
ROPE_BASE = 10000.0

MLA_H, MLA_NOPE, MLA_ROPE, MLA_V, MLA_QR, MLA_KVR = 8, 64, 32, 64, 256, 128
DN_H, DN_DK = 4, 128
RET_H, RET_DK = 4, 128
MIX_W = 512
N_EXP, TOP_K, N_GRP, TOPK_GRP, D_EXP = 64, 8, 8, 4, 256
ROUTED_SCALE = 2.5

LANE = 128
COL_CQ = 0
COL_KV = 256
N_PROJ_A = 512
AB_LANE = 160
COL_GATE = 0
COL_DNQKV = 4096
COL_DNZ = 5632
COL_RET = 6144
COL_SC = 8192
N_PROJ_B = 9728

MOE_BM = 256
VMEM_LIMIT = 56 * 1024 * 1024


def _cparams(sem, vmem=None):
    return pltpu.CompilerParams(dimension_semantics=sem, vmem_limit_bytes=vmem or VMEM_LIMIT)


def _dot(a, b):
    return jnp.dot(a, b, preferred_element_type=F32)


def _dot_nt(a, b):
    return lax.dot_general(a, b, (((1,), (1,)), ((), ())), preferred_element_type=F32)


def _dot_tn(a, b):
    return lax.dot_general(a, b, (((0,), (0,)), ((), ())), preferred_element_type=F32)


def _split(a):
    hi = a.astype(BF16)
    lo = (a - hi.astype(F32)).astype(BF16)
    return hi, lo


def _dot3(a, b):
    ah, al = _split(a)
    bh, bl = _split(b)
    return _dot(ah, bh) + (_dot(ah, bl) + _dot(al, bh))


def _dot3_nt(a, b):
    ah, al = _split(a)
    bh, bl = _split(b)
    return _dot_nt(ah, bh) + (_dot_nt(ah, bl) + _dot_nt(al, bh))


def _sigmoid(x):
    return 1.0 / (1.0 + jnp.exp(-x))


def _silu(x):
    return x * _sigmoid(x)


def _rms(x, g):
    return x * lax.rsqrt(jnp.mean(x * x, axis=-1, keepdims=True) + EPS) * g


def _ada_kernel(c_ref, w_ref, b_ref, o_ref):
    o_ref[...] = _dot3(_silu(c_ref[...]), w_ref[...]) + b_ref[...]


def _ada(cvec, w, b):
    n = w.shape[1]
    tn = 1024
    return pl.pallas_call(
        _ada_kernel,
        out_shape=jax.ShapeDtypeStruct((cvec.shape[0], n), F32),
        grid=(n // tn,),
        in_specs=[pl.BlockSpec(cvec.shape, lambda j: (0, 0)),
                  pl.BlockSpec((D, tn), lambda j: (0, j)),
                  pl.BlockSpec((1, tn), lambda j: (0, j))],
        out_specs=pl.BlockSpec((cvec.shape[0], tn), lambda j: (0, j)),
        compiler_params=_cparams(("arbitrary",)),
    )(cvec, w, b.reshape(1, n))


def _in_proj_kernel(x_ref, mod_ref, g_ref, w_ref, o_ref, h_scr):
    @pl.when(pl.program_id(1) == 0)
    def _():
        m = mod_ref[0]
        y = _rms(x_ref[...], g_ref[...])
        h_scr[...] = (y * (1.0 + m[:, D:2 * D]) + m[:, 0:D]).astype(BF16)

    o_ref[...] = _dot(h_scr[...], w_ref[...]).astype(o_ref.dtype)


def _in_proj(x, mod3, g, w, grp_of_tile, tm, tn, out_dtype):
    t = x.shape[0]
    n = w.shape[1]
    return pl.pallas_call(
        _in_proj_kernel,
        out_shape=jax.ShapeDtypeStruct((t, n), out_dtype),
        grid=(t // tm, n // tn),
        in_specs=[pl.BlockSpec((tm, D), lambda i, j: (i, 0)),
                  pl.BlockSpec((1, 1, 6 * D), lambda i, j: (grp_of_tile(tm)(i), 0, 0)),
                  pl.BlockSpec((1, D), lambda i, j: (0, 0)),
                  pl.BlockSpec((D, tn), lambda i, j: (0, j))],
        out_specs=pl.BlockSpec((tm, tn), lambda i, j: (i, j)),
        scratch_shapes=[pltpu.VMEM((tm, D), BF16)],
        compiler_params=_cparams(("arbitrary", "arbitrary")),
    )(x, mod3, g.reshape(1, D), w)


def _rope3(x, c, s1, s2, width):
    return x * c + pltpu.roll(x, width - 16, 1) * s1 + pltpu.roll(x, 16, 1) * s2


def _mla_q_kernel(*refs, rope):
    if rope:
        p_ref, g_ref, w_ref, c_ref, s1_ref, s2_ref, o_ref = refs
    else:
        p_ref, g_ref, w_ref, o_ref = refs
    y = _rms(p_ref[...], g_ref[...])
    q = _dot(y.astype(BF16), w_ref[...])
    if rope:
        tile = lambda r: jnp.concatenate([r[...]] * MLA_H, axis=1)
        q = _rope3(q, tile(c_ref), tile(s1_ref), tile(s2_ref), MLA_H * LANE)
    o_ref[...] = q.astype(BF16)


def _mla_q(proj, g, w, row_off, rows, tm, rope_tabs, seq):
    nrow = rows // tm
    r0 = row_off // tm
    in_specs = [pl.BlockSpec((tm, MLA_QR), lambda i: (r0 + i, COL_CQ // MLA_QR)),
                pl.BlockSpec((1, MLA_QR), lambda i: (0, 0)),
                pl.BlockSpec((MLA_QR, MLA_H * LANE), lambda i: (0, 0))]
    args = [proj, g.reshape(1, MLA_QR), w]
    if rope_tabs is not None:
        per = seq // tm
        in_specs += [pl.BlockSpec((tm, LANE), lambda i: (i % per, 0))] * 3
        args += list(rope_tabs)
    return pl.pallas_call(
        functools.partial(_mla_q_kernel, rope=rope_tabs is not None),
        out_shape=jax.ShapeDtypeStruct((rows, MLA_H * LANE), BF16),
        grid=(nrow,),
        in_specs=in_specs,
        out_specs=pl.BlockSpec((tm, MLA_H * LANE), lambda i: (i, 0)),
        compiler_params=_cparams(("arbitrary",)),
    )(*args)


def _mla_kv_kernel(*refs, norm, rope):
    if rope:
        p_ref, g_ref, wk_ref, wv_ref, c_ref, s1_ref, s2_ref, ckv_ref, kpe_ref, k_ref, v_ref = refs
    else:
        p_ref, g_ref, wk_ref, wv_ref, ckv_ref, kpe_ref, k_ref, v_ref = refs
    blk = p_ref[...]
    ckv = blk[:, :MLA_KVR]
    if norm:
        ckv = _rms(ckv, g_ref[...])
    kp = blk[:, MLA_KVR:]
    ckv_ref[...] = ckv
    kpe_ref[...] = kp[:, :MLA_ROPE]
    if rope:
        kp = _rope3(kp, c_ref[...], s1_ref[...], s2_ref[...], LANE)
    a = jnp.concatenate([ckv, kp], axis=1).astype(BF16)
    k_ref[...] = _dot(a, wk_ref[...]).astype(BF16)
    v_ref[...] = _dot(ckv.astype(BF16), wv_ref[...]).astype(BF16)


def _mla_kv(src, col_blk, g, wk, wv, row_off, rows, tm, norm, rope_tabs, seq):
    nrow = rows // tm
    r0 = row_off // tm
    in_specs = [pl.BlockSpec((tm, 2 * LANE), lambda i: (r0 + i, col_blk)),
                pl.BlockSpec((1, MLA_KVR), lambda i: (0, 0)),
                pl.BlockSpec((2 * LANE, MLA_H * LANE), lambda i: (0, 0)),
                pl.BlockSpec((MLA_KVR, MLA_H * MLA_V), lambda i: (0, 0))]
    args = [src, g.reshape(1, MLA_KVR), wk, wv]
    if rope_tabs is not None:
        per = seq // tm
        in_specs += [pl.BlockSpec((tm, LANE), lambda i: (i % per, 0))] * 3
        args += list(rope_tabs)
    return pl.pallas_call(
        functools.partial(_mla_kv_kernel, norm=norm, rope=rope_tabs is not None),
        out_shape=(jax.ShapeDtypeStruct((rows, MLA_KVR), F32),
                   jax.ShapeDtypeStruct((rows, MLA_ROPE), F32),
                   jax.ShapeDtypeStruct((rows, MLA_H * LANE), BF16),
                   jax.ShapeDtypeStruct((rows, MLA_H * MLA_V), BF16)),
        grid=(nrow,),
        in_specs=in_specs,
        out_specs=(pl.BlockSpec((tm, MLA_KVR), lambda i: (i, 0)),
                   pl.BlockSpec((tm, MLA_ROPE), lambda i: (i, 0)),
                   pl.BlockSpec((tm, MLA_H * LANE), lambda i: (i, 0)),
                   pl.BlockSpec((tm, MLA_H * MLA_V), lambda i: (i, 0))),
        compiler_params=_cparams(("arbitrary",)),
    )(*args)


def _attn_kernel(*refs, nseg):
    q_ref = refs[0]
    k_refs = refs[1:1 + nseg]
    v_refs = refs[1 + nseg:1 + 2 * nseg]
    o_ref = refs[1 + 2 * nseg]
    scale = (MLA_NOPE + MLA_ROPE) ** -0.5
    outs = []
    for h in range(MLA_H):
        qh = q_ref[0, :, h * LANE:(h + 1) * LANE]
        ss = [_dot_nt(qh, k_ref[0, :, h * LANE:(h + 1) * LANE]) * scale for k_ref in k_refs]
        m = functools.reduce(jnp.maximum, [jnp.max(s, axis=-1, keepdims=True) for s in ss])
        es = [jnp.exp(s - m) for s in ss]
        inv = 1.0 / functools.reduce(jnp.add, [jnp.sum(e, axis=-1, keepdims=True) for e in es])
        o = functools.reduce(jnp.add, [
            _dot((e * inv).astype(BF16), v_ref[0, :, h * MLA_V:(h + 1) * MLA_V])
            for e, v_ref in zip(es, v_refs)])
        outs.append(o)
    o_ref[0] = jnp.concatenate(outs, axis=1)


def _attn(q, ks, vs, tq):
    b, s, _ = q.shape
    nseg = len(ks)
    in_specs = [pl.BlockSpec((1, tq, MLA_H * LANE), lambda i, j: (i, j, 0))]
    in_specs += [pl.BlockSpec((1,) + k.shape[1:], lambda i, j: (i, 0, 0)) for k in ks]
    in_specs += [pl.BlockSpec((1,) + v.shape[1:], lambda i, j: (i, 0, 0)) for v in vs]
    return pl.pallas_call(
        functools.partial(_attn_kernel, nseg=nseg),
        out_shape=jax.ShapeDtypeStruct((b, s, MLA_H * MLA_V), F32),
        grid=(b, s // tq),
        in_specs=in_specs,
        out_specs=pl.BlockSpec((1, tq, MLA_H * MLA_V), lambda i, j: (i, j, 0)),
        compiler_params=_cparams(("arbitrary", "arbitrary")),
    )(q, *ks, *vs)


def _conv3(x, w):
    s = x.shape[0]
    row = lax.broadcasted_iota(I32, x.shape, 0)
    prev = jnp.where(row == 0, 0.0, pltpu.roll(x, 1, 0))
    nxt = jnp.where(row == s - 1, 0.0, pltpu.roll(x, s - 1, 0))
    return prev * w[0:1] + x * w[1:2] + nxt * w[2:3]


def _sconv_kernel(b_ref, c_ref, x_ref, w_ref, o_ref):
    f = lambda r: r[...].astype(F32)
    o_ref[...] = f(b_ref) * _conv3(f(c_ref) * f(x_ref), w_ref[...])


def _sconv(proj, w, row_off, nseq, seq):
    r0 = row_off // seq
    cb = COL_SC // LANE
    nc = MIX_W // LANE
    spec = lambda off: pl.BlockSpec((seq, LANE), lambda b, j: (r0 + b, cb + off + j))
    return pl.pallas_call(
        _sconv_kernel,
        out_shape=jax.ShapeDtypeStruct((nseq * seq, MIX_W), F32),
        grid=(nseq, nc),
        in_specs=[spec(0), spec(nc), spec(2 * nc), pl.BlockSpec((3, LANE), lambda b, j: (0, j))],
        out_specs=pl.BlockSpec((seq, LANE), lambda b, j: (b, j)),
        compiler_params=_cparams(("arbitrary", "arbitrary")),
    )(proj, proj, proj, w)


def _ret_kernel(*refs, seq, rope, has_s0):
    it = iter(refs)
    q_ref, k_ref, v_ref, g_ref, dec_ref, gn_ref = (next(it) for _ in range(6))
    if rope:
        c_ref, s_ref = next(it), next(it)
    if has_s0:
        s0_ref = next(it)
    y_ref, sfin_ref, of_scr, ob_scr = (next(it) for _ in range(4))
    nch = seq // CHUNK

    q = q_ref[...].astype(F32)
    k = k_ref[...].astype(F32) * (RET_DK ** -0.5)
    if rope:
        c, sn = c_ref[...], s_ref[...]
        q = q * c + pltpu.roll(q, RET_DK // 2, 1) * sn
        k = k * c + pltpu.roll(k, RET_DK // 2, 1) * sn
    v = v_ref[...].astype(F32)

    ti = lax.broadcasted_iota(I32, (CHUNK, CHUNK), 0)
    si = lax.broadcasted_iota(I32, (CHUNK, CHUNK), 1)
    pos = lax.broadcasted_iota(I32, (CHUNK, 1), 0).astype(F32)
    for d in range(2):
        lg = -jnp.exp(dec_ref[d, 0, 0:1, 0:1])
        if d == 0:
            dist = (ti - si).astype(F32)
            qpow, kpow = pos + 1.0, (CHUNK - 1.0) - pos
        else:
            dist = (si - ti).astype(F32)
            qpow, kpow = CHUNK - pos, pos
        decay = jnp.where(dist >= 0, jnp.exp(jnp.maximum(dist, 0.0) * lg), 0.0)
        qs = jnp.exp(qpow * lg)
        ks = jnp.exp(kpow * lg)
        gtot = jnp.exp(CHUNK * lg)
        st = s0_ref[0, 0, d, 0] if has_s0 else jnp.zeros((RET_DK, RET_DK), F32)
        o_scr = of_scr if d == 0 else ob_scr
        order = range(nch) if d == 0 else range(nch - 1, -1, -1)
        for n in order:
            sl = slice(n * CHUNK, (n + 1) * CHUNK)
            qn, kn, vn = q[sl], k[sl], v[sl]
            vb = vn.astype(BF16)
            a = _dot_nt(qn.astype(BF16), kn.astype(BF16)) * decay
            o = _dot(a.astype(BF16), vb) + _dot((qn * qs).astype(BF16), st.astype(BF16))
            st = gtot * st + _dot_tn((kn * ks).astype(BF16), vb)
            o_scr[sl, :] = o
        sfin_ref[0, d, 0] = st

    o = of_scr[...] + ob_scr[...]
    dlt = o - jnp.mean(o, axis=-1, keepdims=True)
    y = dlt * lax.rsqrt(jnp.mean(dlt * dlt, axis=-1, keepdims=True) + EPS) * gn_ref[...]
    y_ref[...] = _silu(g_ref[...].astype(F32)) * y


def _retention(proj, dec_b, gn, row_off, nseq, seq, rope_tabs, s0, layer):
    r0 = row_off // seq
    cb = COL_RET // LANE
    spec = lambda off: pl.BlockSpec((seq, LANE), lambda b, h: (r0 + b, cb + off + h))
    in_specs = [spec(0), spec(RET_H), spec(2 * RET_H), spec(3 * RET_H),
                pl.BlockSpec((2, 1, 8, LANE), lambda b, h: (0, h, 0, 0)),
                pl.BlockSpec((1, LANE), lambda b, h: (0, h))]
    args = [proj, proj, proj, proj, dec_b, gn.reshape(1, RET_H * RET_DK)]
    if rope_tabs is not None:
        in_specs += [pl.BlockSpec((seq, LANE), lambda b, h: (0, 0))] * 2
        args += list(rope_tabs)
    if s0 is not None:
        in_specs.append(pl.BlockSpec((1, 1, 2, 1, RET_DK, RET_DK), lambda b, h: (b, layer, 0, h, 0, 0)))
        args.append(s0)
    return pl.pallas_call(
        functools.partial(_ret_kernel, seq=seq, rope=rope_tabs is not None, has_s0=s0 is not None),
        out_shape=(jax.ShapeDtypeStruct((nseq * seq, RET_H * RET_DK), F32),
                   jax.ShapeDtypeStruct((nseq, 2, RET_H, RET_DK, RET_DK), F32)),
        grid=(nseq, RET_H),
        in_specs=in_specs,
        out_specs=(pl.BlockSpec((seq, LANE), lambda b, h: (b, h)),
                   pl.BlockSpec((1, 2, 1, RET_DK, RET_DK), lambda b, h: (b, 0, h, 0, 0))),
        scratch_shapes=[pltpu.VMEM((seq, RET_DK), F32), pltpu.VMEM((seq, RET_DK), F32)],
        compiler_params=_cparams(("arbitrary", "arbitrary")),
    )(*args)


def _softplus(x):
    return jnp.maximum(x, 0.0) + jnp.log1p(jnp.exp(-jnp.abs(x)))


def _l2n(x):
    return x * lax.rsqrt(jnp.sum(x * x, axis=-1, keepdims=True) + EPS)


def _dn_kernel(*refs, seq, has_s0):
    it = iter(refs)
    q_ref, k_ref, v_ref, z_ref, ab_ref, wq_ref, wk_ref, wv_ref, alog_ref, dtb_ref, nrm_ref = (
        next(it) for _ in range(11))
    if has_s0:
        s0_ref = next(it)
    y_ref, sfin_ref = next(it), next(it)
    of_scr, ob_scr, q_scr, k_scr, v_scr, qe_scr, oc_scr, m_scr, c_scr, gt_scr = (next(it) for _ in range(10))
    nch = seq // CHUNK
    npair = seq // PAIR
    h = pl.program_id(1)

    q_scr[...] = _l2n(_silu(_conv3(q_ref[...].astype(F32), wq_ref[...]))) * (DN_DK ** -0.5)
    k_scr[...] = _l2n(_silu(_conv3(k_ref[...].astype(F32), wk_ref[...])))
    v_scr[...] = _silu(_conv3(v_ref[...].astype(F32), wv_ref[...]))

    ti = lax.broadcasted_iota(I32, (PAIR, PAIR), 0)
    si = lax.broadcasted_iota(I32, (PAIR, PAIR), 1)
    same = (ti >> 6) == (si >> 6)
    s_in = si & (CHUNK - 1)
    eye = ti == si
    eye_f = eye.astype(F32)
    lane = lax.broadcasted_iota(I32, (1, 2 * LANE), 1)

    def ab_col(blk, idx):
        return jnp.sum(jnp.where(lane == AB_LANE + idx, blk, 0.0), axis=1, keepdims=True)

    def prep(p, carry):
        rows = pl.ds(pl.multiple_of(p * PAIR, PAIR), PAIR)
        blk = ab_ref[rows, :]
        qn, kn, vn = q_scr[rows, :], k_scr[rows, :], v_scr[rows, :]
        kb = kn.astype(BF16)
        kk = _dot_nt(kb, kb)
        qkr = _dot_nt(qn.astype(BF16), kb)
        for d in range(2):
            incl = same & ((si <= ti) if d == 0 else (si >= ti))
            strict = same & ((si < ti) if d == 0 else (si > ti))
            incl_t = same & ((ti <= si) if d == 0 else (ti >= si))
            last_s = same & (s_in == (CHUNK - 1 if d == 0 else 0))
            neg_a = -jnp.exp(alog_ref[d, 0, 0:1, 0:1])
            dtb = dtb_ref[d, 0, 0:1, 0:1]
            la = neg_a * _softplus(ab_col(blk, d * DN_H + h) + dtb)
            beta = _sigmoid(ab_col(blk, 2 * DN_H + d * DN_H + h))
            g_row = jnp.sum(jnp.where(incl_t, la, 0.0), axis=0, keepdims=True)
            g_col = jnp.sum(jnp.where(eye, g_row, 0.0), axis=1, keepdims=True)
            g_end = jnp.sum(jnp.where(last_s, g_row, 0.0), axis=1, keepdims=True)
            decay = jnp.where(incl, jnp.exp(jnp.where(incl, g_col - g_row, 0.0)), 0.0)
            nmat = jnp.where(strict, -(beta * decay * kk), 0.0)
            tinv = eye_f + nmat
            pw = nmat
            for _ in range(5):
                pw = _dot3(pw, pw)
                tinv = tinv + _dot3(tinv, pw)
            eg = jnp.exp(g_col)
            uvb = _dot3(tinv, beta * vn).astype(BF16)
            wkb = _dot3(tinv, (beta * eg) * kn).astype(BF16)
            qk = (qkr * decay).astype(BF16)
            kd = (jnp.exp(g_end - g_col) * kn).astype(BF16)
            qe_scr[d, rows, :] = (eg * qn - _dot(qk, wkb)).astype(BF16)
            oc_scr[d, rows, :] = _dot(qk, uvb)
            for c in range(2):
                sl = slice(c * CHUNK, (c + 1) * CHUNK)
                n = p * 2 + c
                mrows = pl.ds(pl.multiple_of(n * DN_DK, DN_DK), DN_DK)
                m_scr[d, mrows, :] = _dot_tn(kd[sl], wkb[sl]).astype(BF16)
                c_scr[d, mrows, :] = _dot_tn(kd[sl], uvb[sl])
                e = c * CHUNK + (CHUNK - 1 if d == 0 else 0)
                gt_scr[d, pl.ds(pl.multiple_of(n * 8, 8), 8), :] = jnp.broadcast_to(
                    jnp.exp(g_row[:, e:e + 1]), (8, LANE))
        return carry

    lax.fori_loop(0, npair, prep, 0, unroll=2)

    def step(i, carry):
        sts = list(carry)
        for d in range(2):
            n = i if d == 0 else nch - 1 - i
            rows = pl.ds(pl.multiple_of(n * CHUNK, CHUNK), CHUNK)
            mrows = pl.ds(pl.multiple_of(n * DN_DK, DN_DK), DN_DK)
            st = sts[d]
            sb = st.astype(BF16)
            o = _dot(qe_scr[d, rows, :], sb) + oc_scr[d, rows, :]
            gt = gt_scr[d, pl.ds(pl.multiple_of(n * 8, 8), 8), :][0:1, 0:1]
            sts[d] = gt * st - _dot(m_scr[d, mrows, :], sb) + c_scr[d, mrows, :]
            if d == 0:
                of_scr[rows, :] = o
            else:
                ob_scr[rows, :] = o
        return tuple(sts)

    if has_s0:
        init = (s0_ref[0, 0, 0, 0], s0_ref[0, 0, 1, 0])
    else:
        init = (jnp.zeros((DN_DK, DN_DK), F32), jnp.zeros((DN_DK, DN_DK), F32))
    fin = lax.fori_loop(0, nch, step, init)
    sfin_ref[0, 0, 0] = fin[0]
    sfin_ref[0, 1, 0] = fin[1]

    o = of_scr[...] + ob_scr[...]
    y = o * lax.rsqrt(jnp.mean(o * o, axis=-1, keepdims=True) + EPS) * nrm_ref[...]
    y_ref[...] = y * _silu(z_ref[...].astype(F32))


def _deltanet(proj_a, proj, conv_w, alog_b, dtb_b, nrm, row_off, nseq, seq, s0, layer):
    r0 = row_off // seq
    cb = COL_DNQKV // LANE
    zb = COL_DNZ // LANE
    spec = lambda off: pl.BlockSpec((seq, LANE), lambda b, h: (r0 + b, off + h))
    wspec = lambda off: pl.BlockSpec((3, LANE), lambda b, h: (0, off + h))
    bspec = pl.BlockSpec((2, 1, 8, LANE), lambda b, h: (0, h, 0, 0))
    in_specs = [spec(cb), spec(cb + DN_H), spec(cb + 2 * DN_H), spec(zb),
                pl.BlockSpec((seq, 2 * LANE), lambda b, h: (r0 + b, COL_KV // (2 * LANE))),
                wspec(0), wspec(DN_H), wspec(2 * DN_H), bspec, bspec,
                pl.BlockSpec((1, DN_DK), lambda b, h: (0, 0))]
    args = [proj, proj, proj, proj, proj_a, conv_w, conv_w, conv_w, alog_b, dtb_b, nrm.reshape(1, DN_DK)]
    if s0 is not None:
        in_specs.append(pl.BlockSpec((1, 1, 2, 1, DN_DK, DN_DK), lambda b, h: (b, layer, 0, h, 0, 0)))
        args.append(s0)
    nch = seq // CHUNK
    scratch = [pltpu.VMEM((seq, DN_DK), F32), pltpu.VMEM((seq, DN_DK), F32),
               pltpu.VMEM((seq, DN_DK), F32), pltpu.VMEM((seq, DN_DK), F32), pltpu.VMEM((seq, DN_DK), F32),
               pltpu.VMEM((2, seq, DN_DK), BF16), pltpu.VMEM((2, seq, DN_DK), F32),
               pltpu.VMEM((2, nch * DN_DK, DN_DK), BF16), pltpu.VMEM((2, nch * DN_DK, DN_DK), F32),
               pltpu.VMEM((2, nch * 8, LANE), F32)]
    return pl.pallas_call(
        functools.partial(_dn_kernel, seq=seq, has_s0=s0 is not None),
        out_shape=(jax.ShapeDtypeStruct((nseq * seq, DN_H * DN_DK), F32),
                   jax.ShapeDtypeStruct((nseq, 2, DN_H, DN_DK, DN_DK), F32)),
        grid=(nseq, DN_H),
        in_specs=in_specs,
        out_specs=(pl.BlockSpec((seq, LANE), lambda b, h: (b, h)),
                   pl.BlockSpec((1, 2, 1, DN_DK, DN_DK), lambda b, h: (b, 0, h, 0, 0))),
        scratch_shapes=scratch,
        compiler_params=_cparams(("arbitrary", "arbitrary")),
    )(*args)


def _branch_out_kernel(b0, b1, b2, b3, g0, g1, g2, g3, x_ref, mod_ref, nf_ref, wb_ref, wo_ref,
                       xo_ref, h_ref):
    acc = None
    for n, (br, gl) in enumerate(((b0, g0), (b1, g1), (b2, g2), (b3, g3))):
        p = _sigmoid(gl[...].astype(F32)) * _dot(br[...].astype(BF16), wb_ref[n])
        acc = p if acc is None else acc + p
    y = _dot(acc.astype(BF16), wo_ref[...])
    m = mod_ref[0]
    x = x_ref[...] + m[:, 2 * D:3 * D] * y
    xo_ref[...] = x
    h_ref[...] = _rms(x, nf_ref[...]) * (1.0 + m[:, 4 * D:5 * D]) + m[:, 3 * D:4 * D]


def _branch_out(branches, proj, x, mod3, nf, wb, wo, grp_of_tile, tm):
    t = x.shape[0]
    gb = COL_GATE // D
    bspec = pl.BlockSpec((tm, MIX_W), lambda i: (i, 0))
    gspec = lambda n: pl.BlockSpec((tm, D), lambda i: (i, gb + n))
    row = pl.BlockSpec((tm, D), lambda i: (i, 0))
    return pl.pallas_call(
        _branch_out_kernel,
        out_shape=(jax.ShapeDtypeStruct((t, D), F32), jax.ShapeDtypeStruct((t, D), F32)),
        grid=(t // tm,),
        in_specs=[bspec] * 4 + [gspec(n) for n in range(4)] + [
            row,
            pl.BlockSpec((1, 1, 6 * D), lambda i: (grp_of_tile(tm)(i), 0, 0)),
            pl.BlockSpec((1, D), lambda i: (0, 0)),
            pl.BlockSpec((4, MIX_W, D), lambda i: (0, 0, 0)),
            pl.BlockSpec((D, D), lambda i: (0, 0))],
        out_specs=(row, row),
        compiler_params=_cparams(("arbitrary",)),
    )(*branches, proj, proj, proj, proj, x, mod3, nf.reshape(1, D), wb, wo)


def _route_kernel(h_ref, wr_ref, bias_ref, idx_ref, w_ref, rank_ref, cnt_ref, run_scr, *, tm):
    @pl.when(pl.program_id(0) == 0)
    def _():
        run_scr[...] = jnp.zeros_like(run_scr)

    neg = -jnp.inf
    gsz = N_EXP // N_GRP
    scores = _sigmoid(_dot3_nt(wr_ref[...], h_ref[...]))
    choice = scores + bias_ref[:, 0:1]
    row8 = lax.broadcasted_iota(I32, (gsz, tm), 0)
    gscore = []
    for g in range(N_GRP):
        blk = choice[g * gsz:(g + 1) * gsz]
        m1 = jnp.max(blk, axis=0, keepdims=True)
        i1 = jnp.min(jnp.where(blk == m1, row8, gsz), axis=0, keepdims=True)
        m2 = jnp.max(jnp.where(row8 == i1, neg, blk), axis=0, keepdims=True)
        gscore.append(m1 + m2)
    masked = []
    for g in range(N_GRP):
        rank = jnp.zeros((1, tm), I32)
        for g2 in range(N_GRP):
            if g2 == g:
                continue
            ahead = (gscore[g2] >= gscore[g]) if g2 < g else (gscore[g2] > gscore[g])
            rank = rank + ahead.astype(I32)
        masked.append(jnp.where(rank < TOPK_GRP, choice[g * gsz:(g + 1) * gsz], neg))
    cur = jnp.concatenate(masked, axis=0)
    row = lax.broadcasted_iota(I32, (N_EXP, tm), 0)
    sel = jnp.zeros((N_EXP, tm), jnp.bool_)
    idxs, scs = [], []
    for _ in range(TOP_K):
        m = jnp.max(cur, axis=0, keepdims=True)
        ik = jnp.min(jnp.where(cur == m, row, N_EXP), axis=0, keepdims=True)
        hit = row == ik
        scs.append(jnp.sum(jnp.where(hit, scores, 0.0), axis=0, keepdims=True))
        idxs.append(ik)
        cur = jnp.where(hit, neg, cur)
        sel = sel | hit
    tot = functools.reduce(jnp.add, scs)
    self_f = jnp.where(sel, 1.0, 0.0)
    tri = (lax.broadcasted_iota(I32, (tm, tm), 0) <= lax.broadcasted_iota(I32, (tm, tm), 1))
    csum = _dot(self_f.astype(BF16), jnp.where(tri, 1.0, 0.0).astype(BF16))
    run = run_scr[:, 0:1]
    rank_all = run + csum - self_f
    ranks = [jnp.sum(jnp.where(row == ik, rank_all, 0.0), axis=0, keepdims=True) for ik in idxs]
    idx_ref[...] = jnp.concatenate(idxs, axis=0)
    w_ref[...] = jnp.concatenate([s / tot * ROUTED_SCALE for s in scs], axis=0)
    rank_ref[...] = jnp.concatenate(ranks, axis=0).astype(I32)
    new_run = run + csum[:, tm - 1:tm]
    run_scr[...] = jnp.broadcast_to(new_run, run_scr.shape)
    cnt_ref[...] = jnp.broadcast_to(new_run, cnt_ref.shape).astype(I32)


def _route(h, wr_t, bias_b, tm):
    t = h.shape[0]
    tok = pl.BlockSpec((TOP_K, tm), lambda i: (0, i))
    return pl.pallas_call(
        functools.partial(_route_kernel, tm=tm),
        out_shape=(jax.ShapeDtypeStruct((TOP_K, t), I32), jax.ShapeDtypeStruct((TOP_K, t), F32),
                   jax.ShapeDtypeStruct((TOP_K, t), I32), jax.ShapeDtypeStruct((N_EXP, LANE), I32)),
        grid=(t // tm,),
        in_specs=[pl.BlockSpec((tm, D), lambda i: (i, 0)),
                  pl.BlockSpec((N_EXP, D), lambda i: (0, 0)),
                  pl.BlockSpec((N_EXP, LANE), lambda i: (0, 0))],
        out_specs=(tok, tok, tok, pl.BlockSpec((N_EXP, LANE), lambda i: (0, 0))),
        scratch_shapes=[pltpu.VMEM((N_EXP, LANE), F32)],
        compiler_params=_cparams(("arbitrary",)),
    )(h, wr_t, bias_b)


def _dispatch_kernel(pstart_ref, padded_ref, idx_ref, rank_ref, h_ref, xs_ref, zero_scr, sem, zsem, *, tm):
    def zero_copy(e):
        start = pl.multiple_of(pstart_ref[e] + padded_ref[e] - MOE_BM, MOE_BM)
        return pltpu.make_async_copy(zero_scr, xs_ref.at[pl.ds(start, MOE_BM)], zsem)

    @pl.when(pl.program_id(0) == 0)
    def _():
        zero_scr[...] = jnp.zeros_like(zero_scr)

        def zstart(e, c):
            @pl.when(padded_ref[e] > 0)
            def _():
                zero_copy(e).start()
            return c

        def zwait(e, c):
            @pl.when(padded_ref[e] > 0)
            def _():
                zero_copy(e).wait()
            return c

        lax.fori_loop(0, N_EXP, zstart, 0)
        lax.fori_loop(0, N_EXP, zwait, 0)

    def row_copy(t, k):
        pos = pstart_ref[idx_ref[k, t]] + rank_ref[k, t]
        return pltpu.make_async_copy(h_ref.at[pl.ds(t, 1)], xs_ref.at[pl.ds(pos, 1)], sem)

    def issue(t, c):
        for k in range(TOP_K):
            row_copy(t, k).start()
        return c

    def drain(t, c):
        for k in range(TOP_K):
            row_copy(t, k).wait()
        return c

    lax.fori_loop(0, tm, issue, 0)
    lax.fori_loop(0, tm, drain, 0)


def _dispatch(pstart, padded, idx_t, rank_t, h, n_slots, tm):
    t = h.shape[0]
    smem_tok = pl.BlockSpec((TOP_K, tm), lambda i, *_: (0, i), memory_space=pltpu.SMEM)
    return pl.pallas_call(
        functools.partial(_dispatch_kernel, tm=tm),
        out_shape=jax.ShapeDtypeStruct((n_slots, D), F32),
        grid_spec=pltpu.PrefetchScalarGridSpec(
            num_scalar_prefetch=2,
            grid=(t // tm,),
            in_specs=[smem_tok, smem_tok, pl.BlockSpec((tm, D), lambda i, *_: (i, 0))],
            out_specs=pl.BlockSpec(memory_space=pl.ANY),
            scratch_shapes=[pltpu.VMEM((MOE_BM, D), F32), pltpu.SemaphoreType.DMA(()),
                            pltpu.SemaphoreType.DMA(())]),
        compiler_params=_cparams(("arbitrary",)),
    )(pstart, padded, idx_t, rank_t, h)


def _experts_kernel(be_ref, nu_ref, x_ref, wg_ref, wu_ref, wd_ref, y_ref):
    @pl.when(pl.program_id(0) < nu_ref[0])
    def _():
        x = x_ref[...].astype(BF16)
        a = _silu(_dot(x, wg_ref[0].astype(BF16))) * _dot(x, wu_ref[0].astype(BF16))
        y_ref[...] = _dot(a.astype(BF16), wd_ref[0].astype(BF16))


def _experts(block_e, n_used, xs, wg, wu, wd):
    nb = xs.shape[0] // MOE_BM
    blk = lambda b, be, nu: (jnp.minimum(b, nu[0] - 1), 0)
    wsel = lambda b, be, nu: (be[b], 0, 0)
    return pl.pallas_call(
        _experts_kernel,
        out_shape=jax.ShapeDtypeStruct(xs.shape, F32),
        grid_spec=pltpu.PrefetchScalarGridSpec(
            num_scalar_prefetch=2,
            grid=(nb,),
            in_specs=[pl.BlockSpec((MOE_BM, D), blk),
                      pl.BlockSpec((1, D, D_EXP), wsel),
                      pl.BlockSpec((1, D, D_EXP), wsel),
                      pl.BlockSpec((1, D_EXP, D), wsel)],
            out_specs=pl.BlockSpec((MOE_BM, D), blk)),
        compiler_params=_cparams(("arbitrary",)),
    )(block_e, n_used, xs, wg, wu, wd)


def _combine_kernel(*refs, tm, final):
    it = iter(refs)
    pstart_ref, idx_ref, rank_ref, w_ref, h_ref, x_ref, mod_ref, wsg_ref, wsu_ref, wsd_ref = (
        next(it) for _ in range(10))
    if final:
        fn_ref = next(it)
    ys_ref = next(it)
    xo_ref = next(it)
    if final:
        yo_ref = next(it)
    buf, sem = next(it), next(it)

    def row_copy(t, k):
        pos = pstart_ref[idx_ref[k, t]] + rank_ref[k, t]
        return pltpu.make_async_copy(ys_ref.at[pl.ds(pos, 1)], buf.at[k, pl.ds(t, 1)], sem)

    def issue(t, c):
        for k in range(TOP_K):
            row_copy(t, k).start()
        return c

    def drain(t, c):
        for k in range(TOP_K):
            row_copy(t, k).wait()
        return c

    lax.fori_loop(0, tm, issue, 0)
    hb = h_ref[...].astype(BF16)
    a = _silu(_dot(hb, wsg_ref[...])) * _dot(hb, wsu_ref[...])
    shared = _dot(a.astype(BF16), wsd_ref[...])
    lax.fori_loop(0, tm, drain, 0)
    w = w_ref[...]
    routed = buf[0] * w[:, 0:1]
    for k in range(1, TOP_K):
        routed = routed + buf[k] * w[:, k:k + 1]
    m = mod_ref[0]
    x = x_ref[...] + m[:, 5 * D:6 * D] * (routed + shared)
    xo_ref[...] = x
    if final:
        yo_ref[...] = _rms(x, fn_ref[...])


def _combine(pstart, idx_t, rank_t, w_tok, h, x, mod3, wsg, wsu, wsd, fn, ys, grp_of_tile, tm):
    t = h.shape[0]
    final = fn is not None
    smem_tok = pl.BlockSpec((TOP_K, tm), lambda i, *_: (0, i), memory_space=pltpu.SMEM)
    row = pl.BlockSpec((tm, D), lambda i, *_: (i, 0))
    full = lambda shp: pl.BlockSpec(shp, lambda i, *_: (0,) * len(shp))
    in_specs = [smem_tok, smem_tok, pl.BlockSpec((tm, TOP_K), lambda i, *_: (i, 0)), row, row,
                pl.BlockSpec((1, 1, 6 * D), lambda i, *_: (grp_of_tile(tm)(i), 0, 0)),
                full((D, D_EXP)), full((D, D_EXP)), full((D_EXP, D))]
    args = [pstart, idx_t, rank_t, w_tok, h, x, mod3, wsg, wsu, wsd]
    if final:
        in_specs.append(full((1, D)))
        args.append(fn.reshape(1, D))
    in_specs.append(pl.BlockSpec(memory_space=pl.ANY))
    args.append(ys)
    out_shape = [jax.ShapeDtypeStruct((t, D), F32)]
    out_specs = [row]
    if final:
        out_shape.append(jax.ShapeDtypeStruct((t, D), F32))
        out_specs.append(row)
    return pl.pallas_call(
        functools.partial(_combine_kernel, tm=tm, final=final),
        out_shape=tuple(out_shape),
        grid_spec=pltpu.PrefetchScalarGridSpec(
            num_scalar_prefetch=1,
            grid=(t // tm,),
            in_specs=in_specs,
            out_specs=tuple(out_specs),
            scratch_shapes=[pltpu.VMEM((TOP_K, tm, D), F32), pltpu.SemaphoreType.DMA(())]),
        compiler_params=_cparams(("arbitrary",)),
    )(*args)


def _pack_w_in(w):
    cuts = np.cumsum([0, 256, 160, 1536, 512, 16, 2048, 1536, 4096])
    seg = lambda i: w[:, cuts[i]:cuts[i + 1]]
    pad = jnp.zeros((D, 2 * LANE - 160 - 16), w.dtype)
    wa = jnp.concatenate([seg(0), seg(1), seg(4), pad], axis=1).astype(BF16)
    wb = jnp.concatenate([seg(7), seg(2), seg(3), seg(5), seg(6)], axis=1).astype(BF16)
    return wa, wb


def _pack_w_uq(w):
    w = w.reshape(MLA_QR, MLA_H, MLA_NOPE + MLA_ROPE)
    w = jnp.pad(w, ((0, 0), (0, 0), (0, LANE - MLA_NOPE - MLA_ROPE)))
    return w.reshape(MLA_QR, MLA_H * LANE).astype(BF16)


def _pack_w_ukv(w):
    w = w.reshape(MLA_KVR, MLA_H, MLA_NOPE + MLA_V)
    k_nope = jnp.pad(w[:, :, :MLA_NOPE], ((0, 0), (0, 0), (0, LANE - MLA_NOPE)))
    eye = jnp.eye(MLA_ROPE, dtype=w.dtype)[:, None, :]
    k_pe = jnp.pad(jnp.broadcast_to(eye, (MLA_ROPE, MLA_H, MLA_ROPE)),
                   ((0, LANE - MLA_ROPE), (0, 0), (MLA_NOPE, LANE - MLA_NOPE - MLA_ROPE)))
    wk = jnp.concatenate([k_nope, k_pe], axis=0).reshape(2 * LANE, MLA_H * LANE)
    wv = w[:, :, MLA_NOPE:].reshape(MLA_KVR, MLA_H * MLA_V)
    return wk.astype(BF16), wv.astype(BF16)


def _axial_angles(n_tok, dim):
    nf = dim // 4
    inv = ROPE_BASE ** (-jnp.arange(nf, dtype=F32) / nf)
    r = jnp.repeat(jnp.arange(n_tok // GRID_W, dtype=F32), GRID_W)
    cc = jnp.tile(jnp.arange(GRID_W, dtype=F32), n_tok // GRID_W)
    ang = jnp.concatenate([r[:, None] * inv, cc[:, None] * inv], axis=-1)
    return jnp.cos(ang), jnp.sin(ang)


def _rope_tables_mla(n_tok, lane0):
    cos, sin = _axial_angles(n_tok, MLA_ROPE)
    half = MLA_ROPE // 2
    z = lambda w: jnp.zeros((n_tok, w), F32)
    o = lambda w: jnp.ones((n_tok, w), F32)
    rest = LANE - lane0 - MLA_ROPE
    c = jnp.concatenate([o(lane0), cos, cos, o(rest)], axis=1)
    s1 = jnp.concatenate([z(lane0), -sin, z(half), z(rest)], axis=1)
    s2 = jnp.concatenate([z(lane0), z(half), sin, z(rest)], axis=1)
    return c, s1, s2


def _rope_tables_ret(n_tok):
    cos, sin = _axial_angles(n_tok, RET_DK)
    return jnp.concatenate([cos, cos], axis=1), jnp.concatenate([-sin, sin], axis=1)


def _bcast_dh(a):
    return jnp.broadcast_to(a.astype(F32)[:, :, None, None], a.shape + (8, LANE))


def _forward(x_prompt, x_sample, c, cache_ckv, cache_kpe, state_dn, state_ret, c_ctx, P, tiles):
    nb, sc, _ = x_prompt.shape
    nl, sl, _ = x_sample.shape
    past = cache_ckv.shape[2]
    depth = P['w_in'].shape[0]
    t_ctx, t_lat = nb * sc, nl * sl
    t = t_ctx + t_lat
    tm = tiles['tm']
    assert sc % tm == 0 and sl % tm == 0 and t_ctx % sl == 0

    assert sl % tiles['tc'] == 0 and t_ctx % tiles['tc'] == 0

    def grp_of_tile(rows_per_tile):
        nct = t_ctx // rows_per_tile
        return lambda i: jnp.where(i < nct, 0, 1 + (i - nct) // (sl // rows_per_tile))

    x = jnp.concatenate([x_prompt.reshape(t_ctx, D), x_sample.reshape(t_lat, D)], axis=0)
    ngrp = 1 + nl
    cvec = jnp.concatenate([c_ctx[None], c, jnp.zeros((-(ngrp) % 8, D), F32)], axis=0)

    tab_q = _rope_tables_mla(sl, MLA_NOPE)
    tab_k = _rope_tables_mla(sl, 0)
    tab_r = _rope_tables_ret(sl)

    ckv_l, kpe_l, dn_l, ret_l = [], [], [], []
    y_final = None
    for l in range(depth):
        mod3 = _ada(cvec, P['w_ada'][l], P['b_ada'][l]).reshape(cvec.shape[0], 1, 6 * D)
        w_a, w_b = _pack_w_in(P['w_in'][l])
        tp = tiles['tp']
        proj_a = _in_proj(x, mod3, P['norm_mix'][l], w_a, grp_of_tile, tp, N_PROJ_A, F32)
        proj = _in_proj(x, mod3, P['norm_mix'][l], w_b, grp_of_tile, tp, tiles['tn'], BF16)

        w_uq = _pack_w_uq(P['mla_w_uq'][l])
        wk, wv = _pack_w_ukv(P['mla_w_ukv'][l])
        gq, gkv = P['mla_q_norm'][l], P['mla_kv_norm'][l]
        q_c = _mla_q(proj_a, gq, w_uq, 0, t_ctx, tm, None, sc)
        q_l = _mla_q(proj_a, gq, w_uq, t_ctx, t_lat, tm, tab_q, sl)
        kvb = COL_KV // (2 * LANE)
        ckv_c, kpe_c, k_c, v_c = _mla_kv(proj_a, kvb, gkv, wk, wv, 0, t_ctx, tm, True, None, sc)
        _, _, k_l, v_l = _mla_kv(proj_a, kvb, gkv, wk, wv, t_ctx, t_lat, tm, True, tab_k, sl)
        cached = jnp.concatenate([cache_ckv[:, l], cache_kpe[:, l],
                                  jnp.zeros((nl, past, 2 * LANE - MLA_KVR - MLA_ROPE), F32)], axis=-1)
        pt = min(tm, past)
        _, _, k_p, v_p = _mla_kv(cached.reshape(nl * past, 2 * LANE), 0, gkv, wk, wv, 0, nl * past, pt,
                                 False, None, past)
        r3 = lambda a, n, s: a.reshape(n, s, a.shape[-1])
        y_mla_c = _attn(r3(q_c, nb, sc), [r3(k_c, nb, sc)], [r3(v_c, nb, sc)], min(sc, tiles['tq']))
        y_mla_l = _attn(r3(q_l, nl, sl), [r3(k_p, nl, past), r3(k_l, nl, sl)],
                        [r3(v_p, nl, past), r3(v_l, nl, sl)], min(sl, tiles['tq']))
        y_mla = jnp.concatenate([y_mla_c.reshape(t_ctx, MIX_W), y_mla_l.reshape(t_lat, MIX_W)], axis=0)

        alog_b, dtb_b = _bcast_dh(P['dn_A_log'][l]), _bcast_dh(P['dn_dt_bias'][l])
        dn_args = (proj_a, proj, P['dn_conv'][l], alog_b, dtb_b, P['dn_norm'][l])
        y_dn_c, dn_fin = _deltanet(*dn_args, 0, nb, sc, None, l)
        y_dn_l, _ = _deltanet(*dn_args, t_ctx, nl, sl, state_dn, l)
        y_dn = jnp.concatenate([y_dn_c, y_dn_l], axis=0)

        dec_b = _bcast_dh(P['ret_decay'][l])
        y_ret_c, ret_fin = _retention(proj, dec_b, P['ret_gn'][l], 0, nb, sc, None, None, l)
        y_ret_l, _ = _retention(proj, dec_b, P['ret_gn'][l], t_ctx, nl, sl, tab_r, state_ret, l)
        y_ret = jnp.concatenate([y_ret_c, y_ret_l], axis=0)

        y_sc = jnp.concatenate([_sconv(proj, P['sc_conv'][l], 0, nb, sc),
                                _sconv(proj, P['sc_conv'][l], t_ctx, nl, sl)], axis=0)

        x_mid, h2 = _branch_out((y_mla, y_dn, y_ret, y_sc), proj, x, mod3, P['norm_ffn'][l],
                                P['w_branch'][l].astype(BF16), P['w_out'][l].astype(BF16), grp_of_tile, tm)

        bias_b = jnp.broadcast_to(P['router_bias'][l].astype(F32)[:, None], (N_EXP, LANE))
        idx_t, w_t, rank_t, cnt = _route(h2, P['router'][l].T, bias_b, tiles['tr'])
        counts = cnt[:, 0]
        padded = (counts + MOE_BM - 1) // MOE_BM * MOE_BM
        pad_end = jnp.cumsum(padded)
        pstart = (pad_end - padded).astype(I32)
        n_blocks = (t * TOP_K) // MOE_BM + N_EXP
        blk_row0 = jnp.arange(n_blocks, dtype=I32) * MOE_BM
        block_e = jnp.minimum(jnp.sum((pad_end[None, :] <= blk_row0[:, None]).astype(I32), axis=1), N_EXP - 1)
        n_used = (pad_end[-1:] // MOE_BM).astype(I32)
        xs = _dispatch(pstart, padded.astype(I32), idx_t, rank_t, h2, n_blocks * MOE_BM, tiles['td'])
        ys = _experts(block_e, n_used, xs, P['w_eg'][l], P['w_eu'][l], P['w_ed'][l])
        fn = P['final_norm'] if l == depth - 1 else None
        outs = _combine(pstart, idx_t, rank_t, w_t.T, h2, x_mid, mod3, P['w_sg'][l].astype(BF16),
                        P['w_su'][l].astype(BF16), P['w_sd'][l].astype(BF16), fn, ys, grp_of_tile, tiles['tc'])
        x = outs[0]
        if fn is not None:
            y_final = outs[1]

        ckv_l.append(ckv_c.reshape(nb, sc, MLA_KVR))
        kpe_l.append(kpe_c.reshape(nb, sc, MLA_ROPE))
        dn_l.append(dn_fin)
        ret_l.append(ret_fin)

    y_prompt = y_final[:t_ctx].reshape(nb, sc, D)
    y_sample = y_final[t_ctx:].reshape(nl, sl, D)
    return (y_prompt, y_sample, jnp.stack(ckv_l, axis=1), jnp.stack(kpe_l, axis=1),
            jnp.stack(dn_l, axis=1), jnp.stack(ret_l, axis=1))


_TILES = dict(tm=256, tp=1024, tn=512, tq=256, tr=512, td=256, tc=128)


def kernel(x_prompt, x_sample, c, cache_ckv, cache_kpe, state_dn, state_ret, c_ctx, w_ada, b_ada, norm_mix, norm_ffn, w_in, mla_q_norm, mla_w_uq, mla_kv_norm, mla_w_ukv, dn_conv, dn_A_log, dn_dt_bias, dn_norm, ret_decay, ret_gn, sc_conv, w_branch, w_out, router, router_bias, w_eg, w_eu, w_ed, w_sg, w_su, w_sd, final_norm):
    P = dict(w_ada=w_ada, b_ada=b_ada, norm_mix=norm_mix, norm_ffn=norm_ffn, w_in=w_in,
             mla_q_norm=mla_q_norm, mla_w_uq=mla_w_uq, mla_kv_norm=mla_kv_norm, mla_w_ukv=mla_w_ukv,
             dn_conv=dn_conv, dn_A_log=dn_A_log, dn_dt_bias=dn_dt_bias, dn_norm=dn_norm,
             ret_decay=ret_decay, ret_gn=ret_gn, sc_conv=sc_conv, w_branch=w_branch, w_out=w_out,
             router=router, router_bias=router_bias, w_eg=w_eg, w_eu=w_eu, w_ed=w_ed,
             w_sg=w_sg, w_su=w_su, w_sd=w_sd, final_norm=final_norm)
    return _forward(x_prompt, x_sample, c, cache_ckv, cache_kpe, state_dn, state_ret, c_ctx, P, _TILES)
```

```python
import functools
import math

import jax
import jax.numpy as jnp
import numpy as np
from jax import lax
from jax.experimental import pallas as pl
from jax.experimental.pallas import tpu as pltpu

F32 = jnp.float32
BF16 = jnp.bfloat16
I32 = jnp.int32

D = 1024
EPS = 1e-6
CHUNK = 64
PAIR = 2 * CHUNK
GRID_W = 64
ROPE_BASE = 10000.0

MLA_H, MLA_NOPE, MLA_ROPE, MLA_V, MLA_QR, MLA_KVR = 8, 64, 32, 64, 256, 128
DN_H, DN_DK = 4, 128
RET_H, RET_DK = 4, 128
MIX_W = 512
N_EXP, TOP_K, N_GRP, TOPK_GRP, D_EXP = 64, 8, 8, 4, 256
ROUTED_SCALE = 2.5

LANE = 128
COL_CQ = 0
COL_KV = 256
N_PROJ_A = 512
AB_LANE = 160
COL_GATE = 0
COL_DNQKV = 4096
COL_DNZ = 5632
COL_RET = 6144
COL_SC = 8192
N_PROJ_B = 9728

MOE_BM = 256
VMEM_LIMIT = 56 * 1024 * 1024


def _cparams(sem, vmem=None):
    return pltpu.CompilerParams(dimension_semantics=sem, vmem_limit_bytes=vmem or VMEM_LIMIT)


def _dot(a, b):
    return jnp.dot(a, b, preferred_element_type=F32)


def _dot_nt(a, b):
    return lax.dot_general(a, b, (((1,), (1,)), ((), ())), preferred_element_type=F32)


def _dot_tn(a, b):
    return lax.dot_general(a, b, (((0,), (0,)), ((), ())), preferred_element_type=F32)


def _split(a):
    hi = a.astype(BF16)
    lo = (a - hi.astype(F32)).astype(BF16)
    return hi, lo


def _dot3(a, b):
    ah, al = _split(a)
    bh, bl = _split(b)
    return _dot(ah, bh) + (_dot(ah, bl) + _dot(al, bh))


def _dot3s(a, b):
    (ah, al), (bh, bl) = a, b
    return _dot(jnp.concatenate([ah, ah, al], axis=1), jnp.concatenate([bh, bl, bh], axis=0))


def _dot3_nt(a, b):
    ah, al = _split(a)
    bh, bl = _split(b)
    return _dot_nt(ah, bh) + (_dot_nt(ah, bl) + _dot_nt(al, bh))


def _sigmoid(x):
    return 1.0 / (1.0 + jnp.exp(-x))


def _silu(x):
    return x * _sigmoid(x)


def _rms(x, g):
    return x * lax.rsqrt(jnp.mean(x * x, axis=-1, keepdims=True) + EPS) * g


def _ada_kernel(c_ref, w_ref, b_ref, o_ref):
    o_ref[...] = _dot3(_silu(c_ref[...]), w_ref[...]) + b_ref[...]


def _ada(cvec, w, b):
    n = w.shape[1]
    tn = 1024
    return pl.pallas_call(
        _ada_kernel,
        out_shape=jax.ShapeDtypeStruct((cvec.shape[0], n), F32),
        grid=(n // tn,),
        in_specs=[pl.BlockSpec(cvec.shape, lambda j: (0, 0)),
                  pl.BlockSpec((D, tn), lambda j: (0, j)),
                  pl.BlockSpec((1, tn), lambda j: (0, j))],
        out_specs=pl.BlockSpec((cvec.shape[0], tn), lambda j: (0, j)),
        compiler_params=_cparams(("arbitrary",)),
    )(cvec, w, b.reshape(1, n))


def _in_proj_kernel(x_ref, mod_ref, g_ref, w_ref, o_ref, h_scr):
    @pl.when(pl.program_id(1) == 0)
    def _():
        m = mod_ref[0]
        y = _rms(x_ref[...], g_ref[...])
        h_scr[...] = (y * (1.0 + m[:, D:2 * D]) + m[:, 0:D]).astype(BF16)

    o_ref[...] = _dot(h_scr[...], w_ref[...]).astype(o_ref.dtype)


def _in_proj(x, mod3, g, w, grp_of_tile, tm, tn, out_dtype):
    t = x.shape[0]
    n = w.shape[1]
    return pl.pallas_call(
        _in_proj_kernel,
        out_shape=jax.ShapeDtypeStruct((t, n), out_dtype),
        grid=(t // tm, n // tn),
        in_specs=[pl.BlockSpec((tm, D), lambda i, j: (i, 0)),
                  pl.BlockSpec((1, 1, 6 * D), lambda i, j: (grp_of_tile(tm)(i), 0, 0)),
                  pl.BlockSpec((1, D), lambda i, j: (0, 0)),
                  pl.BlockSpec((D, tn), lambda i, j: (0, j))],
        out_specs=pl.BlockSpec((tm, tn), lambda i, j: (i, j)),
        scratch_shapes=[pltpu.VMEM((tm, D), BF16)],
        compiler_params=_cparams(("arbitrary", "arbitrary")),
    )(x, mod3, g.reshape(1, D), w)


def _rope3(x, c, s1, s2, width):
    return x * c + pltpu.roll(x, width - 16, 1) * s1 + pltpu.roll(x, 16, 1) * s2


def _mla_q_kernel(*refs, rope):
    if rope:
        p_ref, g_ref, w_ref, c_ref, s1_ref, s2_ref, o_ref = refs
    else:
        p_ref, g_ref, w_ref, o_ref = refs
    y = _rms(p_ref[...], g_ref[...])
    q = _dot(y.astype(BF16), w_ref[...])
    if rope:
        tile = lambda r: jnp.concatenate([r[...]] * MLA_H, axis=1)
        q = _rope3(q, tile(c_ref), tile(s1_ref), tile(s2_ref), MLA_H * LANE)
    o_ref[...] = q.astype(BF16)


def _mla_q(proj, g, w, row_off, rows, tm, rope_tabs, seq):
    nrow = rows // tm
    r0 = row_off // tm
    in_specs = [pl.BlockSpec((tm, MLA_QR), lambda i: (r0 + i, COL_CQ // MLA_QR)),
                pl.BlockSpec((1, MLA_QR), lambda i: (0, 0)),
                pl.BlockSpec((MLA_QR, MLA_H * LANE), lambda i: (0, 0))]
    args = [proj, g.reshape(1, MLA_QR), w]
    if rope_tabs is not None:
        per = seq // tm
        in_specs += [pl.BlockSpec((tm, LANE), lambda i: (i % per, 0))] * 3
        args += list(rope_tabs)
    return pl.pallas_call(
        functools.partial(_mla_q_kernel, rope=rope_tabs is not None),
        out_shape=jax.ShapeDtypeStruct((rows, MLA_H * LANE), BF16),
        grid=(nrow,),
        in_specs=in_specs,
        out_specs=pl.BlockSpec((tm, MLA_H * LANE), lambda i: (i, 0)),
        compiler_params=_cparams(("arbitrary",)),
    )(*args)


def _mla_kv_kernel(*refs, norm, rope):
    if rope:
        p_ref, g_ref, wk_ref, wv_ref, c_ref, s1_ref, s2_ref, ckv_ref, kpe_ref, k_ref, v_ref = refs
    else:
        p_ref, g_ref, wk_ref, wv_ref, ckv_ref, kpe_ref, k_ref, v_ref = refs
    blk = p_ref[...]
    ckv = blk[:, :MLA_KVR]
    if norm:
        ckv = _rms(ckv, g_ref[...])
    kp = blk[:, MLA_KVR:]
    ckv_ref[...] = ckv
    kpe_ref[...] = kp[:, :MLA_ROPE]
    if rope:
        kp = _rope3(kp, c_ref[...], s1_ref[...], s2_ref[...], LANE)
    a = jnp.concatenate([ckv, kp], axis=1).astype(BF16)
    k_ref[...] = _dot(a, wk_ref[...]).astype(BF16)
    v_ref[...] = _dot(ckv.astype(BF16), wv_ref[...]).astype(BF16)


def _mla_kv(src, col_blk, g, wk, wv, row_off, rows, tm, norm, rope_tabs, seq):
    nrow = rows // tm
    r0 = row_off // tm
    in_specs = [pl.BlockSpec((tm, 2 * LANE), lambda i: (r0 + i, col_blk)),
                pl.BlockSpec((1, MLA_KVR), lambda i: (0, 0)),
                pl.BlockSpec((2 * LANE, MLA_H * LANE), lambda i: (0, 0)),
                pl.BlockSpec((MLA_KVR, MLA_H * MLA_V), lambda i: (0, 0))]
    args = [src, g.reshape(1, MLA_KVR), wk, wv]
    if rope_tabs is not None:
        per = seq // tm
        in_specs += [pl.BlockSpec((tm, LANE), lambda i: (i % per, 0))] * 3
        args += list(rope_tabs)
    return pl.pallas_call(
        functools.partial(_mla_kv_kernel, norm=norm, rope=rope_tabs is not None),
        out_shape=(jax.ShapeDtypeStruct((rows, MLA_KVR), F32),
                   jax.ShapeDtypeStruct((rows, MLA_ROPE), F32),
                   jax.ShapeDtypeStruct((rows, MLA_H * LANE), BF16),
                   jax.ShapeDtypeStruct((rows, MLA_H * MLA_V), BF16)),
        grid=(nrow,),
        in_specs=in_specs,
        out_specs=(pl.BlockSpec((tm, MLA_KVR), lambda i: (i, 0)),
                   pl.BlockSpec((tm, MLA_ROPE), lambda i: (i, 0)),
                   pl.BlockSpec((tm, MLA_H * LANE), lambda i: (i, 0)),
                   pl.BlockSpec((tm, MLA_H * MLA_V), lambda i: (i, 0))),
        compiler_params=_cparams(("arbitrary",)),
    )(*args)


def _attn_kernel(*refs, nseg):
    q_ref = refs[0]
    k_refs = refs[1:1 + nseg]
    v_refs = refs[1 + nseg:1 + 2 * nseg]
    o_ref = refs[1 + 2 * nseg]
    scale = (MLA_NOPE + MLA_ROPE) ** -0.5
    outs = []
    for h in range(MLA_H):
        qh = q_ref[0, :, h * LANE:(h + 1) * LANE]
        ss = [_dot_nt(qh, k_ref[0, :, h * LANE:(h + 1) * LANE]) * scale for k_ref in k_refs]
        m = functools.reduce(jnp.maximum, [jnp.max(s, axis=-1, keepdims=True) for s in ss])
        es = [jnp.exp(s - m) for s in ss]
        inv = 1.0 / functools.reduce(jnp.add, [jnp.sum(e, axis=-1, keepdims=True) for e in es])
        o = functools.reduce(jnp.add, [
            _dot((e * inv).astype(BF16), v_ref[0, :, h * MLA_V:(h + 1) * MLA_V])
            for e, v_ref in zip(es, v_refs)])
        outs.append(o)
    o_ref[0] = jnp.concatenate(outs, axis=1)


def _attn(q, ks, vs, tq):
    b, s, _ = q.shape
    nseg = len(ks)
    in_specs = [pl.BlockSpec((1, tq, MLA_H * LANE), lambda i, j: (i, j, 0))]
    in_specs += [pl.BlockSpec((1,) + k.shape[1:], lambda i, j: (i, 0, 0)) for k in ks]
    in_specs += [pl.BlockSpec((1,) + v.shape[1:], lambda i, j: (i, 0, 0)) for v in vs]
    return pl.pallas_call(
        functools.partial(_attn_kernel, nseg=nseg),
        out_shape=jax.ShapeDtypeStruct((b, s, MLA_H * MLA_V), F32),
        grid=(b, s // tq),
        in_specs=in_specs,
        out_specs=pl.BlockSpec((1, tq, MLA_H * MLA_V), lambda i, j: (i, j, 0)),
        compiler_params=_cparams(("arbitrary", "arbitrary")),
    )(q, *ks, *vs)


def _conv3(x, w):
    s = x.shape[0]
    row = lax.broadcasted_iota(I32, x.shape, 0)
    prev = jnp.where(row == 0, 0.0, pltpu.roll(x, 1, 0))
    nxt = jnp.where(row == s - 1, 0.0, pltpu.roll(x, s - 1, 0))
    return prev * w[0:1] + x * w[1:2] + nxt * w[2:3]


def _sconv_kernel(b_ref, c_ref, x_ref, w_ref, o_ref):
    f = lambda r: r[...].astype(F32)
    o_ref[...] = f(b_ref) * _conv3(f(c_ref) * f(x_ref), w_ref[...])


def _sconv(proj, w, row_off, nseq, seq):
    r0 = row_off // seq
    cb = COL_SC // LANE
    nc = MIX_W // LANE
    spec = lambda off: pl.BlockSpec((seq, LANE), lambda b, j: (r0 + b, cb + off + j))
    return pl.pallas_call(
        _sconv_kernel,
        out_shape=jax.ShapeDtypeStruct((nseq * seq, MIX_W), F32),
        grid=(nseq, nc),
        in_specs=[spec(0), spec(nc), spec(2 * nc), pl.BlockSpec((3, LANE), lambda b, j: (0, j))],
        out_specs=pl.BlockSpec((seq, LANE), lambda b, j: (b, j)),
        compiler_params=_cparams(("arbitrary", "arbitrary")),
    )(proj, proj, proj, w)


def _ret_kernel(*refs, seq, rope, has_s0):
    it = iter(refs)
    q_ref, k_ref, v_ref, g_ref, dec_ref, gn_ref = (next(it) for _ in range(6))
    if rope:
        c_ref, s_ref = next(it), next(it)
    if has_s0:
        s0_ref = next(it)
    y_ref, sfin_ref, of_scr, ob_scr = (next(it) for _ in range(4))
    nch = seq // CHUNK

    q = q_ref[...].astype(F32)
    k = k_ref[...].astype(F32) * (RET_DK ** -0.5)
    if rope:
        c, sn = c_ref[...], s_ref[...]
        q = q * c + pltpu.roll(q, RET_DK // 2, 1) * sn
        k = k * c + pltpu.roll(k, RET_DK // 2, 1) * sn
    v = v_ref[...].astype(F32)

    ti = lax.broadcasted_iota(I32, (CHUNK, CHUNK), 0)
    si = lax.broadcasted_iota(I32, (CHUNK, CHUNK), 1)
    pos = lax.broadcasted_iota(I32, (CHUNK, 1), 0).astype(F32)
    for d in range(2):
        lg = -jnp.exp(dec_ref[d, 0, 0:1, 0:1])
        if d == 0:
            dist = (ti - si).astype(F32)
            qpow, kpow = pos + 1.0, (CHUNK - 1.0) - pos
        else:
            dist = (si - ti).astype(F32)
            qpow, kpow = CHUNK - pos, pos
        decay = jnp.where(dist >= 0, jnp.exp(jnp.maximum(dist, 0.0) * lg), 0.0)
        qs = jnp.exp(qpow * lg)
        ks = jnp.exp(kpow * lg)
        gtot = jnp.exp(CHUNK * lg)
        st = s0_ref[0, 0, d, 0] if has_s0 else jnp.zeros((RET_DK, RET_DK), F32)
        o_scr = of_scr if d == 0 else ob_scr
        order = range(nch) if d == 0 else range(nch - 1, -1, -1)
        for n in order:
            sl = slice(n * CHUNK, (n + 1) * CHUNK)
            qn, kn, vn = q[sl], k[sl], v[sl]
            vb = vn.astype(BF16)
            a = _dot_nt(qn.astype(BF16), kn.astype(BF16)) * decay
            o = _dot(a.astype(BF16), vb) + _dot((qn * qs).astype(BF16), st.astype(BF16))
            st = gtot * st + _dot_tn((kn * ks).astype(BF16), vb)
            o_scr[sl, :] = o
        sfin_ref[0, d, 0] = st

    o = of_scr[...] + ob_scr[...]
    dlt = o - jnp.mean(o, axis=-1, keepdims=True)
    y = dlt * lax.rsqrt(jnp.mean(dlt * dlt, axis=-1, keepdims=True) + EPS) * gn_ref[...]
    y_ref[...] = _silu(g_ref[...].astype(F32)) * y


def _retention(proj, dec_b, gn, row_off, nseq, seq, rope_tabs, s0, layer):
    r0 = row_off // seq
    cb = COL_RET // LANE
    spec = lambda off: pl.BlockSpec((seq, LANE), lambda b, h: (r0 + b, cb + off + h))
    in_specs = [spec(0), spec(RET_H), spec(2 * RET_H), spec(3 * RET_H),
                pl.BlockSpec((2, 1, 8, LANE), lambda b, h: (0, h, 0, 0)),
                pl.BlockSpec((1, LANE), lambda b, h: (0, h))]
    args = [proj, proj, proj, proj, dec_b, gn.reshape(1, RET_H * RET_DK)]
    if rope_tabs is not None:
        in_specs += [pl.BlockSpec((seq, LANE), lambda b, h: (0, 0))] * 2
        args += list(rope_tabs)
    if s0 is not None:
        in_specs.append(pl.BlockSpec((1, 1, 2, 1, RET_DK, RET_DK), lambda b, h: (b, layer, 0, h, 0, 0)))
        args.append(s0)
    return pl.pallas_call(
        functools.partial(_ret_kernel, seq=seq, rope=rope_tabs is not None, has_s0=s0 is not None),
        out_shape=(jax.ShapeDtypeStruct((nseq * seq, RET_H * RET_DK), F32),
                   jax.ShapeDtypeStruct((nseq, 2, RET_H, RET_DK, RET_DK), F32)),
        grid=(nseq, RET_H),
        in_specs=in_specs,
        out_specs=(pl.BlockSpec((seq, LANE), lambda b, h: (b, h)),
                   pl.BlockSpec((1, 2, 1, RET_DK, RET_DK), lambda b, h: (b, 0, h, 0, 0))),
        scratch_shapes=[pltpu.VMEM((seq, RET_DK), F32), pltpu.VMEM((seq, RET_DK), F32)],
        compiler_params=_cparams(("arbitrary", "arbitrary")),
    )(*args)


def _softplus(x):
    return jnp.maximum(x, 0.0) + jnp.log1p(jnp.exp(-jnp.abs(x)))


def _l2n(x):
    return x * lax.rsqrt(jnp.sum(x * x, axis=-1, keepdims=True) + EPS)


def _dn_kernel(*refs, seq, has_s0):
    it = iter(refs)
    q_ref, k_ref, v_ref, z_ref, ab_ref, wq_ref, wk_ref, wv_ref, alog_ref, dtb_ref, nrm_ref = (
        next(it) for _ in range(11))
    if has_s0:
        s0_ref = next(it)
    y_ref, sfin_ref = next(it), next(it)
    of_scr, ob_scr, q_scr, k_scr, v_scr, qe_scr, oc_scr, m_scr, c_scr, gt_scr = (next(it) for _ in range(10))
    nch = seq // CHUNK
    npair = seq // PAIR
    hcols = lambda hh: slice(hh * DN_DK, (hh + 1) * DN_DK)

    xq = _silu(_conv3(q_ref[...].astype(F32), wq_ref[...]))
    xk = _silu(_conv3(k_ref[...].astype(F32), wk_ref[...]))
    v_scr[...] = _silu(_conv3(v_ref[...].astype(F32), wv_ref[...]))
    for hh in range(DN_H):
        q_scr[:, hcols(hh)] = _l2n(xq[:, hcols(hh)]) * (DN_DK ** -0.5)
        k_scr[:, hcols(hh)] = _l2n(xk[:, hcols(hh)])

    ti = lax.broadcasted_iota(I32, (PAIR, PAIR), 0)
    si = lax.broadcasted_iota(I32, (PAIR, PAIR), 1)
    same = (ti >> 6) == (si >> 6)
    s_in = si & (CHUNK - 1)
    eye = ti == si
    eye_f = eye.astype(F32)
    lane = lax.broadcasted_iota(I32, (1, 2 * LANE), 1)

    def ab_col(blk, idx):
        return jnp.sum(jnp.where(lane == AB_LANE + idx, blk, 0.0), axis=1, keepdims=True)

    masks = []
    for d in range(2):
        masks.append((same & ((si <= ti) if d == 0 else (si >= ti)),
                      same & ((si < ti) if d == 0 else (si > ti)),
                      same & ((ti <= si) if d == 0 else (ti >= si)),
                      same & (s_in == (CHUNK - 1 if d == 0 else 0))))

    def prep(p, carry):
        rows = pl.ds(pl.multiple_of(p * PAIR, PAIR), PAIR)
        blk = ab_ref[rows, :]
        chains = []
        for hh in range(DN_H):
            cs = hcols(hh)
            qn, kn, vn = q_scr[rows, cs], k_scr[rows, cs], v_scr[rows, cs]
            kb = kn.astype(BF16)
            kk = _dot_nt(kb, kb)
            qkr = _dot_nt(qn.astype(BF16), kb)
            for d in range(2):
                incl, strict, incl_t, last_s = masks[d]
                neg_a = -jnp.exp(alog_ref[d, hh, 0:1, 0:1])
                dtb = dtb_ref[d, hh, 0:1, 0:1]
                la = neg_a * _softplus(ab_col(blk, d * DN_H + hh) + dtb)
                beta = _sigmoid(ab_col(blk, 2 * DN_H + d * DN_H + hh))
                g_row = jnp.sum(jnp.where(incl_t, la, 0.0), axis=0, keepdims=True)
                g_col = jnp.sum(jnp.where(eye, g_row, 0.0), axis=1, keepdims=True)
                g_end = jnp.sum(jnp.where(last_s, g_row, 0.0), axis=1, keepdims=True)
                decay = jnp.where(incl, jnp.exp(jnp.where(incl, g_col - g_row, 0.0)), 0.0)
                nmat = jnp.where(strict, -(beta * decay * kk), 0.0)
                eg = jnp.exp(g_col)
                chains.append(dict(
                    hh=hh, d=d, cs=cs, g_row=g_row, tinv=eye_f + nmat, pw=_split(nmat),
                    bv=_split(beta * vn), bk=_split((beta * eg) * kn), egq=eg * qn,
                    qk=(qkr * decay).astype(BF16), kd=(jnp.exp(g_end - g_col) * kn).astype(BF16)))
        for _ in range(5):
            sq = [_dot3s(c['pw'], c['pw']) for c in chains]
            for c, s in zip(chains, sq):
                c['pw'] = _split(s)
            up = [_dot3s(_split(c['tinv']), c['pw']) for c in chains]
            for c, u in zip(chains, up):
                c['tinv'] = c['tinv'] + u
        for c in chains:
            c['ts'] = _split(c['tinv'])
        for c in chains:
            c['uvb'] = _dot3s(c['ts'], c['bv']).astype(BF16)
        for c in chains:
            c['wkb'] = _dot3s(c['ts'], c['bk']).astype(BF16)
        for c in chains:
            qe_scr[c['d'], rows, c['cs']] = (c['egq'] - _dot(c['qk'], c['wkb'])).astype(BF16)
            oc_scr[c['d'], rows, c['cs']] = _dot(c['qk'], c['uvb'])
        for c in chains:
            d, j = c['d'], c['d'] * DN_H + c['hh']
            for half in range(2):
                sl = slice(half * CHUNK, (half + 1) * CHUNK)
                n = p * 2 + half
                mrows = pl.ds(pl.multiple_of(n * DN_DK, DN_DK), DN_DK)
                m_scr[j, mrows, :] = _dot_tn(c['kd'][sl], c['wkb'][sl]).astype(BF16)
                c_scr[j, mrows, :] = _dot_tn(c['kd'][sl], c['uvb'][sl])
                e = half * CHUNK + (CHUNK - 1 if d == 0 else 0)
                gt_scr[j, pl.ds(pl.multiple_of(n * 8, 8), 8), :] = jnp.broadcast_to(
                    jnp.exp(c['g_row'][:, e:e + 1]), (8, LANE))
        return carry

    lax.fori_loop(0, npair, prep, 0)

    for d in range(2):
        for hh in range(DN_H):
            sfin_ref[0, d, hh] = s0_ref[0, 0, d, hh] if has_s0 else jnp.zeros((DN_DK, DN_DK), F32)

    def step(i, carry):
        for d in range(2):
            n = i if d == 0 else nch - 1 - i
            rows = pl.ds(pl.multiple_of(n * CHUNK, CHUNK), CHUNK)
            mrows = pl.ds(pl.multiple_of(n * DN_DK, DN_DK), DN_DK)
            o_scr = of_scr if d == 0 else ob_scr
            for hh in range(DN_H):
                st = sfin_ref[0, d, hh]
                sb = st.astype(BF16)
                o_scr[rows, hcols(hh)] = _dot(qe_scr[d, rows, hcols(hh)], sb) + oc_scr[d, rows, hcols(hh)]
                gt = gt_scr[d * DN_H + hh, pl.ds(pl.multiple_of(n * 8, 8), 8), :][0:1, 0:1]
                sfin_ref[0, d, hh] = (gt * st - _dot(m_scr[d * DN_H + hh, mrows, :], sb)
                                      + c_scr[d * DN_H + hh, mrows, :])
        return carry

    lax.fori_loop(0, nch, step, 0)

    for hh in range(DN_H):
        o = of_scr[:, hcols(hh)] + ob_scr[:, hcols(hh)]
        y = o * lax.rsqrt(jnp.mean(o * o, axis=-1, keepdims=True) + EPS) * nrm_ref[...]
        y_ref[:, hcols(hh)] = y * _silu(z_ref[:, hcols(hh)].astype(F32))


def _deltanet(proj_a, proj, conv_w, alog_b, dtb_b, nrm, row_off, nseq, seq, s0, layer):
    r0 = row_off // seq
    w = DN_H * DN_DK
    spec = lambda col: pl.BlockSpec((seq, w), lambda b: (r0 + b, col // w))
    wspec = lambda j: pl.BlockSpec((3, w), lambda b: (0, j))
    bspec = pl.BlockSpec((2, DN_H, 8, LANE), lambda b: (0, 0, 0, 0))
    in_specs = [spec(COL_DNQKV), spec(COL_DNQKV + w), spec(COL_DNQKV + 2 * w), spec(COL_DNZ),
                pl.BlockSpec((seq, 2 * LANE), lambda b: (r0 + b, COL_KV // (2 * LANE))),
                wspec(0), wspec(1), wspec(2), bspec, bspec,
                pl.BlockSpec((1, DN_DK), lambda b: (0, 0))]
    args = [proj, proj, proj, proj, proj_a, conv_w, conv_w, conv_w, alog_b, dtb_b, nrm.reshape(1, DN_DK)]
    if s0 is not None:
        in_specs.append(pl.BlockSpec((1, 1, 2, DN_H, DN_DK, DN_DK), lambda b: (b, layer, 0, 0, 0, 0)))
        args.append(s0)
    nch = seq // CHUNK
    scratch = [pltpu.VMEM((seq, w), F32), pltpu.VMEM((seq, w), F32),
               pltpu.VMEM((seq, w), F32), pltpu.VMEM((seq, w), F32), pltpu.VMEM((seq, w), F32),
               pltpu.VMEM((2, seq, w), BF16), pltpu.VMEM((2, seq, w), F32),
               pltpu.VMEM((2 * DN_H, nch * DN_DK, DN_DK), BF16), pltpu.VMEM((2 * DN_H, nch * DN_DK, DN_DK), F32),
               pltpu.VMEM((2 * DN_H, nch * 8, LANE), F32)]
    return pl.pallas_call(
        functools.partial(_dn_kernel, seq=seq, has_s0=s0 is not None),
        out_shape=(jax.ShapeDtypeStruct((nseq * seq, w), F32),
                   jax.ShapeDtypeStruct((nseq, 2, DN_H, DN_DK, DN_DK), F32)),
        grid=(nseq,),
        in_specs=in_specs,
        out_specs=(pl.BlockSpec((seq, w), lambda b: (b, 0)),
                   pl.BlockSpec((1, 2, DN_H, DN_DK, DN_DK), lambda b: (b, 0, 0, 0, 0))),
        scratch_shapes=scratch,
        compiler_params=_cparams(("arbitrary",)),
    )(*args)


def _branch_out_kernel(b0, b1, b2, b3, g0, g1, g2, g3, x_ref, mod_ref, nf_ref, wb_ref, wo_ref,
                       xo_ref, h_ref):
    acc = None
    for n, (br, gl) in enumerate(((b0, g0), (b1, g1), (b2, g2), (b3, g3))):
        p = _sigmoid(gl[...].astype(F32)) * _dot(br[...].astype(BF16), wb_ref[n])
        acc = p if acc is None else acc + p
    y = _dot(acc.astype(BF16), wo_ref[...])
    m = mod_ref[0]
    x = x_ref[...] + m[:, 2 * D:3 * D] * y
    xo_ref[...] = x
    h_ref[...] = _rms(x, nf_ref[...]) * (1.0 + m[:, 4 * D:5 * D]) + m[:, 3 * D:4 * D]


def _branch_out(branches, proj, x, mod3, nf, wb, wo, grp_of_tile, tm):
    t = x.shape[0]
    gb = COL_GATE // D
    bspec = pl.BlockSpec((tm, MIX_W), lambda i: (i, 0))
    gspec = lambda n: pl.BlockSpec((tm, D), lambda i: (i, gb + n))
    row = pl.BlockSpec((tm, D), lambda i: (i, 0))
    return pl.pallas_call(
        _branch_out_kernel,
        out_shape=(jax.ShapeDtypeStruct((t, D), F32), jax.ShapeDtypeStruct((t, D), F32)),
        grid=(t // tm,),
        in_specs=[bspec] * 4 + [gspec(n) for n in range(4)] + [
            row,
            pl.BlockSpec((1, 1, 6 * D), lambda i: (grp_of_tile(tm)(i), 0, 0)),
            pl.BlockSpec((1, D), lambda i: (0, 0)),
            pl.BlockSpec((4, MIX_W, D), lambda i: (0, 0, 0)),
            pl.BlockSpec((D, D), lambda i: (0, 0))],
        out_specs=(row, row),
        compiler_params=_cparams(("arbitrary",)),
    )(*branches, proj, proj, proj, proj, x, mod3, nf.reshape(1, D), wb, wo)


def _route_kernel(h_ref, wr_ref, bias_ref, idx_ref, w_ref, rank_ref, cnt_ref, run_scr, *, tm):
    @pl.when(pl.program_id(0) == 0)
    def _():
        run_scr[...] = jnp.zeros_like(run_scr)

    neg = -jnp.inf
    gsz = N_EXP // N_GRP
    scores = _sigmoid(_dot3_nt(wr_ref[...], h_ref[...]))
    choice = scores + bias_ref[:, 0:1]
    row8 = lax.broadcasted_iota(I32, (gsz, tm), 0)
    gscore = []
    for g in range(N_GRP):
        blk = choice[g * gsz:(g + 1) * gsz]
        m1 = jnp.max(blk, axis=0, keepdims=True)
        i1 = jnp.min(jnp.where(blk == m1, row8, gsz), axis=0, keepdims=True)
        m2 = jnp.max(jnp.where(row8 == i1, neg, blk), axis=0, keepdims=True)
        gscore.append(m1 + m2)
    masked = []
    for g in range(N_GRP):
        rank = jnp.zeros((1, tm), I32)
        for g2 in range(N_GRP):
            if g2 == g:
                continue
            ahead = (gscore[g2] >= gscore[g]) if g2 < g else (gscore[g2] > gscore[g])
            rank = rank + ahead.astype(I32)
        masked.append(jnp.where(rank < TOPK_GRP, choice[g * gsz:(g + 1) * gsz], neg))
    cur = jnp.concatenate(masked, axis=0)
    row = lax.broadcasted_iota(I32, (N_EXP, tm), 0)
    sel = jnp.zeros((N_EXP, tm), jnp.bool_)
    idxs, scs = [], []
    for _ in range(TOP_K):
        m = jnp.max(cur, axis=0, keepdims=True)
        ik = jnp.min(jnp.where(cur == m, row, N_EXP), axis=0, keepdims=True)
        hit = row == ik
        scs.append(jnp.sum(jnp.where(hit, scores, 0.0), axis=0, keepdims=True))
        idxs.append(ik)
        cur = jnp.where(hit, neg, cur)
        sel = sel | hit
    tot = functools.reduce(jnp.add, scs)
    self_f = jnp.where(sel, 1.0, 0.0)
    tri = (lax.broadcasted_iota(I32, (tm, tm), 0) <= lax.broadcasted_iota(I32, (tm, tm), 1))
    csum = _dot(self_f.astype(BF16), jnp.where(tri, 1.0, 0.0).astype(BF16))
    run = run_scr[:, 0:1]
    rank_all = run + csum - self_f
    ranks = [jnp.sum(jnp.where(row == ik, rank_all, 0.0), axis=0, keepdims=True) for ik in idxs]
    idx_ref[...] = jnp.concatenate(idxs, axis=0)
    w_ref[...] = jnp.concatenate([s / tot * ROUTED_SCALE for s in scs], axis=0)
    rank_ref[...] = jnp.concatenate(ranks, axis=0).astype(I32)
    new_run = run + csum[:, tm - 1:tm]
    run_scr[...] = jnp.broadcast_to(new_run, run_scr.shape)
    cnt_ref[...] = jnp.broadcast_to(new_run, cnt_ref.shape).astype(I32)


def _route(h, wr_t, bias_b, tm):
    t = h.shape[0]
    tok = pl.BlockSpec((TOP_K, tm), lambda i: (0, i))
    return pl.pallas_call(
        functools.partial(_route_kernel, tm=tm),
        out_shape=(jax.ShapeDtypeStruct((TOP_K, t), I32), jax.ShapeDtypeStruct((TOP_K, t), F32),
                   jax.ShapeDtypeStruct((TOP_K, t), I32), jax.ShapeDtypeStruct((N_EXP, LANE), I32)),
        grid=(t // tm,),
        in_specs=[pl.BlockSpec((tm, D), lambda i: (i, 0)),
                  pl.BlockSpec((N_EXP, D), lambda i: (0, 0)),
                  pl.BlockSpec((N_EXP, LANE), lambda i: (0, 0))],
        out_specs=(tok, tok, tok, pl.BlockSpec((N_EXP, LANE), lambda i: (0, 0))),
        scratch_shapes=[pltpu.VMEM((N_EXP, LANE), F32)],
        compiler_params=_cparams(("arbitrary",)),
    )(h, wr_t, bias_b)


def _slot_pos_kernel(ps_ref, idx_ref, rank_ref, pos_ref):
    idx = idx_ref[...]
    pos = rank_ref[...]
    for e in range(N_EXP):
        pos = pos + jnp.where(idx == e, ps_ref[e], 0)
    pos_ref[...] = pos


def _slot_pos(pstart, idx_t, rank_t, tm):
    t = idx_t.shape[1]
    tok = pl.BlockSpec((TOP_K, tm), lambda i, *_: (0, i))
    return pl.pallas_call(
        _slot_pos_kernel,
        out_shape=jax.ShapeDtypeStruct((TOP_K, t), I32),
        grid_spec=pltpu.PrefetchScalarGridSpec(
            num_scalar_prefetch=1, grid=(t // tm,), in_specs=[tok, tok], out_specs=tok),
        compiler_params=_cparams(("arbitrary",)),
    )(pstart, idx_t, rank_t)


def _dispatch_kernel(pstart_ref, padded_ref, pos_ref, h_ref, xs_ref, zero_scr, sem, zsem, *, tm):
    def zero_copy(e):
        start = pl.multiple_of(pstart_ref[e] + padded_ref[e] - MOE_BM, MOE_BM)
        return pltpu.make_async_copy(zero_scr, xs_ref.at[pl.ds(start, MOE_BM)], zsem)

    @pl.when(pl.program_id(0) == 0)
    def _():
        zero_scr[...] = jnp.zeros_like(zero_scr)

        def zstart(e, c):
            @pl.when(padded_ref[e] > 0)
            def _():
                zero_copy(e).start()
            return c

        def zwait(e, c):
            @pl.when(padded_ref[e] > 0)
            def _():
                zero_copy(e).wait()
            return c

        lax.fori_loop(0, N_EXP, zstart, 0)
        lax.fori_loop(0, N_EXP, zwait, 0)

    def row_copy(t, k):
        return pltpu.make_async_copy(h_ref.at[pl.ds(t, 1)], xs_ref.at[pl.ds(pos_ref[k, t], 1)], sem)

    def issue(t, c):
        for k in range(TOP_K):
            row_copy(t, k).start(priority=k % 2)
        return c

    def drain(t, c):
        for k in range(TOP_K):
            row_copy(t, k).wait()
        return c

    lax.fori_loop(0, tm, issue, 0)
    lax.fori_loop(0, tm, drain, 0)


def _dispatch(pstart, padded, pos_t, h, n_slots, tm):
    t = h.shape[0]
    smem_tok = pl.BlockSpec((TOP_K, tm), lambda i, *_: (0, i), memory_space=pltpu.SMEM)
    return pl.pallas_call(
        functools.partial(_dispatch_kernel, tm=tm),
        out_shape=jax.ShapeDtypeStruct((n_slots, D), F32),
        grid_spec=pltpu.PrefetchScalarGridSpec(
            num_scalar_prefetch=2,
            grid=(t // tm,),
            in_specs=[smem_tok, pl.BlockSpec((tm, D), lambda i, *_: (i, 0))],
            out_specs=pl.BlockSpec(memory_space=pl.ANY),
            scratch_shapes=[pltpu.VMEM((MOE_BM, D), F32), pltpu.SemaphoreType.DMA(()),
                            pltpu.SemaphoreType.DMA(())]),
        compiler_params=_cparams(("arbitrary",)),
    )(pstart, padded, pos_t, h)


def _experts_kernel(be_ref, nu_ref, x_ref, wg_ref, wu_ref, wd_ref, y_ref):
    @pl.when(pl.program_id(0) < nu_ref[0])
    def _():
        x = x_ref[...].astype(BF16)
        a = _silu(_dot(x, wg_ref[0].astype(BF16))) * _dot(x, wu_ref[0].astype(BF16))
        y_ref[...] = _dot(a.astype(BF16), wd_ref[0].astype(BF16))


def _experts(block_e, n_used, xs, wg, wu, wd):
    nb = xs.shape[0] // MOE_BM
    blk = lambda b, be, nu: (jnp.minimum(b, nu[0] - 1), 0)
    wsel = lambda b, be, nu: (be[b], 0, 0)
    return pl.pallas_call(
        _experts_kernel,
        out_shape=jax.ShapeDtypeStruct(xs.shape, F32),
        grid_spec=pltpu.PrefetchScalarGridSpec(
            num_scalar_prefetch=2,
            grid=(nb,),
            in_specs=[pl.BlockSpec((MOE_BM, D), blk),
                      pl.BlockSpec((1, D, D_EXP), wsel),
                      pl.BlockSpec((1, D, D_EXP), wsel),
                      pl.BlockSpec((1, D_EXP, D), wsel)],
            out_specs=pl.BlockSpec((MOE_BM, D), blk)),
        compiler_params=_cparams(("arbitrary",)),
    )(block_e, n_used, xs, wg, wu, wd)


def _combine_kernel(*refs, tm, final):
    it = iter(refs)
    pos_ref, w_ref, h_ref, x_ref, mod_ref, wsg_ref, wsu_ref, wsd_ref = (next(it) for _ in range(8))
    if final:
        fn_ref = next(it)
    ys_ref = next(it)
    xo_ref = next(it)
    if final:
        yo_ref = next(it)
    buf, sem = next(it), next(it)

    def row_copy(t, k):
        return pltpu.make_async_copy(ys_ref.at[pl.ds(pos_ref[k, t], 1)], buf.at[k, pl.ds(t, 1)], sem)

    def issue(t, c):
        for k in range(TOP_K):
            row_copy(t, k).start(priority=k % 2)
        return c

    def drain(t, c):
        for k in range(TOP_K):
            row_copy(t, k).wait()
        return c

    lax.fori_loop(0, tm, issue, 0)
    hb = h_ref[...].astype(BF16)
    a = _silu(_dot(hb, wsg_ref[...])) * _dot(hb, wsu_ref[...])
    shared = _dot(a.astype(BF16), wsd_ref[...])
    lax.fori_loop(0, tm, drain, 0)
    w = w_ref[...]
    routed = buf[0] * w[:, 0:1]
    for k in range(1, TOP_K):
        routed = routed + buf[k] * w[:, k:k + 1]
    m = mod_ref[0]
    x = x_ref[...] + m[:, 5 * D:6 * D] * (routed + shared)
    xo_ref[...] = x
    if final:
        yo_ref[...] = _rms(x, fn_ref[...])


def _combine(pos_t, w_tok, h, x, mod3, wsg, wsu, wsd, fn, ys, grp_of_tile, tm):
    t = h.shape[0]
    final = fn is not None
    row = pl.BlockSpec((tm, D), lambda i: (i, 0))
    full = lambda shp: pl.BlockSpec(shp, lambda i: (0,) * len(shp))
    in_specs = [pl.BlockSpec((TOP_K, tm), lambda i: (0, i), memory_space=pltpu.SMEM),
                pl.BlockSpec((tm, TOP_K), lambda i: (i, 0)), row, row,
                pl.BlockSpec((1, 1, 6 * D), lambda i: (grp_of_tile(tm)(i), 0, 0)),
                full((D, D_EXP)), full((D, D_EXP)), full((D_EXP, D))]
    args = [pos_t, w_tok, h, x, mod3, wsg, wsu, wsd]
    if final:
        in_specs.append(full((1, D)))
        args.append(fn.reshape(1, D))
    in_specs.append(pl.BlockSpec(memory_space=pl.ANY))
    args.append(ys)
    out_shape = [jax.ShapeDtypeStruct((t, D), F32)]
    out_specs = [row]
    if final:
        out_shape.append(jax.ShapeDtypeStruct((t, D), F32))
        out_specs.append(row)
    return pl.pallas_call(
        functools.partial(_combine_kernel, tm=tm, final=final),
        out_shape=tuple(out_shape),
        grid=(t // tm,),
        in_specs=in_specs,
        out_specs=tuple(out_specs),
        scratch_shapes=[pltpu.VMEM((TOP_K, tm, D), F32), pltpu.SemaphoreType.DMA(())],
        compiler_params=_cparams(("arbitrary",)),
    )(*args)


def _pack_w_in(w):
    cuts = np.cumsum([0, 256, 160, 1536, 512, 16, 2048, 1536, 4096])
    seg = lambda i: w[:, cuts[i]:cuts[i + 1]]
    pad = jnp.zeros((D, 2 * LANE - 160 - 16), w.dtype)
    wa = jnp.concatenate([seg(0), seg(1), seg(4), pad], axis=1).astype(BF16)
    wb = jnp.concatenate([seg(7), seg(2), seg(3), seg(5), seg(6)], axis=1).astype(BF16)
    return wa, wb


def _pack_w_uq(w):
    w = w.reshape(MLA_QR, MLA_H, MLA_NOPE + MLA_ROPE)
    w = jnp.pad(w, ((0, 0), (0, 0), (0, LANE - MLA_NOPE - MLA_ROPE)))
    return w.reshape(MLA_QR, MLA_H * LANE).astype(BF16)


def _pack_w_ukv(w):
    w = w.reshape(MLA_KVR, MLA_H, MLA_NOPE + MLA_V)
    k_nope = jnp.pad(w[:, :, :MLA_NOPE], ((0, 0), (0, 0), (0, LANE - MLA_NOPE)))
    eye = jnp.eye(MLA_ROPE, dtype=w.dtype)[:, None, :]
    k_pe = jnp.pad(jnp.broadcast_to(eye, (MLA_ROPE, MLA_H, MLA_ROPE)),
                   ((0, LANE - MLA_ROPE), (0, 0), (MLA_NOPE, LANE - MLA_NOPE - MLA_ROPE)))
    wk = jnp.concatenate([k_nope, k_pe], axis=0).reshape(2 * LANE, MLA_H * LANE)
    wv = w[:, :, MLA_NOPE:].reshape(MLA_KVR, MLA_H * MLA_V)
    return wk.astype(BF16), wv.astype(BF16)


def _axial_angles(n_tok, dim):
    nf = dim // 4
    inv = ROPE_BASE ** (-jnp.arange(nf, dtype=F32) / nf)
    r = jnp.repeat(jnp.arange(n_tok // GRID_W, dtype=F32), GRID_W)
    cc = jnp.tile(jnp.arange(GRID_W, dtype=F32), n_tok // GRID_W)
    ang = jnp.concatenate([r[:, None] * inv, cc[:, None] * inv], axis=-1)
    return jnp.cos(ang), jnp.sin(ang)


def _rope_tables_mla(n_tok, lane0):
    cos, sin = _axial_angles(n_tok, MLA_ROPE)
    half = MLA_ROPE // 2
    z = lambda w: jnp.zeros((n_tok, w), F32)
    o = lambda w: jnp.ones((n_tok, w), F32)
    rest = LANE - lane0 - MLA_ROPE
    c = jnp.concatenate([o(lane0), cos, cos, o(rest)], axis=1)
    s1 = jnp.concatenate([z(lane0), -sin, z(half), z(rest)], axis=1)
    s2 = jnp.concatenate([z(lane0), z(half), sin, z(rest)], axis=1)
    return c, s1, s2


def _rope_tables_ret(n_tok):
    cos, sin = _axial_angles(n_tok, RET_DK)
    return jnp.concatenate([cos, cos], axis=1), jnp.concatenate([-sin, sin], axis=1)


def _bcast_dh(a):
    return jnp.broadcast_to(a.astype(F32)[:, :, None, None], a.shape + (8, LANE))


def _forward(x_prompt, x_sample, c, cache_ckv, cache_kpe, state_dn, state_ret, c_ctx, P, tiles):
    nb, sc, _ = x_prompt.shape
    nl, sl, _ = x_sample.shape
    past = cache_ckv.shape[2]
    depth = P['w_in'].shape[0]
    t_ctx, t_lat = nb * sc, nl * sl
    t = t_ctx + t_lat
    tm = tiles['tm']
    assert sc % tm == 0 and sl % tm == 0 and t_ctx % sl == 0

    assert sl % tiles['tc'] == 0 and t_ctx % tiles['tc'] == 0

    def grp_of_tile(rows_per_tile):
        nct = t_ctx // rows_per_tile
        return lambda i: jnp.where(i < nct, 0, 1 + (i - nct) // (sl // rows_per_tile))

    x = jnp.concatenate([x_prompt.reshape(t_ctx, D), x_sample.reshape(t_lat, D)], axis=0)
    ngrp = 1 + nl
    cvec = jnp.concatenate([c_ctx[None], c, jnp.zeros((-(ngrp) % 8, D), F32)], axis=0)

    tab_q = _rope_tables_mla(sl, MLA_NOPE)
    tab_k = _rope_tables_mla(sl, 0)
    tab_r = _rope_tables_ret(sl)

    ckv_l, kpe_l, dn_l, ret_l = [], [], [], []
    y_final = None
    for l in range(depth):
        mod3 = _ada(cvec, P['w_ada'][l], P['b_ada'][l]).reshape(cvec.shape[0], 1, 6 * D)
        w_a, w_b = _pack_w_in(P['w_in'][l])
        tp = tiles['tp']
        proj_a = _in_proj(x, mod3, P['norm_mix'][l], w_a, grp_of_tile, tp, N_PROJ_A, F32)
        proj = _in_proj(x, mod3, P['norm_mix'][l], w_b, grp_of_tile, tp, tiles['tn'], BF16)

        w_uq = _pack_w_uq(P['mla_w_uq'][l])
        wk, wv = _pack_w_ukv(P['mla_w_ukv'][l])
        gq, gkv = P['mla_q_norm'][l], P['mla_kv_norm'][l]
        q_c = _mla_q(proj_a, gq, w_uq, 0, t_ctx, tm, None, sc)
        q_l = _mla_q(proj_a, gq, w_uq, t_ctx, t_lat, tm, tab_q, sl)
        kvb = COL_KV // (2 * LANE)
        ckv_c, kpe_c, k_c, v_c = _mla_kv(proj_a, kvb, gkv, wk, wv, 0, t_ctx, tm, True, None, sc)
        _, _, k_l, v_l = _mla_kv(proj_a, kvb, gkv, wk, wv, t_ctx, t_lat, tm, True, tab_k, sl)
        cached = jnp.concatenate([cache_ckv[:, l], cache_kpe[:, l],
                                  jnp.zeros((nl, past, 2 * LANE - MLA_KVR - MLA_ROPE), F32)], axis=-1)
        pt = min(tm, past)
        _, _, k_p, v_p = _mla_kv(cached.reshape(nl * past, 2 * LANE), 0, gkv, wk, wv, 0, nl * past, pt,
                                 False, None, past)
        r3 = lambda a, n, s: a.reshape(n, s, a.shape[-1])
        y_mla_c = _attn(r3(q_c, nb, sc), [r3(k_c, nb, sc)], [r3(v_c, nb, sc)], min(sc, tiles['tq']))
        y_mla_l = _attn(r3(q_l, nl, sl), [r3(k_p, nl, past), r3(k_l, nl, sl)],
                        [r3(v_p, nl, past), r3(v_l, nl, sl)], min(sl, tiles['tq']))
        y_mla = jnp.concatenate([y_mla_c.reshape(t_ctx, MIX_W), y_mla_l.reshape(t_lat, MIX_W)], axis=0)

        alog_b, dtb_b = _bcast_dh(P['dn_A_log'][l]), _bcast_dh(P['dn_dt_bias'][l])
        dn_args = (proj_a, proj, P['dn_conv'][l], alog_b, dtb_b, P['dn_norm'][l])
        y_dn_c, dn_fin = _deltanet(*dn_args, 0, nb, sc, None, l)
        y_dn_l, _ = _deltanet(*dn_args, t_ctx, nl, sl, state_dn, l)
        y_dn = jnp.concatenate([y_dn_c, y_dn_l], axis=0)

        dec_b = _bcast_dh(P['ret_decay'][l])
        y_ret_c, ret_fin = _retention(proj, dec_b, P['ret_gn'][l], 0, nb, sc, None, None, l)
        y_ret_l, _ = _retention(proj, dec_b, P['ret_gn'][l], t_ctx, nl, sl, tab_r, state_ret, l)
        y_ret = jnp.concatenate([y_ret_c, y_ret_l], axis=0)

        y_sc = jnp.concatenate([_sconv(proj, P['sc_conv'][l], 0, nb, sc),
                                _sconv(proj, P['sc_conv'][l], t_ctx, nl, sl)], axis=0)

        x_mid, h2 = _branch_out((y_mla, y_dn, y_ret, y_sc), proj, x, mod3, P['norm_ffn'][l],
                                P['w_branch'][l].astype(BF16), P['w_out'][l].astype(BF16), grp_of_tile, tm)

        bias_b = jnp.broadcast_to(P['router_bias'][l].astype(F32)[:, None], (N_EXP, LANE))
        idx_t, w_t, rank_t, cnt = _route(h2, P['router'][l].T, bias_b, tiles['tr'])
        counts = cnt[:, 0]
        padded = (counts + MOE_BM - 1) // MOE_BM * MOE_BM
        pad_end = jnp.cumsum(padded)
        pstart = (pad_end - padded).astype(I32)
        n_blocks = (t * TOP_K) // MOE_BM + N_EXP
        blk_row0 = jnp.arange(n_blocks, dtype=I32) * MOE_BM
        block_e = jnp.minimum(jnp.sum((pad_end[None, :] <= blk_row0[:, None]).astype(I32), axis=1), N_EXP - 1)
        n_used = (pad_end[-1:] // MOE_BM).astype(I32)
        pos_t = _slot_pos(pstart, idx_t, rank_t, tiles['tr'])
        xs = _dispatch(pstart, padded.astype(I32), pos_t, h2, n_blocks * MOE_BM, tiles['td'])
        ys = _experts(block_e, n_used, xs, P['w_eg'][l], P['w_eu'][l], P['w_ed'][l])
        fn = P['final_norm'] if l == depth - 1 else None
        outs = _combine(pos_t, w_t.T, h2, x_mid, mod3, P['w_sg'][l].astype(BF16),
                        P['w_su'][l].astype(BF16), P['w_sd'][l].astype(BF16), fn, ys, grp_of_tile, tiles['tc'])
        x = outs[0]
        if fn is not None:
            y_final = outs[1]

        ckv_l.append(ckv_c.reshape(nb, sc, MLA_KVR))
        kpe_l.append(kpe_c.reshape(nb, sc, MLA_ROPE))
        dn_l.append(dn_fin)
        ret_l.append(ret_fin)

    y_prompt = y_final[:t_ctx].reshape(nb, sc, D)
    y_sample = y_final[t_ctx:].reshape(nl, sl, D)
    return (y_prompt, y_sample, jnp.stack(ckv_l, axis=1), jnp.stack(kpe_l, axis=1),
            jnp.stack(dn_l, axis=1), jnp.stack(ret_l, axis=1))


_TILES = dict(tm=256, tp=1024, tn=512, tq=256, tr=512, td=256, tc=128)


def kernel(x_prompt, x_sample, c, cache_ckv, cache_kpe, state_dn, state_ret, c_ctx, w_ada, b_ada, norm_mix, norm_ffn, w_in, mla_q_norm, mla_w_uq, mla_kv_norm, mla_w_ukv, dn_conv, dn_A_log, dn_dt_bias, dn_norm, ret_decay, ret_gn, sc_conv, w_branch, w_out, router, router_bias, w_eg, w_eu, w_ed, w_sg, w_su, w_sd, final_norm):
    P = dict(w_ada=w_ada, b_ada=b_ada, norm_mix=norm_mix, norm_ffn=norm_ffn, w_in=w_in,
             mla_q_norm=mla_q_norm, mla_w_uq=mla_w_uq, mla_kv_norm=mla_kv_norm, mla_w_ukv=mla_w_ukv,
             dn_conv=dn_conv, dn_A_log=dn_A_log, dn_dt_bias=dn_dt_bias, dn_norm=dn_norm,
             ret_decay=ret_decay, ret_gn=ret_gn, sc_conv=sc_conv, w_branch=w_branch, w_out=w_out,
             router=router, router_bias=router_bias, w_eg=w_eg, w_eu=w_eu, w_ed=w_ed,
             w_sg=w_sg, w_su=w_su, w_sd=w_sd, final_norm=final_norm)
    return _forward(x_prompt, x_sample, c, cache_ckv, cache_kpe, state_dn, state_ret, c_ctx, P, _TILES)
```

```python
import functools
import math

import jax
import jax.numpy as jnp
import numpy as np
from jax import lax
from jax.experimental import pallas as pl
from jax.experimental.pallas import tpu as pltpu

F32 = jnp.float32
BF16 = jnp.bfloat16
I32 = jnp.int32
U32 = jnp.uint32

D = 1024
EPS = 1e-6
CHUNK = 64
PAIR = 2 * CHUNK
GRID_W = 64
ROPE_BASE = 10000.0

MLA_H, MLA_NOPE, MLA_ROPE, MLA_V, MLA_QR, MLA_KVR = 8, 64, 32, 64, 256, 128
DN_H, DN_DK = 4, 128
RET_H, RET_DK = 4, 128
MIX_W = 512
N_EXP, TOP_K, N_GRP, TOPK_GRP, D_EXP = 64, 8, 8, 4, 256
ROUTED_SCALE = 2.5

LANE = 128
COL_CQ = 0
COL_KV = 256
N_PROJ_A = 512
AB_LANE = 160
COL_GATE = 0
COL_DNQKV = 4096
COL_DNZ = 5632
COL_RET = 6144
COL_SC = 8192
N_PROJ_B = 9728

MOE_BM = 256
VMEM_LIMIT = 56 * 1024 * 1024


def _cparams(sem, vmem=None):
    return pltpu.CompilerParams(dimension_semantics=sem, vmem_limit_bytes=vmem or VMEM_LIMIT)


def _dot(a, b):
    return jnp.dot(a, b, preferred_element_type=F32)


def _dot_nt(a, b):
    return lax.dot_general(a, b, (((1,), (1,)), ((), ())), preferred_element_type=F32)


def _dot_tn(a, b):
    return lax.dot_general(a, b, (((0,), (0,)), ((), ())), preferred_element_type=F32)


def _split(a):
    hi = a.astype(BF16)
    lo = (a - hi.astype(F32)).astype(BF16)
    return hi, lo


def _dot3(a, b):
    ah, al = _split(a)
    bh, bl = _split(b)
    return _dot(ah, bh) + (_dot(ah, bl) + _dot(al, bh))


def _dot3s(a, b):
    (ah, al), (bh, bl) = a, b
    return _dot(jnp.concatenate([ah, ah, al], axis=1), jnp.concatenate([bh, bl, bh], axis=0))


def _dot3_nt(a, b):
    ah, al = _split(a)
    bh, bl = _split(b)
    return _dot_nt(ah, bh) + (_dot_nt(ah, bl) + _dot_nt(al, bh))


def _sigmoid(x):
    return 1.0 / (1.0 + jnp.exp(-x))


def _silu(x):
    return x * _sigmoid(x)


def _rms(x, g):
    return x * lax.rsqrt(jnp.mean(x * x, axis=-1, keepdims=True) + EPS) * g


def _ada_kernel(c_ref, w_ref, b_ref, o_ref):
    o_ref[...] = _dot3(_silu(c_ref[...]), w_ref[...]) + b_ref[...]


def _ada(cvec, w, b):
    n = w.shape[1]
    tn = 1024
    return pl.pallas_call(
        _ada_kernel,
        out_shape=jax.ShapeDtypeStruct((cvec.shape[0], n), F32),
        grid=(n // tn,),
        in_specs=[pl.BlockSpec(cvec.shape, lambda j: (0, 0)),
                  pl.BlockSpec((D, tn), lambda j: (0, j)),
                  pl.BlockSpec((1, tn), lambda j: (0, j))],
        out_specs=pl.BlockSpec((cvec.shape[0], tn), lambda j: (0, j)),
        compiler_params=_cparams(("arbitrary",)),
    )(cvec, w, b.reshape(1, n))


def _in_proj_kernel(x_ref, mod_ref, g_ref, w_ref, o_ref, h_scr):
    @pl.when(pl.program_id(1) == 0)
    def _():
        m = mod_ref[0]
        y = _rms(x_ref[...], g_ref[...])
        h_scr[...] = (y * (1.0 + m[:, D:2 * D]) + m[:, 0:D]).astype(BF16)

    o_ref[...] = _dot(h_scr[...], w_ref[...]).astype(o_ref.dtype)


def _in_proj(x, mod3, g, w, grp_of_tile, tm, tn, out_dtype):
    t = x.shape[0]
    n = w.shape[1]
    return pl.pallas_call(
        _in_proj_kernel,
        out_shape=jax.ShapeDtypeStruct((t, n), out_dtype),
        grid=(t // tm, n // tn),
        in_specs=[pl.BlockSpec((tm, D), lambda i, j: (i, 0)),
                  pl.BlockSpec((1, 1, 6 * D), lambda i, j: (grp_of_tile(tm)(i), 0, 0)),
                  pl.BlockSpec((1, D), lambda i, j: (0, 0)),
                  pl.BlockSpec((D, tn), lambda i, j: (0, j))],
        out_specs=pl.BlockSpec((tm, tn), lambda i, j: (i, j)),
        scratch_shapes=[pltpu.VMEM((tm, D), BF16)],
        compiler_params=_cparams(("arbitrary", "arbitrary")),
    )(x, mod3, g.reshape(1, D), w)


def _rope3(x, c, s1, s2, width):
    return x * c + pltpu.roll(x, width - 16, 1) * s1 + pltpu.roll(x, 16, 1) * s2


def _mla_q_kernel(*refs, rope):
    if rope:
        p_ref, g_ref, w_ref, c_ref, s1_ref, s2_ref, o_ref = refs
    else:
        p_ref, g_ref, w_ref, o_ref = refs
    y = _rms(p_ref[...], g_ref[...])
    q = _dot(y.astype(BF16), w_ref[...])
    if rope:
        tile = lambda r: jnp.concatenate([r[...]] * MLA_H, axis=1)
        q = _rope3(q, tile(c_ref), tile(s1_ref), tile(s2_ref), MLA_H * LANE)
    o_ref[...] = q.astype(BF16)


def _mla_q(proj, g, w, row_off, rows, tm, rope_tabs, seq):
    nrow = rows // tm
    r0 = row_off // tm
    in_specs = [pl.BlockSpec((tm, MLA_QR), lambda i: (r0 + i, COL_CQ // MLA_QR)),
                pl.BlockSpec((1, MLA_QR), lambda i: (0, 0)),
                pl.BlockSpec((MLA_QR, MLA_H * LANE), lambda i: (0, 0))]
    args = [proj, g.reshape(1, MLA_QR), w]
    if rope_tabs is not None:
        per = seq // tm
        in_specs += [pl.BlockSpec((tm, LANE), lambda i: (i % per, 0))] * 3
        args += list(rope_tabs)
    return pl.pallas_call(
        functools.partial(_mla_q_kernel, rope=rope_tabs is not None),
        out_shape=jax.ShapeDtypeStruct((rows, MLA_H * LANE), BF16),
        grid=(nrow,),
        in_specs=in_specs,
        out_specs=pl.BlockSpec((tm, MLA_H * LANE), lambda i: (i, 0)),
        compiler_params=_cparams(("arbitrary",)),
    )(*args)


def _mla_kv_kernel(*refs, norm, rope):
    if rope:
        p_ref, g_ref, wk_ref, wv_ref, c_ref, s1_ref, s2_ref, ckv_ref, kpe_ref, k_ref, v_ref = refs
    else:
        p_ref, g_ref, wk_ref, wv_ref, ckv_ref, kpe_ref, k_ref, v_ref = refs
    blk = p_ref[...]
    ckv = blk[:, :MLA_KVR]
    if norm:
        ckv = _rms(ckv, g_ref[...])
    kp = blk[:, MLA_KVR:]
    ckv_ref[...] = ckv
    kpe_ref[...] = kp[:, :MLA_ROPE]
    if rope:
        kp = _rope3(kp, c_ref[...], s1_ref[...], s2_ref[...], LANE)
    a = jnp.concatenate([ckv, kp], axis=1).astype(BF16)
    k_ref[...] = _dot(a, wk_ref[...]).astype(BF16)
    v_ref[...] = _dot(ckv.astype(BF16), wv_ref[...]).astype(BF16)


def _mla_kv(src, col_blk, g, wk, wv, row_off, rows, tm, norm, rope_tabs, seq):
    nrow = rows // tm
    r0 = row_off // tm
    in_specs = [pl.BlockSpec((tm, 2 * LANE), lambda i: (r0 + i, col_blk)),
                pl.BlockSpec((1, MLA_KVR), lambda i: (0, 0)),
                pl.BlockSpec((2 * LANE, MLA_H * LANE), lambda i: (0, 0)),
                pl.BlockSpec((MLA_KVR, MLA_H * MLA_V), lambda i: (0, 0))]
    args = [src, g.reshape(1, MLA_KVR), wk, wv]
    if rope_tabs is not None:
        per = seq // tm
        in_specs += [pl.BlockSpec((tm, LANE), lambda i: (i % per, 0))] * 3
        args += list(rope_tabs)
    return pl.pallas_call(
        functools.partial(_mla_kv_kernel, norm=norm, rope=rope_tabs is not None),
        out_shape=(jax.ShapeDtypeStruct((rows, MLA_KVR), F32),
                   jax.ShapeDtypeStruct((rows, MLA_ROPE), F32),
                   jax.ShapeDtypeStruct((rows, MLA_H * LANE), BF16),
                   jax.ShapeDtypeStruct((rows, MLA_H * MLA_V), BF16)),
        grid=(nrow,),
        in_specs=in_specs,
        out_specs=(pl.BlockSpec((tm, MLA_KVR), lambda i: (i, 0)),
                   pl.BlockSpec((tm, MLA_ROPE), lambda i: (i, 0)),
                   pl.BlockSpec((tm, MLA_H * LANE), lambda i: (i, 0)),
                   pl.BlockSpec((tm, MLA_H * MLA_V), lambda i: (i, 0))),
        compiler_params=_cparams(("arbitrary",)),
    )(*args)


def _attn_kernel(*refs, nseg):
    q_ref = refs[0]
    k_refs = refs[1:1 + nseg]
    v_refs = refs[1 + nseg:1 + 2 * nseg]
    o_ref = refs[1 + 2 * nseg]
    scale = (MLA_NOPE + MLA_ROPE) ** -0.5
    outs = []
    for h in range(MLA_H):
        qh = q_ref[0, :, h * LANE:(h + 1) * LANE]
        ss = [_dot_nt(qh, k_ref[0, :, h * LANE:(h + 1) * LANE]) * scale for k_ref in k_refs]
        m = functools.reduce(jnp.maximum, [jnp.max(s, axis=-1, keepdims=True) for s in ss])
        es = [jnp.exp(s - m) for s in ss]
        inv = 1.0 / functools.reduce(jnp.add, [jnp.sum(e, axis=-1, keepdims=True) for e in es])
        o = functools.reduce(jnp.add, [
            _dot((e * inv).astype(BF16), v_ref[0, :, h * MLA_V:(h + 1) * MLA_V])
            for e, v_ref in zip(es, v_refs)])
        outs.append(o)
    o_ref[...] = jnp.concatenate(outs, axis=1)


def _into(kernel, in_specs, args, prev):
    if prev is None:
        return kernel, {}
    pos = len(args)
    in_specs.append(pl.BlockSpec(memory_space=pl.ANY))
    args.append(prev)
    return (lambda *refs: kernel(*refs[:pos], *refs[pos + 1:])), {pos: 0}


def _attn(q, ks, vs, tq, t_total, row_off, prev):
    b, s, _ = q.shape
    nseg = len(ks)
    in_specs = [pl.BlockSpec((1, tq, MLA_H * LANE), lambda i, j: (i, j, 0))]
    in_specs += [pl.BlockSpec((1,) + k.shape[1:], lambda i, j: (i, 0, 0)) for k in ks]
    in_specs += [pl.BlockSpec((1,) + v.shape[1:], lambda i, j: (i, 0, 0)) for v in vs]
    args = [q, *ks, *vs]
    kern, alias = _into(functools.partial(_attn_kernel, nseg=nseg), in_specs, args, prev)
    r0, per = row_off // tq, s // tq
    return pl.pallas_call(
        kern,
        out_shape=jax.ShapeDtypeStruct((t_total, MLA_H * MLA_V), F32),
        grid=(b, per),
        in_specs=in_specs,
        out_specs=pl.BlockSpec((tq, MLA_H * MLA_V), lambda i, j: (r0 + i * per + j, 0)),
        input_output_aliases=alias,
        compiler_params=_cparams(("arbitrary", "arbitrary")),
    )(*args)


def _conv3(x, w):
    s = x.shape[0]
    row = lax.broadcasted_iota(I32, x.shape, 0)
    prev = jnp.where(row == 0, 0.0, pltpu.roll(x, 1, 0))
    nxt = jnp.where(row == s - 1, 0.0, pltpu.roll(x, s - 1, 0))
    return prev * w[0:1] + x * w[1:2] + nxt * w[2:3]


def _sconv_kernel(b_ref, c_ref, x_ref, w_ref, o_ref):
    f = lambda r: r[...].astype(F32)
    o_ref[...] = f(b_ref) * _conv3(f(c_ref) * f(x_ref), w_ref[...])


def _sconv(proj, w, row_off, nseq, seq, t_total, prev):
    r0 = row_off // seq
    cb = COL_SC // LANE
    nc = MIX_W // LANE
    spec = lambda off: pl.BlockSpec((seq, LANE), lambda b, j: (r0 + b, cb + off + j))
    in_specs = [spec(0), spec(nc), spec(2 * nc), pl.BlockSpec((3, LANE), lambda b, j: (0, j))]
    args = [proj, proj, proj, w]
    kern, alias = _into(_sconv_kernel, in_specs, args, prev)
    return pl.pallas_call(
        kern,
        out_shape=jax.ShapeDtypeStruct((t_total, MIX_W), F32),
        grid=(nseq, nc),
        in_specs=in_specs,
        out_specs=pl.BlockSpec((seq, LANE), lambda b, j: (r0 + b, j)),
        input_output_aliases=alias,
        compiler_params=_cparams(("arbitrary", "arbitrary")),
    )(*args)


def _ret_kernel(*refs, seq, rope, has_s0):
    it = iter(refs)
    q_ref, k_ref, v_ref, g_ref, dec_ref, gn_ref = (next(it) for _ in range(6))
    if rope:
        c_ref, s_ref = next(it), next(it)
    if has_s0:
        s0_ref = next(it)
    y_ref, sfin_ref, of_scr, ob_scr = (next(it) for _ in range(4))
    nch = seq // CHUNK

    q = q_ref[...].astype(F32)
    k = k_ref[...].astype(F32) * (RET_DK ** -0.5)
    if rope:
        c, sn = c_ref[...], s_ref[...]
        q = q * c + pltpu.roll(q, RET_DK // 2, 1) * sn
        k = k * c + pltpu.roll(k, RET_DK // 2, 1) * sn
    v = v_ref[...].astype(F32)

    ti = lax.broadcasted_iota(I32, (CHUNK, CHUNK), 0)
    si = lax.broadcasted_iota(I32, (CHUNK, CHUNK), 1)
    pos = lax.broadcasted_iota(I32, (CHUNK, 1), 0).astype(F32)
    for d in range(2):
        lg = -jnp.exp(dec_ref[d, 0, 0:1, 0:1])
        if d == 0:
            dist = (ti - si).astype(F32)
            qpow, kpow = pos + 1.0, (CHUNK - 1.0) - pos
        else:
            dist = (si - ti).astype(F32)
            qpow, kpow = CHUNK - pos, pos
        decay = jnp.where(dist >= 0, jnp.exp(jnp.maximum(dist, 0.0) * lg), 0.0)
        qs = jnp.exp(qpow * lg)
        ks = jnp.exp(kpow * lg)
        gtot = jnp.exp(CHUNK * lg)
        st = s0_ref[0, 0, d, 0] if has_s0 else jnp.zeros((RET_DK, RET_DK), F32)
        o_scr = of_scr if d == 0 else ob_scr
        order = range(nch) if d == 0 else range(nch - 1, -1, -1)
        for n in order:
            sl = slice(n * CHUNK, (n + 1) * CHUNK)
            qn, kn, vn = q[sl], k[sl], v[sl]
            vb = vn.astype(BF16)
            a = _dot_nt(qn.astype(BF16), kn.astype(BF16)) * decay
            o = _dot(a.astype(BF16), vb) + _dot((qn * qs).astype(BF16), st.astype(BF16))
            st = gtot * st + _dot_tn((kn * ks).astype(BF16), vb)
            o_scr[sl, :] = o
        sfin_ref[0, d, 0] = st

    o = of_scr[...] + ob_scr[...]
    dlt = o - jnp.mean(o, axis=-1, keepdims=True)
    y = dlt * lax.rsqrt(jnp.mean(dlt * dlt, axis=-1, keepdims=True) + EPS) * gn_ref[...]
    y_ref[...] = _silu(g_ref[...].astype(F32)) * y


def _retention(proj, dec_b, gn, row_off, nseq, seq, rope_tabs, s0, layer, t_total, prev):
    r0 = row_off // seq
    cb = COL_RET // LANE
    spec = lambda off: pl.BlockSpec((seq, LANE), lambda b, h: (r0 + b, cb + off + h))
    in_specs = [spec(0), spec(RET_H), spec(2 * RET_H), spec(3 * RET_H),
                pl.BlockSpec((2, 1, 8, LANE), lambda b, h: (0, h, 0, 0)),
                pl.BlockSpec((1, LANE), lambda b, h: (0, h))]
    args = [proj, proj, proj, proj, dec_b, gn.reshape(1, RET_H * RET_DK)]
    if rope_tabs is not None:
        in_specs += [pl.BlockSpec((seq, LANE), lambda b, h: (0, 0))] * 2
        args += list(rope_tabs)
    if s0 is not None:
        in_specs.append(pl.BlockSpec((1, 1, 2, 1, RET_DK, RET_DK), lambda b, h: (b, layer, 0, h, 0, 0)))
        args.append(s0)
    kern, alias = _into(functools.partial(_ret_kernel, seq=seq, rope=rope_tabs is not None, has_s0=s0 is not None),
                        in_specs, args, prev)
    return pl.pallas_call(
        kern,
        out_shape=(jax.ShapeDtypeStruct((t_total, RET_H * RET_DK), F32),
                   jax.ShapeDtypeStruct((nseq, 2, RET_H, RET_DK, RET_DK), F32)),
        grid=(nseq, RET_H),
        in_specs=in_specs,
        out_specs=(pl.BlockSpec((seq, LANE), lambda b, h: (r0 + b, h)),
                   pl.BlockSpec((1, 2, 1, RET_DK, RET_DK), lambda b, h: (b, 0, h, 0, 0))),
        scratch_shapes=[pltpu.VMEM((seq, RET_DK), F32), pltpu.VMEM((seq, RET_DK), F32)],
        input_output_aliases=alias,
        compiler_params=_cparams(("arbitrary", "arbitrary")),
    )(*args)


def _softplus(x):
    return jnp.maximum(x, 0.0) + jnp.log1p(jnp.exp(-jnp.abs(x)))


def _l2n(x):
    return x * lax.rsqrt(jnp.sum(x * x, axis=-1, keepdims=True) + EPS)


def _dn_kernel(*refs, seq, has_s0):
    it = iter(refs)
    q_ref, k_ref, v_ref, z_ref, ab_ref, wq_ref, wk_ref, wv_ref, alog_ref, dtb_ref, nrm_ref = (
        next(it) for _ in range(11))
    if has_s0:
        s0_ref = next(it)
    y_ref, sfin_ref = next(it), next(it)
    of_scr, ob_scr, q_scr, k_scr, v_scr, qe_scr, oc_scr, m_scr, c_scr, gt_scr = (next(it) for _ in range(10))
    nch = seq // CHUNK
    npair = seq // PAIR
    hcols = lambda hh: slice(hh * DN_DK, (hh + 1) * DN_DK)

    xq = _silu(_conv3(q_ref[...].astype(F32), wq_ref[...]))
    xk = _silu(_conv3(k_ref[...].astype(F32), wk_ref[...]))
    v_scr[...] = _silu(_conv3(v_ref[...].astype(F32), wv_ref[...]))
    for hh in range(DN_H):
        q_scr[:, hcols(hh)] = _l2n(xq[:, hcols(hh)]) * (DN_DK ** -0.5)
        k_scr[:, hcols(hh)] = _l2n(xk[:, hcols(hh)])

    ti = lax.broadcasted_iota(I32, (PAIR, PAIR), 0)
    si = lax.broadcasted_iota(I32, (PAIR, PAIR), 1)
    same = (ti >> 6) == (si >> 6)
    s_in = si & (CHUNK - 1)
    eye = ti == si
    eye_f = eye.astype(F32)
    lane = lax.broadcasted_iota(I32, (1, 2 * LANE), 1)

    def ab_col(blk, idx):
        return jnp.sum(jnp.where(lane == AB_LANE + idx, blk, 0.0), axis=1, keepdims=True)

    masks = []
    for d in range(2):
        masks.append((same & ((si <= ti) if d == 0 else (si >= ti)),
                      same & ((si < ti) if d == 0 else (si > ti)),
                      same & ((ti <= si) if d == 0 else (ti >= si)),
                      same & (s_in == (CHUNK - 1 if d == 0 else 0))))

    def prep(p, carry):
        rows = pl.ds(pl.multiple_of(p * PAIR, PAIR), PAIR)
        blk = ab_ref[rows, :]
        chains = []
        for hh in range(DN_H):
            cs = hcols(hh)
            qn, kn, vn = q_scr[rows, cs], k_scr[rows, cs], v_scr[rows, cs]
            kb = kn.astype(BF16)
            kk = _dot_nt(kb, kb)
            qkr = _dot_nt(qn.astype(BF16), kb)
            for d in range(2):
                incl, strict, incl_t, last_s = masks[d]
                neg_a = -jnp.exp(alog_ref[d, hh, 0:1, 0:1])
                dtb = dtb_ref[d, hh, 0:1, 0:1]
                la = neg_a * _softplus(ab_col(blk, d * DN_H + hh) + dtb)
                beta = _sigmoid(ab_col(blk, 2 * DN_H + d * DN_H + hh))
                g_row = jnp.sum(jnp.where(incl_t, la, 0.0), axis=0, keepdims=True)
                g_col = jnp.sum(jnp.where(eye, g_row, 0.0), axis=1, keepdims=True)
                g_end = jnp.sum(jnp.where(last_s, g_row, 0.0), axis=1, keepdims=True)
                decay = jnp.where(incl, jnp.exp(jnp.where(incl, g_col - g_row, 0.0)), 0.0)
                nmat = jnp.where(strict, -(beta * decay * kk), 0.0)
                eg = jnp.exp(g_col)
                chains.append(dict(
                    hh=hh, d=d, cs=cs, g_row=g_row, tinv=eye_f + nmat, pw=_split(nmat),
                    bv=_split(beta * vn), bk=_split((beta * eg) * kn), egq=eg * qn,
                    qk=(qkr * decay).astype(BF16), kd=(jnp.exp(g_end - g_col) * kn).astype(BF16)))
        for _ in range(5):
            sq = [_dot3s(c['pw'], c['pw']) for c in chains]
            for c, s in zip(chains, sq):
                c['pw'] = _split(s)
            up = [_dot3s(_split(c['tinv']), c['pw']) for c in chains]
            for c, u in zip(chains, up):
                c['tinv'] = c['tinv'] + u
        for c in chains:
            c['ts'] = _split(c['tinv'])
        for c in chains:
            c['uvb'] = _dot3s(c['ts'], c['bv']).astype(BF16)
        for c in chains:
            c['wkb'] = _dot3s(c['ts'], c['bk']).astype(BF16)
        for c in chains:
            qe_scr[c['d'], rows, c['cs']] = (c['egq'] - _dot(c['qk'], c['wkb'])).astype(BF16)
            oc_scr[c['d'], rows, c['cs']] = _dot(c['qk'], c['uvb'])
        for c in chains:
            d, j = c['d'], c['d'] * DN_H + c['hh']
            for half in range(2):
                sl = slice(half * CHUNK, (half + 1) * CHUNK)
                n = p * 2 + half
                mrows = pl.ds(pl.multiple_of(n * DN_DK, DN_DK), DN_DK)
                m_scr[j, mrows, :] = _dot_tn(c['kd'][sl], c['wkb'][sl]).astype(BF16)
                c_scr[j, mrows, :] = _dot_tn(c['kd'][sl], c['uvb'][sl])
                e = half * CHUNK + (CHUNK - 1 if d == 0 else 0)
                gt_scr[j, pl.ds(pl.multiple_of(n * 8, 8), 8), :] = jnp.broadcast_to(
                    jnp.exp(c['g_row'][:, e:e + 1]), (8, LANE))
        return carry

    lax.fori_loop(0, npair, prep, 0)

    for d in range(2):
        for hh in range(DN_H):
            sfin_ref[0, d, hh] = s0_ref[0, 0, d, hh] if has_s0 else jnp.zeros((DN_DK, DN_DK), F32)

    def step(i, carry):
        for d in range(2):
            n = i if d == 0 else nch - 1 - i
            rows = pl.ds(pl.multiple_of(n * CHUNK, CHUNK), CHUNK)
            mrows = pl.ds(pl.multiple_of(n * DN_DK, DN_DK), DN_DK)
            o_scr = of_scr if d == 0 else ob_scr
            for hh in range(DN_H):
                st = sfin_ref[0, d, hh]
                sb = st.astype(BF16)
                o_scr[rows, hcols(hh)] = _dot(qe_scr[d, rows, hcols(hh)], sb) + oc_scr[d, rows, hcols(hh)]
                gt = gt_scr[d * DN_H + hh, pl.ds(pl.multiple_of(n * 8, 8), 8), :][0:1, 0:1]
                sfin_ref[0, d, hh] = (gt * st - _dot(m_scr[d * DN_H + hh, mrows, :], sb)
                                      + c_scr[d * DN_H + hh, mrows, :])
        return carry

    lax.fori_loop(0, nch, step, 0)

    for hh in range(DN_H):
        o = of_scr[:, hcols(hh)] + ob_scr[:, hcols(hh)]
        y = o * lax.rsqrt(jnp.mean(o * o, axis=-1, keepdims=True) + EPS) * nrm_ref[...]
        y_ref[:, hcols(hh)] = y * _silu(z_ref[:, hcols(hh)].astype(F32))


def _deltanet(proj_a, proj, conv_w, alog_b, dtb_b, nrm, row_off, nseq, seq, s0, layer, t_total, prev):
    r0 = row_off // seq
    w = DN_H * DN_DK
    spec = lambda col: pl.BlockSpec((seq, w), lambda b: (r0 + b, col // w))
    wspec = lambda j: pl.BlockSpec((3, w), lambda b: (0, j))
    bspec = pl.BlockSpec((2, DN_H, 8, LANE), lambda b: (0, 0, 0, 0))
    in_specs = [spec(COL_DNQKV), spec(COL_DNQKV + w), spec(COL_DNQKV + 2 * w), spec(COL_DNZ),
                pl.BlockSpec((seq, 2 * LANE), lambda b: (r0 + b, COL_KV // (2 * LANE))),
                wspec(0), wspec(1), wspec(2), bspec, bspec,
                pl.BlockSpec((1, DN_DK), lambda b: (0, 0))]
    args = [proj, proj, proj, proj, proj_a, conv_w, conv_w, conv_w, alog_b, dtb_b, nrm.reshape(1, DN_DK)]
    if s0 is not None:
        in_specs.append(pl.BlockSpec((1, 1, 2, DN_H, DN_DK, DN_DK), lambda b: (b, layer, 0, 0, 0, 0)))
        args.append(s0)
    nch = seq // CHUNK
    scratch = [pltpu.VMEM((seq, w), F32), pltpu.VMEM((seq, w), F32),
               pltpu.VMEM((seq, w), F32), pltpu.VMEM((seq, w), F32), pltpu.VMEM((seq, w), F32),
               pltpu.VMEM((2, seq, w), BF16), pltpu.VMEM((2, seq, w), F32),
               pltpu.VMEM((2 * DN_H, nch * DN_DK, DN_DK), BF16), pltpu.VMEM((2 * DN_H, nch * DN_DK, DN_DK), F32),
               pltpu.VMEM((2 * DN_H, nch * 8, LANE), F32)]
    kern, alias = _into(functools.partial(_dn_kernel, seq=seq, has_s0=s0 is not None), in_specs, args, prev)
    return pl.pallas_call(
        kern,
        out_shape=(jax.ShapeDtypeStruct((t_total, w), F32),
                   jax.ShapeDtypeStruct((nseq, 2, DN_H, DN_DK, DN_DK), F32)),
        grid=(nseq,),
        in_specs=in_specs,
        out_specs=(pl.BlockSpec((seq, w), lambda b: (r0 + b, 0)),
                   pl.BlockSpec((1, 2, DN_H, DN_DK, DN_DK), lambda b: (b, 0, 0, 0, 0))),
        scratch_shapes=scratch,
        input_output_aliases=alias,
        compiler_params=_cparams(("arbitrary",)),
    )(*args)


def _pack_bf16_pairs(x):
    n = x.shape[1] // 2
    bits = lax.bitcast_convert_type(x.astype(BF16).astype(F32), U32)
    return (bits[:, n:] & jnp.uint32(0xFFFF0000)) | (bits[:, :n] >> 16)


def _unpack_bf16_pairs(u):
    lo = lax.bitcast_convert_type(u << 16, F32)
    hi = lax.bitcast_convert_type(u & jnp.uint32(0xFFFF0000), F32)
    return lo, hi


def _branch_out_kernel(b0, b1, b2, b3, g0, g1, g2, g3, x_ref, mod_ref, nf_ref, wb_ref, wo_ref,
                       xo_ref, h_ref, hp_ref):
    acc = None
    for n, (br, gl) in enumerate(((b0, g0), (b1, g1), (b2, g2), (b3, g3))):
        p = _sigmoid(gl[...].astype(F32)) * _dot(br[...].astype(BF16), wb_ref[n])
        acc = p if acc is None else acc + p
    y = _dot(acc.astype(BF16), wo_ref[...])
    m = mod_ref[0]
    x = x_ref[...] + m[:, 2 * D:3 * D] * y
    xo_ref[...] = x
    h = _rms(x, nf_ref[...]) * (1.0 + m[:, 4 * D:5 * D]) + m[:, 3 * D:4 * D]
    h_ref[...] = h
    hp_ref[...] = _pack_bf16_pairs(h)


def _branch_out(branches, proj, x, mod3, nf, wb, wo, grp_of_tile, tm):
    t = x.shape[0]
    gb = COL_GATE // D
    bspec = pl.BlockSpec((tm, MIX_W), lambda i: (i, 0))
    gspec = lambda n: pl.BlockSpec((tm, D), lambda i: (i, gb + n))
    row = pl.BlockSpec((tm, D), lambda i: (i, 0))
    return pl.pallas_call(
        _branch_out_kernel,
        out_shape=(jax.ShapeDtypeStruct((t, D), F32), jax.ShapeDtypeStruct((t, D), F32),
                   jax.ShapeDtypeStruct((t, D // 2), U32)),
        grid=(t // tm,),
        in_specs=[bspec] * 4 + [gspec(n) for n in range(4)] + [
            row,
            pl.BlockSpec((1, 1, 6 * D), lambda i: (grp_of_tile(tm)(i), 0, 0)),
            pl.BlockSpec((1, D), lambda i: (0, 0)),
            pl.BlockSpec((4, MIX_W, D), lambda i: (0, 0, 0)),
            pl.BlockSpec((D, D), lambda i: (0, 0))],
        out_specs=(row, row, pl.BlockSpec((tm, D // 2), lambda i: (i, 0))),
        compiler_params=_cparams(("arbitrary",)),
    )(*branches, proj, proj, proj, proj, x, mod3, nf.reshape(1, D), wb, wo)


def _route_kernel(h_ref, wr_ref, bias_ref, idx_ref, w_ref, rank_ref, cnt_ref, run_scr, *, tm):
    @pl.when(pl.program_id(0) == 0)
    def _():
        run_scr[...] = jnp.zeros_like(run_scr)

    neg = -jnp.inf
    gsz = N_EXP // N_GRP
    scores = _sigmoid(_dot3_nt(wr_ref[...], h_ref[...]))
    choice = scores + bias_ref[:, 0:1]
    row8 = lax.broadcasted_iota(I32, (gsz, tm), 0)
    gscore = []
    for g in range(N_GRP):
        blk = choice[g * gsz:(g + 1) * gsz]
        m1 = jnp.max(blk, axis=0, keepdims=True)
        i1 = jnp.min(jnp.where(blk == m1, row8, gsz), axis=0, keepdims=True)
        m2 = jnp.max(jnp.where(row8 == i1, neg, blk), axis=0, keepdims=True)
        gscore.append(m1 + m2)
    masked = []
    for g in range(N_GRP):
        rank = jnp.zeros((1, tm), I32)
        for g2 in range(N_GRP):
            if g2 == g:
                continue
            ahead = (gscore[g2] >= gscore[g]) if g2 < g else (gscore[g2] > gscore[g])
            rank = rank + ahead.astype(I32)
        masked.append(jnp.where(rank < TOPK_GRP, choice[g * gsz:(g + 1) * gsz], neg))
    cur = jnp.concatenate(masked, axis=0)
    row = lax.broadcasted_iota(I32, (N_EXP, tm), 0)
    sel = jnp.zeros((N_EXP, tm), jnp.bool_)
    idxs, scs = [], []
    for _ in range(TOP_K):
        m = jnp.max(cur, axis=0, keepdims=True)
        ik = jnp.min(jnp.where(cur == m, row, N_EXP), axis=0, keepdims=True)
        hit = row == ik
        scs.append(jnp.sum(jnp.where(hit, scores, 0.0), axis=0, keepdims=True))
        idxs.append(ik)
        cur = jnp.where(hit, neg, cur)
        sel = sel | hit
    tot = functools.reduce(jnp.add, scs)
    self_f = jnp.where(sel, 1.0, 0.0)
    tri = (lax.broadcasted_iota(I32, (tm, tm), 0) <= lax.broadcasted_iota(I32, (tm, tm), 1))
    csum = _dot(self_f.astype(BF16), jnp.where(tri, 1.0, 0.0).astype(BF16))
    run = run_scr[:, 0:1]
    rank_all = run + csum - self_f
    ranks = [jnp.sum(jnp.where(row == ik, rank_all, 0.0), axis=0, keepdims=True) for ik in idxs]
    idx_ref[...] = jnp.concatenate(idxs, axis=0)
    w_ref[...] = jnp.concatenate([s / tot * ROUTED_SCALE for s in scs], axis=0)
    rank_ref[...] = jnp.concatenate(ranks, axis=0).astype(I32)
    new_run = run + csum[:, tm - 1:tm]
    run_scr[...] = jnp.broadcast_to(new_run, run_scr.shape)
    cnt_ref[...] = jnp.broadcast_to(new_run, cnt_ref.shape).astype(I32)


def _route(h, wr_t, bias_b, tm):
    t = h.shape[0]
    tok = pl.BlockSpec((TOP_K, tm), lambda i: (0, i))
    return pl.pallas_call(
        functools.partial(_route_kernel, tm=tm),
        out_shape=(jax.ShapeDtypeStruct((TOP_K, t), I32), jax.ShapeDtypeStruct((TOP_K, t), F32),
                   jax.ShapeDtypeStruct((TOP_K, t), I32), jax.ShapeDtypeStruct((N_EXP, LANE), I32)),
        grid=(t // tm,),
        in_specs=[pl.BlockSpec((tm, D), lambda i: (i, 0)),
                  pl.BlockSpec((N_EXP, D), lambda i: (0, 0)),
                  pl.BlockSpec((N_EXP, LANE), lambda i: (0, 0))],
        out_specs=(tok, tok, tok, pl.BlockSpec((N_EXP, LANE), lambda i: (0, 0))),
        scratch_shapes=[pltpu.VMEM((N_EXP, LANE), F32)],
        compiler_params=_cparams(("arbitrary",)),
    )(h, wr_t, bias_b)


def _slot_pos_kernel(ps_ref, idx_ref, rank_ref, pos_ref):
    idx = idx_ref[...]
    pos = rank_ref[...]
    for e in range(N_EXP):
        pos = pos + jnp.where(idx == e, ps_ref[e], 0)
    pos_ref[...] = pos


def _slot_pos(pstart, idx_t, rank_t, tm):
    t = idx_t.shape[1]
    tok = pl.BlockSpec((TOP_K, tm), lambda i, *_: (0, i))
    return pl.pallas_call(
        _slot_pos_kernel,
        out_shape=jax.ShapeDtypeStruct((TOP_K, t), I32),
        grid_spec=pltpu.PrefetchScalarGridSpec(
            num_scalar_prefetch=1, grid=(t // tm,), in_specs=[tok, tok], out_specs=tok),
        compiler_params=_cparams(("arbitrary",)),
    )(pstart, idx_t, rank_t)


def _dispatch_kernel(pstart_ref, padded_ref, pos_ref, h_ref, xs_ref, zero_scr, sem, zsem, *, tm):
    def zero_copy(e):
        start = pl.multiple_of(pstart_ref[e] + padded_ref[e] - MOE_BM, MOE_BM)
        return pltpu.make_async_copy(zero_scr, xs_ref.at[pl.ds(start, MOE_BM)], zsem)

    @pl.when(pl.program_id(0) == 0)
    def _():
        zero_scr[...] = jnp.zeros_like(zero_scr)

        def zstart(e, c):
            @pl.when(padded_ref[e] > 0)
            def _():
                zero_copy(e).start()
            return c

        def zwait(e, c):
            @pl.when(padded_ref[e] > 0)
            def _():
                zero_copy(e).wait()
            return c

        lax.fori_loop(0, N_EXP, zstart, 0)
        lax.fori_loop(0, N_EXP, zwait, 0)

    def row_copy(t, k):
        return pltpu.make_async_copy(h_ref.at[pl.ds(t, 1)], xs_ref.at[pl.ds(pos_ref[k, t], 1)], sem)

    def issue(t, c):
        for k in range(TOP_K):
            row_copy(t, k).start(priority=k % 2)
        return c

    def drain(t, c):
        for k in range(TOP_K):
            row_copy(t, k).wait()
        return c

    lax.fori_loop(0, tm, issue, 0)
    lax.fori_loop(0, tm, drain, 0)


def _dispatch(pstart, padded, pos_t, h, n_slots, tm):
    t = h.shape[0]
    smem_tok = pl.BlockSpec((TOP_K, tm), lambda i, *_: (0, i), memory_space=pltpu.SMEM)
    return pl.pallas_call(
        functools.partial(_dispatch_kernel, tm=tm),
        out_shape=jax.ShapeDtypeStruct((n_slots, h.shape[1]), h.dtype),
        grid_spec=pltpu.PrefetchScalarGridSpec(
            num_scalar_prefetch=2,
            grid=(t // tm,),
            in_specs=[smem_tok, pl.BlockSpec((tm, h.shape[1]), lambda i, *_: (i, 0))],
            out_specs=pl.BlockSpec(memory_space=pl.ANY),
            scratch_shapes=[pltpu.VMEM((MOE_BM, h.shape[1]), h.dtype), pltpu.SemaphoreType.DMA(()),
                            pltpu.SemaphoreType.DMA(())]),
        compiler_params=_cparams(("arbitrary",)),
    )(pstart, padded, pos_t, h)


def _experts_kernel(be_ref, nu_ref, x_ref, wg_ref, wu_ref, wd_ref, y_ref, wg_s, wu_s, wd_s):
    b = pl.program_id(0)

    @pl.when(jnp.logical_or(b == 0, be_ref[b] != be_ref[jnp.maximum(b - 1, 0)]))
    def _():
        wg_s[...] = wg_ref[0].astype(BF16)
        wu_s[...] = wu_ref[0].astype(BF16)
        wd_s[...] = wd_ref[0].astype(BF16)

    @pl.when(b < nu_ref[0])
    def _():
        x = jnp.concatenate(_unpack_bf16_pairs(x_ref[...]), axis=1).astype(BF16)
        a = _silu(_dot(x, wg_s[...])) * _dot(x, wu_s[...])
        y_ref[...] = _pack_bf16_pairs(_dot(a.astype(BF16), wd_s[...]))


def _experts(block_e, n_used, xs, wg, wu, wd):
    nb = xs.shape[0] // MOE_BM
    blk = lambda b, be, nu: (jnp.minimum(b, nu[0] - 1), 0)
    wsel = lambda b, be, nu: (be[b], 0, 0)
    return pl.pallas_call(
        _experts_kernel,
        out_shape=jax.ShapeDtypeStruct(xs.shape, U32),
        grid_spec=pltpu.PrefetchScalarGridSpec(
            num_scalar_prefetch=2,
            grid=(nb,),
            in_specs=[pl.BlockSpec((MOE_BM, D // 2), blk),
                      pl.BlockSpec((1, D, D_EXP), wsel),
                      pl.BlockSpec((1, D, D_EXP), wsel),
                      pl.BlockSpec((1, D_EXP, D), wsel)],
            out_specs=pl.BlockSpec((MOE_BM, D // 2), blk),
            scratch_shapes=[pltpu.VMEM((D, D_EXP), BF16), pltpu.VMEM((D, D_EXP), BF16),
                            pltpu.VMEM((D_EXP, D), BF16)]),
        compiler_params=_cparams(("arbitrary",)),
    )(block_e, n_used, xs, wg, wu, wd)


def _combine_kernel(*refs, tm, final):
    it = iter(refs)
    pos_ref, w_ref, h_ref, x_ref, mod_ref, wsg_ref, wsu_ref, wsd_ref = (next(it) for _ in range(8))
    if final:
        fn_ref = next(it)
    ys_ref = next(it)
    xo_ref = next(it)
    if final:
        yo_ref = next(it)
    buf, sem = next(it), next(it)

    def row_copy(t, k):
        return pltpu.make_async_copy(ys_ref.at[pl.ds(pos_ref[k, t], 1)], buf.at[k, pl.ds(t, 1)], sem)

    def issue(t, c):
        for k in range(TOP_K):
            row_copy(t, k).start(priority=k % 2)
        return c

    def drain(t, c):
        for k in range(TOP_K):
            row_copy(t, k).wait()
        return c

    lax.fori_loop(0, tm, issue, 0)
    hb = h_ref[...].astype(BF16)
    a = _silu(_dot(hb, wsg_ref[...])) * _dot(hb, wsu_ref[...])
    shared = _dot(a.astype(BF16), wsd_ref[...])
    lax.fori_loop(0, tm, drain, 0)
    w = w_ref[...]
    lo, hi = None, None
    for k in range(TOP_K):
        yl, yh = _unpack_bf16_pairs(buf[k])
        wk = w[:, k:k + 1]
        lo = yl * wk if lo is None else lo + yl * wk
        hi = yh * wk if hi is None else hi + yh * wk
    routed = jnp.concatenate([lo, hi], axis=1)
    m = mod_ref[0]
    x = x_ref[...] + m[:, 5 * D:6 * D] * (routed + shared)
    xo_ref[...] = x
    if final:
        yo_ref[...] = _rms(x, fn_ref[...])


def _combine(pos_t, w_tok, h, x, mod3, wsg, wsu, wsd, fn, ys, grp_of_tile, tm):
    t = h.shape[0]
    final = fn is not None
    row = pl.BlockSpec((tm, D), lambda i: (i, 0))
    full = lambda shp: pl.BlockSpec(shp, lambda i: (0,) * len(shp))
    in_specs = [pl.BlockSpec((TOP_K, tm), lambda i: (0, i), memory_space=pltpu.SMEM),
                pl.BlockSpec((tm, TOP_K), lambda i: (i, 0)), row, row,
                pl.BlockSpec((1, 1, 6 * D), lambda i: (grp_of_tile(tm)(i), 0, 0)),
                full((D, D_EXP)), full((D, D_EXP)), full((D_EXP, D))]
    args = [pos_t, w_tok, h, x, mod3, wsg, wsu, wsd]
    if final:
        in_specs.append(full((1, D)))
        args.append(fn.reshape(1, D))
    in_specs.append(pl.BlockSpec(memory_space=pl.ANY))
    args.append(ys)
    out_shape = [jax.ShapeDtypeStruct((t, D), F32)]
    out_specs = [row]
    if final:
        out_shape.append(jax.ShapeDtypeStruct((t, D), F32))
        out_specs.append(row)
    return pl.pallas_call(
        functools.partial(_combine_kernel, tm=tm, final=final),
        out_shape=tuple(out_shape),
        grid=(t // tm,),
        in_specs=in_specs,
        out_specs=tuple(out_specs),
        scratch_shapes=[pltpu.VMEM((TOP_K, tm, D // 2), U32), pltpu.SemaphoreType.DMA(())],
        compiler_params=_cparams(("arbitrary",)),
    )(*args)


def _pack_w_in(w):
    cuts = np.cumsum([0, 256, 160, 1536, 512, 16, 2048, 1536, 4096])
    seg = lambda i: w[:, cuts[i]:cuts[i + 1]]
    pad = jnp.zeros((D, 2 * LANE - 160 - 16), w.dtype)
    wa = jnp.concatenate([seg(0), seg(1), seg(4), pad], axis=1).astype(BF16)
    wb = jnp.concatenate([seg(7), seg(2), seg(3), seg(5), seg(6)], axis=1).astype(BF16)
    return wa, wb


def _pack_w_uq(w):
    w = w.reshape(MLA_QR, MLA_H, MLA_NOPE + MLA_ROPE)
    w = jnp.pad(w, ((0, 0), (0, 0), (0, LANE - MLA_NOPE - MLA_ROPE)))
    return w.reshape(MLA_QR, MLA_H * LANE).astype(BF16)


def _pack_w_ukv(w):
    w = w.reshape(MLA_KVR, MLA_H, MLA_NOPE + MLA_V)
    k_nope = jnp.pad(w[:, :, :MLA_NOPE], ((0, 0), (0, 0), (0, LANE - MLA_NOPE)))
    eye = jnp.eye(MLA_ROPE, dtype=w.dtype)[:, None, :]
    k_pe = jnp.pad(jnp.broadcast_to(eye, (MLA_ROPE, MLA_H, MLA_ROPE)),
                   ((0, LANE - MLA_ROPE), (0, 0), (MLA_NOPE, LANE - MLA_NOPE - MLA_ROPE)))
    wk = jnp.concatenate([k_nope, k_pe], axis=0).reshape(2 * LANE, MLA_H * LANE)
    wv = w[:, :, MLA_NOPE:].reshape(MLA_KVR, MLA_H * MLA_V)
    return wk.astype(BF16), wv.astype(BF16)


def _axial_angles(n_tok, dim):
    nf = dim // 4
    inv = ROPE_BASE ** (-jnp.arange(nf, dtype=F32) / nf)
    r = jnp.repeat(jnp.arange(n_tok // GRID_W, dtype=F32), GRID_W)
    cc = jnp.tile(jnp.arange(GRID_W, dtype=F32), n_tok // GRID_W)
    ang = jnp.concatenate([r[:, None] * inv, cc[:, None] * inv], axis=-1)
    return jnp.cos(ang), jnp.sin(ang)


def _rope_tables_mla(n_tok, lane0):
    cos, sin = _axial_angles(n_tok, MLA_ROPE)
    half = MLA_ROPE // 2
    z = lambda w: jnp.zeros((n_tok, w), F32)
    o = lambda w: jnp.ones((n_tok, w), F32)
    rest = LANE - lane0 - MLA_ROPE
    c = jnp.concatenate([o(lane0), cos, cos, o(rest)], axis=1)
    s1 = jnp.concatenate([z(lane0), -sin, z(half), z(rest)], axis=1)
    s2 = jnp.concatenate([z(lane0), z(half), sin, z(rest)], axis=1)
    return c, s1, s2


def _rope_tables_ret(n_tok):
    cos, sin = _axial_angles(n_tok, RET_DK)
    return jnp.concatenate([cos, cos], axis=1), jnp.concatenate([-sin, sin], axis=1)


def _bcast_dh(a):
    return jnp.broadcast_to(a.astype(F32)[:, :, None, None], a.shape + (8, LANE))


def _forward(x_prompt, x_sample, c, cache_ckv, cache_kpe, state_dn, state_ret, c_ctx, P, tiles):
    nb, sc, _ = x_prompt.shape
    nl, sl, _ = x_sample.shape
    past = cache_ckv.shape[2]
    depth = P['w_in'].shape[0]
    t_ctx, t_lat = nb * sc, nl * sl
    t = t_ctx + t_lat
    tm = tiles['tm']
    assert sc % tm == 0 and sl % tm == 0 and t_ctx % sl == 0

    assert sl % tiles['tc'] == 0 and t_ctx % tiles['tc'] == 0

    def grp_of_tile(rows_per_tile):
        nct = t_ctx // rows_per_tile
        return lambda i: jnp.where(i < nct, 0, 1 + (i - nct) // (sl // rows_per_tile))

    x = jnp.concatenate([x_prompt.reshape(t_ctx, D), x_sample.reshape(t_lat, D)], axis=0)
    ngrp = 1 + nl
    cvec = jnp.concatenate([c_ctx[None], c, jnp.zeros((-(ngrp) % 8, D), F32)], axis=0)

    tab_q = _rope_tables_mla(sl, MLA_NOPE)
    tab_k = _rope_tables_mla(sl, 0)
    tab_r = _rope_tables_ret(sl)

    ckv_l, kpe_l, dn_l, ret_l = [], [], [], []
    y_final = None
    for l in range(depth):
        mod3 = _ada(cvec, P['w_ada'][l], P['b_ada'][l]).reshape(cvec.shape[0], 1, 6 * D)
        w_a, w_b = _pack_w_in(P['w_in'][l])
        tp = tiles['tp']
        proj_a = _in_proj(x, mod3, P['norm_mix'][l], w_a, grp_of_tile, tp, N_PROJ_A, F32)
        proj = _in_proj(x, mod3, P['norm_mix'][l], w_b, grp_of_tile, tp, tiles['tn'], BF16)

        w_uq = _pack_w_uq(P['mla_w_uq'][l])
        wk, wv = _pack_w_ukv(P['mla_w_ukv'][l])
        gq, gkv = P['mla_q_norm'][l], P['mla_kv_norm'][l]
        q_c = _mla_q(proj_a, gq, w_uq, 0, t_ctx, tm, None, sc)
        q_l = _mla_q(proj_a, gq, w_uq, t_ctx, t_lat, tm, tab_q, sl)
        kvb = COL_KV // (2 * LANE)
        ckv_c, kpe_c, k_c, v_c = _mla_kv(proj_a, kvb, gkv, wk, wv, 0, t_ctx, tm, True, None, sc)
        _, _, k_l, v_l = _mla_kv(proj_a, kvb, gkv, wk, wv, t_ctx, t_lat, tm, True, tab_k, sl)
        cached = jnp.concatenate([cache_ckv[:, l], cache_kpe[:, l],
                                  jnp.zeros((nl, past, 2 * LANE - MLA_KVR - MLA_ROPE), F32)], axis=-1)
        pt = min(tm, past)
        _, _, k_p, v_p = _mla_kv(cached.reshape(nl * past, 2 * LANE), 0, gkv, wk, wv, 0, nl * past, pt,
                                 False, None, past)
        r3 = lambda a, n, s: a.reshape(n, s, a.shape[-1])
        y_mla = _attn(r3(q_c, nb, sc), [r3(k_c, nb, sc)], [r3(v_c, nb, sc)], min(sc, tiles['tq']), t, 0, None)
        y_mla = _attn(r3(q_l, nl, sl), [r3(k_p, nl, past), r3(k_l, nl, sl)],
                      [r3(v_p, nl, past), r3(v_l, nl, sl)], min(sl, tiles['tq']), t, t_ctx, y_mla)

        alog_b, dtb_b = _bcast_dh(P['dn_A_log'][l]), _bcast_dh(P['dn_dt_bias'][l])
        dn_args = (proj_a, proj, P['dn_conv'][l], alog_b, dtb_b, P['dn_norm'][l])
        y_dn, dn_fin = _deltanet(*dn_args, 0, nb, sc, None, l, t, None)
        y_dn, _ = _deltanet(*dn_args, t_ctx, nl, sl, state_dn, l, t, y_dn)

        dec_b = _bcast_dh(P['ret_decay'][l])
        y_ret, ret_fin = _retention(proj, dec_b, P['ret_gn'][l], 0, nb, sc, None, None, l, t, None)
        y_ret, _ = _retention(proj, dec_b, P['ret_gn'][l], t_ctx, nl, sl, tab_r, state_ret, l, t, y_ret)

        y_sc = _sconv(proj, P['sc_conv'][l], 0, nb, sc, t, None)
        y_sc = _sconv(proj, P['sc_conv'][l], t_ctx, nl, sl, t, y_sc)

        x_mid, h2, h2p = _branch_out((y_mla, y_dn, y_ret, y_sc), proj, x, mod3, P['norm_ffn'][l],
                                P['w_branch'][l].astype(BF16), P['w_out'][l].astype(BF16), grp_of_tile, tm)

        bias_b = jnp.broadcast_to(P['router_bias'][l].astype(F32)[:, None], (N_EXP, LANE))
        idx_t, w_t, rank_t, cnt = _route(h2, P['router'][l].T, bias_b, tiles['tr'])
        counts = cnt[:, 0]
        padded = (counts + MOE_BM - 1) // MOE_BM * MOE_BM
        pad_end = jnp.cumsum(padded)
        pstart = (pad_end - padded).astype(I32)
        n_blocks = (t * TOP_K) // MOE_BM + N_EXP
        blk_row0 = jnp.arange(n_blocks, dtype=I32) * MOE_BM
        block_e = jnp.minimum(jnp.sum((pad_end[None, :] <= blk_row0[:, None]).astype(I32), axis=1), N_EXP - 1)
        n_used = (pad_end[-1:] // MOE_BM).astype(I32)
        pos_t = _slot_pos(pstart, idx_t, rank_t, tiles['tr'])
        xs = _dispatch(pstart, padded.astype(I32), pos_t, h2p, n_blocks * MOE_BM, tiles['td'])
        ys = _experts(block_e, n_used, xs, P['w_eg'][l], P['w_eu'][l], P['w_ed'][l])
        fn = P['final_norm'] if l == depth - 1 else None
        outs = _combine(pos_t, w_t.T, h2, x_mid, mod3, P['w_sg'][l].astype(BF16),
                        P['w_su'][l].astype(BF16), P['w_sd'][l].astype(BF16), fn, ys, grp_of_tile, tiles['tc'])
        x = outs[0]
        if fn is not None:
            y_final = outs[1]

        ckv_l.append(ckv_c.reshape(nb, sc, MLA_KVR))
        kpe_l.append(kpe_c.reshape(nb, sc, MLA_ROPE))
        dn_l.append(dn_fin)
        ret_l.append(ret_fin)

    y_prompt = y_final[:t_ctx].reshape(nb, sc, D)
    y_sample = y_final[t_ctx:].reshape(nl, sl, D)
    return (y_prompt, y_sample, jnp.stack(ckv_l, axis=1), jnp.stack(kpe_l, axis=1),
            jnp.stack(dn_l, axis=1), jnp.stack(ret_l, axis=1))


_TILES = dict(tm=256, tp=1024, tn=512, tq=256, tr=512, td=256, tc=128)


def kernel(x_prompt, x_sample, c, cache_ckv, cache_kpe, state_dn, state_ret, c_ctx, w_ada, b_ada, norm_mix, norm_ffn, w_in, mla_q_norm, mla_w_uq, mla_kv_norm, mla_w_ukv, dn_conv, dn_A_log, dn_dt_bias, dn_norm, ret_decay, ret_gn, sc_conv, w_branch, w_out, router, router_bias, w_eg, w_eu, w_ed, w_sg, w_su, w_sd, final_norm):
    P = dict(w_ada=w_ada, b_ada=b_ada, norm_mix=norm_mix, norm_ffn=norm_ffn, w_in=w_in,
             mla_q_norm=mla_q_norm, mla_w_uq=mla_w_uq, mla_kv_norm=mla_kv_norm, mla_w_ukv=mla_w_ukv,
             dn_conv=dn_conv, dn_A_log=dn_A_log, dn_dt_bias=dn_dt_bias, dn_norm=dn_norm,
             ret_decay=ret_decay, ret_gn=ret_gn, sc_conv=sc_conv, w_branch=w_branch, w_out=w_out,
             router=router, router_bias=router_bias, w_eg=w_eg, w_eu=w_eu, w_ed=w_ed,
             w_sg=w_sg, w_su=w_su, w_sd=w_sd, final_norm=final_norm)
    return _forward(x_prompt, x_sample, c, cache_ckv, cache_kpe, state_dn, state_ret, c_ctx, P, _TILES)
```

```python
import functools
import math

import jax
import jax.numpy as jnp
import numpy as np
from jax import lax
from jax.experimental import pallas as pl
from jax.experimental.pallas import tpu as pltpu

F32 = jnp.float32
BF16 = jnp.bfloat16
I32 = jnp.int32
U32 = jnp.uint32

D = 1024
EPS = 1e-6
CHUNK = 64
PAIR = 2 * CHUNK
GRID_W = 64
ROPE_BASE = 10000.0

MLA_H, MLA_NOPE, MLA_ROPE, MLA_V, MLA_QR, MLA_KVR = 8, 64, 32, 64, 256, 128
DN_H, DN_DK = 4, 128
RET_H, RET_DK = 4, 128
MIX_W = 512
N_EXP, TOP_K, N_GRP, TOPK_GRP, D_EXP = 64, 8, 8, 4, 256
ROUTED_SCALE = 2.5

LANE = 128
COL_CQ = 0
COL_KV = 256
N_PROJ_A = 512
AB_LANE = 160
COL_GATE = 0
COL_DNQKV = 4096
COL_DNZ = 5632
COL_RET = 6144
COL_SC = 8192
N_PROJ_B = 9728

MOE_BM = 512
VMEM_LIMIT = 56 * 1024 * 1024


def _cparams(sem, vmem=None):
    return pltpu.CompilerParams(dimension_semantics=sem, vmem_limit_bytes=vmem or VMEM_LIMIT)


def _dot(a, b):
    return jnp.dot(a, b, preferred_element_type=F32)


def _dot_nt(a, b):
    return lax.dot_general(a, b, (((1,), (1,)), ((), ())), preferred_element_type=F32)


def _dot_tn(a, b):
    return lax.dot_general(a, b, (((0,), (0,)), ((), ())), preferred_element_type=F32)


def _split(a):
    hi = a.astype(BF16)
    lo = (a - hi.astype(F32)).astype(BF16)
    return hi, lo


def _dot3(a, b):
    ah, al = _split(a)
    bh, bl = _split(b)
    return _dot(ah, bh) + (_dot(ah, bl) + _dot(al, bh))


def _dot3s(a, b):
    (ah, al), (bh, bl) = a, b
    return _dot(jnp.concatenate([ah, ah, al], axis=1), jnp.concatenate([bh, bl, bh], axis=0))


def _dot3_nt(a, b):
    ah, al = _split(a)
    bh, bl = _split(b)
    return _dot_nt(ah, bh) + (_dot_nt(ah, bl) + _dot_nt(al, bh))


def _sigmoid(x):
    return 1.0 / (1.0 + jnp.exp(-x))


def _silu(x):
    return x * _sigmoid(x)


def _rms(x, g):
    return x * lax.rsqrt(jnp.mean(x * x, axis=-1, keepdims=True) + EPS) * g


def _ada_kernel(c_ref, w_ref, b_ref, o_ref):
    o_ref[...] = _dot3(_silu(c_ref[...]), w_ref[...]) + b_ref[...]


def _ada(cvec, w, b):
    n = w.shape[1]
    tn = 1024
    return pl.pallas_call(
        _ada_kernel,
        out_shape=jax.ShapeDtypeStruct((cvec.shape[0], n), F32),
        grid=(n // tn,),
        in_specs=[pl.BlockSpec(cvec.shape, lambda j: (0, 0)),
                  pl.BlockSpec((D, tn), lambda j: (0, j)),
                  pl.BlockSpec((1, tn), lambda j: (0, j))],
        out_specs=pl.BlockSpec((cvec.shape[0], tn), lambda j: (0, j)),
        compiler_params=_cparams(("arbitrary",)),
    )(cvec, w, b.reshape(1, n))


def _in_proj_kernel(x_ref, mod_ref, g_ref, w_ref, o_ref, h_scr):
    @pl.when(pl.program_id(1) == 0)
    def _():
        m = mod_ref[0]
        y = _rms(x_ref[...], g_ref[...])
        h_scr[...] = (y * (1.0 + m[:, D:2 * D]) + m[:, 0:D]).astype(BF16)

    o_ref[...] = _dot(h_scr[...], w_ref[...]).astype(o_ref.dtype)


def _in_proj(x, mod3, g, w, grp_of_tile, tm, tn, out_dtype):
    t = x.shape[0]
    n = w.shape[1]
    return pl.pallas_call(
        _in_proj_kernel,
        out_shape=jax.ShapeDtypeStruct((t, n), out_dtype),
        grid=(t // tm, n // tn),
        in_specs=[pl.BlockSpec((tm, D), lambda i, j: (i, 0)),
                  pl.BlockSpec((1, 1, 6 * D), lambda i, j: (grp_of_tile(tm)(i), 0, 0)),
                  pl.BlockSpec((1, D), lambda i, j: (0, 0)),
                  pl.BlockSpec((D, tn), lambda i, j: (0, j))],
        out_specs=pl.BlockSpec((tm, tn), lambda i, j: (i, j)),
        scratch_shapes=[pltpu.VMEM((tm, D), BF16)],
        compiler_params=_cparams(("arbitrary", "arbitrary")),
    )(x, mod3, g.reshape(1, D), w)


def _rope3(x, c, s1, s2, width):
    return x * c + pltpu.roll(x, width - 16, 1) * s1 + pltpu.roll(x, 16, 1) * s2


def _mla_q_kernel(*refs, rope):
    if rope:
        p_ref, g_ref, w_ref, c_ref, s1_ref, s2_ref, o_ref = refs
    else:
        p_ref, g_ref, w_ref, o_ref = refs
    y = _rms(p_ref[...], g_ref[...])
    q = _dot(y.astype(BF16), w_ref[...])
    if rope:
        tile = lambda r: jnp.concatenate([r[...]] * MLA_H, axis=1)
        q = _rope3(q, tile(c_ref), tile(s1_ref), tile(s2_ref), MLA_H * LANE)
    o_ref[...] = q.astype(BF16)


def _mla_q(proj, g, w, row_off, rows, tm, rope_tabs, seq):
    nrow = rows // tm
    r0 = row_off // tm
    in_specs = [pl.BlockSpec((tm, MLA_QR), lambda i: (r0 + i, COL_CQ // MLA_QR)),
                pl.BlockSpec((1, MLA_QR), lambda i: (0, 0)),
                pl.BlockSpec((MLA_QR, MLA_H * LANE), lambda i: (0, 0))]
    args = [proj, g.reshape(1, MLA_QR), w]
    if rope_tabs is not None:
        per = seq // tm
        in_specs += [pl.BlockSpec((tm, LANE), lambda i: (i % per, 0))] * 3
        args += list(rope_tabs)
    return pl.pallas_call(
        functools.partial(_mla_q_kernel, rope=rope_tabs is not None),
        out_shape=jax.ShapeDtypeStruct((rows, MLA_H * LANE), BF16),
        grid=(nrow,),
        in_specs=in_specs,
        out_specs=pl.BlockSpec((tm, MLA_H * LANE), lambda i: (i, 0)),
        compiler_params=_cparams(("arbitrary",)),
    )(*args)


def _mla_kv_kernel(*refs, norm, rope):
    if rope:
        p_ref, g_ref, wk_ref, wv_ref, c_ref, s1_ref, s2_ref, ckv_ref, kpe_ref, k_ref, v_ref = refs
    else:
        p_ref, g_ref, wk_ref, wv_ref, ckv_ref, kpe_ref, k_ref, v_ref = refs
    blk = p_ref[...]
    ckv = blk[:, :MLA_KVR]
    if norm:
        ckv = _rms(ckv, g_ref[...])
    kp = blk[:, MLA_KVR:]
    ckv_ref[...] = ckv
    kpe_ref[...] = kp[:, :MLA_ROPE]
    if rope:
        kp = _rope3(kp, c_ref[...], s1_ref[...], s2_ref[...], LANE)
    a = jnp.concatenate([ckv, kp], axis=1).astype(BF16)
    k_ref[...] = _dot(a, wk_ref[...]).astype(BF16)
    v_ref[...] = _dot(ckv.astype(BF16), wv_ref[...]).astype(BF16)


def _mla_kv(src, col_blk, g, wk, wv, row_off, rows, tm, norm, rope_tabs, seq):
    nrow = rows // tm
    r0 = row_off // tm
    in_specs = [pl.BlockSpec((tm, 2 * LANE), lambda i: (r0 + i, col_blk)),
                pl.BlockSpec((1, MLA_KVR), lambda i: (0, 0)),
                pl.BlockSpec((2 * LANE, MLA_H * LANE), lambda i: (0, 0)),
                pl.BlockSpec((MLA_KVR, MLA_H * MLA_V), lambda i: (0, 0))]
    args = [src, g.reshape(1, MLA_KVR), wk, wv]
    if rope_tabs is not None:
        per = seq // tm
        in_specs += [pl.BlockSpec((tm, LANE), lambda i: (i % per, 0))] * 3
        args += list(rope_tabs)
    return pl.pallas_call(
        functools.partial(_mla_kv_kernel, norm=norm, rope=rope_tabs is not None),
        out_shape=(jax.ShapeDtypeStruct((rows, MLA_KVR), F32),
                   jax.ShapeDtypeStruct((rows, MLA_ROPE), F32),
                   jax.ShapeDtypeStruct((rows, MLA_H * LANE), BF16),
                   jax.ShapeDtypeStruct((rows, MLA_H * MLA_V), BF16)),
        grid=(nrow,),
        in_specs=in_specs,
        out_specs=(pl.BlockSpec((tm, MLA_KVR), lambda i: (i, 0)),
                   pl.BlockSpec((tm, MLA_ROPE), lambda i: (i, 0)),
                   pl.BlockSpec((tm, MLA_H * LANE), lambda i: (i, 0)),
                   pl.BlockSpec((tm, MLA_H * MLA_V), lambda i: (i, 0))),
        compiler_params=_cparams(("arbitrary",)),
    )(*args)


def _attn_kernel(*refs, nseg):
    q_ref = refs[0]
    k_refs = refs[1:1 + nseg]
    v_refs = refs[1 + nseg:1 + 2 * nseg]
    o_ref = refs[1 + 2 * nseg]
    scale = (MLA_NOPE + MLA_ROPE) ** -0.5
    outs = []
    for h in range(MLA_H):
        qh = q_ref[0, :, h * LANE:(h + 1) * LANE]
        ss = [_dot_nt(qh, k_ref[0, :, h * LANE:(h + 1) * LANE]) * scale for k_ref in k_refs]
        m = functools.reduce(jnp.maximum, [jnp.max(s, axis=-1, keepdims=True) for s in ss])
        es = [jnp.exp(s - m) for s in ss]
        inv = 1.0 / functools.reduce(jnp.add, [jnp.sum(e, axis=-1, keepdims=True) for e in es])
        o = functools.reduce(jnp.add, [
            _dot((e * inv).astype(BF16), v_ref[0, :, h * MLA_V:(h + 1) * MLA_V])
            for e, v_ref in zip(es, v_refs)])
        outs.append(o)
    o_ref[...] = jnp.concatenate(outs, axis=1)


def _into(kernel, in_specs, args, prev):
    if prev is None:
        return kernel, {}
    pos = len(args)
    in_specs.append(pl.BlockSpec(memory_space=pl.ANY))
    args.append(prev)
    return (lambda *refs: kernel(*refs[:pos], *refs[pos + 1:])), {pos: 0}


def _attn(q, ks, vs, tq, t_total, row_off, prev):
    b, s, _ = q.shape
    nseg = len(ks)
    in_specs = [pl.BlockSpec((1, tq, MLA_H * LANE), lambda i, j: (i, j, 0))]
    in_specs += [pl.BlockSpec((1,) + k.shape[1:], lambda i, j: (i, 0, 0)) for k in ks]
    in_specs += [pl.BlockSpec((1,) + v.shape[1:], lambda i, j: (i, 0, 0)) for v in vs]
    args = [q, *ks, *vs]
    kern, alias = _into(functools.partial(_attn_kernel, nseg=nseg), in_specs, args, prev)
    r0, per = row_off // tq, s // tq
    return pl.pallas_call(
        kern,
        out_shape=jax.ShapeDtypeStruct((t_total, MLA_H * MLA_V), F32),
        grid=(b, per),
        in_specs=in_specs,
        out_specs=pl.BlockSpec((tq, MLA_H * MLA_V), lambda i, j: (r0 + i * per + j, 0)),
        input_output_aliases=alias,
        compiler_params=_cparams(("arbitrary", "arbitrary")),
    )(*args)


def _conv3(x, w):
    s = x.shape[0]
    row = lax.broadcasted_iota(I32, x.shape, 0)
    prev = jnp.where(row == 0, 0.0, pltpu.roll(x, 1, 0))
    nxt = jnp.where(row == s - 1, 0.0, pltpu.roll(x, s - 1, 0))
    return prev * w[0:1] + x * w[1:2] + nxt * w[2:3]


def _sconv_kernel(b_ref, c_ref, x_ref, w_ref, o_ref):
    f = lambda r: r[...].astype(F32)
    o_ref[...] = f(b_ref) * _conv3(f(c_ref) * f(x_ref), w_ref[...])


def _sconv(proj, w, row_off, nseq, seq, t_total, prev):
    r0 = row_off // seq
    cb = COL_SC // MIX_W
    spec = lambda off: pl.BlockSpec((seq, MIX_W), lambda b: (r0 + b, cb + off))
    in_specs = [spec(0), spec(1), spec(2), pl.BlockSpec((3, MIX_W), lambda b: (0, 0))]
    args = [proj, proj, proj, w]
    kern, alias = _into(_sconv_kernel, in_specs, args, prev)
    return pl.pallas_call(
        kern,
        out_shape=jax.ShapeDtypeStruct((t_total, MIX_W), F32),
        grid=(nseq,),
        in_specs=in_specs,
        out_specs=pl.BlockSpec((seq, MIX_W), lambda b: (r0 + b, 0)),
        input_output_aliases=alias,
        compiler_params=_cparams(("arbitrary",)),
    )(*args)


def _ret_kernel(*refs, seq, rope, has_s0):
    it = iter(refs)
    q_ref, k_ref, v_ref, g_ref, dec_ref, gn_ref = (next(it) for _ in range(6))
    if rope:
        c_ref, s_ref = next(it), next(it)
    if has_s0:
        s0_ref = next(it)
    y_ref, sfin_ref, of_scr, ob_scr = (next(it) for _ in range(4))
    nch = seq // CHUNK

    q = q_ref[...].astype(F32)
    k = k_ref[...].astype(F32) * (RET_DK ** -0.5)
    if rope:
        c, sn = c_ref[...], s_ref[...]
        q = q * c + pltpu.roll(q, RET_DK // 2, 1) * sn
        k = k * c + pltpu.roll(k, RET_DK // 2, 1) * sn
    v = v_ref[...].astype(F32)

    ti = lax.broadcasted_iota(I32, (CHUNK, CHUNK), 0)
    si = lax.broadcasted_iota(I32, (CHUNK, CHUNK), 1)
    pos = lax.broadcasted_iota(I32, (CHUNK, 1), 0).astype(F32)
    for d in range(2):
        lg = -jnp.exp(dec_ref[d, 0, 0:1, 0:1])
        if d == 0:
            dist = (ti - si).astype(F32)
            qpow, kpow = pos + 1.0, (CHUNK - 1.0) - pos
        else:
            dist = (si - ti).astype(F32)
            qpow, kpow = CHUNK - pos, pos
        decay = jnp.where(dist >= 0, jnp.exp(jnp.maximum(dist, 0.0) * lg), 0.0)
        qs = jnp.exp(qpow * lg)
        ks = jnp.exp(kpow * lg)
        gtot = jnp.exp(CHUNK * lg)
        st = s0_ref[0, 0, d, 0] if has_s0 else jnp.zeros((RET_DK, RET_DK), F32)
        o_scr = of_scr if d == 0 else ob_scr
        order = range(nch) if d == 0 else range(nch - 1, -1, -1)
        for n in order:
            sl = slice(n * CHUNK, (n + 1) * CHUNK)
            qn, kn, vn = q[sl], k[sl], v[sl]
            vb = vn.astype(BF16)
            a = _dot_nt(qn.astype(BF16), kn.astype(BF16)) * decay
            o = _dot(a.astype(BF16), vb) + _dot((qn * qs).astype(BF16), st.astype(BF16))
            st = gtot * st + _dot_tn((kn * ks).astype(BF16), vb)
            o_scr[sl, :] = o
        sfin_ref[0, d, 0] = st

    o = of_scr[...] + ob_scr[...]
    dlt = o - jnp.mean(o, axis=-1, keepdims=True)
    y = dlt * lax.rsqrt(jnp.mean(dlt * dlt, axis=-1, keepdims=True) + EPS) * gn_ref[...]
    y_ref[...] = _silu(g_ref[...].astype(F32)) * y


def _retention(proj, dec_b, gn, row_off, nseq, seq, rope_tabs, s0, layer, t_total, prev):
    r0 = row_off // seq
    cb = COL_RET // LANE
    spec = lambda off: pl.BlockSpec((seq, LANE), lambda b, h: (r0 + b, cb + off + h))
    in_specs = [spec(0), spec(RET_H), spec(2 * RET_H), spec(3 * RET_H),
                pl.BlockSpec((2, 1, 8, LANE), lambda b, h: (0, h, 0, 0)),
                pl.BlockSpec((1, LANE), lambda b, h: (0, h))]
    args = [proj, proj, proj, proj, dec_b, gn.reshape(1, RET_H * RET_DK)]
    if rope_tabs is not None:
        in_specs += [pl.BlockSpec((seq, LANE), lambda b, h: (0, 0))] * 2
        args += list(rope_tabs)
    if s0 is not None:
        in_specs.append(pl.BlockSpec((1, 1, 2, 1, RET_DK, RET_DK), lambda b, h: (b, layer, 0, h, 0, 0)))
        args.append(s0)
    kern, alias = _into(functools.partial(_ret_kernel, seq=seq, rope=rope_tabs is not None, has_s0=s0 is not None),
                        in_specs, args, prev)
    return pl.pallas_call(
        kern,
        out_shape=(jax.ShapeDtypeStruct((t_total, RET_H * RET_DK), F32),
                   jax.ShapeDtypeStruct((nseq, 2, RET_H, RET_DK, RET_DK), F32)),
        grid=(nseq, RET_H),
        in_specs=in_specs,
        out_specs=(pl.BlockSpec((seq, LANE), lambda b, h: (r0 + b, h)),
                   pl.BlockSpec((1, 2, 1, RET_DK, RET_DK), lambda b, h: (b, 0, h, 0, 0))),
        scratch_shapes=[pltpu.VMEM((seq, RET_DK), F32), pltpu.VMEM((seq, RET_DK), F32)],
        input_output_aliases=alias,
        compiler_params=_cparams(("arbitrary", "arbitrary")),
    )(*args)


def _softplus(x):
    return jnp.maximum(x, 0.0) + jnp.log1p(jnp.exp(-jnp.abs(x)))


def _l2n(x):
    return x * lax.rsqrt(jnp.sum(x * x, axis=-1, keepdims=True) + EPS)


def _dn_kernel(*refs, seq, has_s0):
    it = iter(refs)
    q_ref, k_ref, v_ref, z_ref, ab_ref, wq_ref, wk_ref, wv_ref, alog_ref, dtb_ref, nrm_ref = (
        next(it) for _ in range(11))
    if has_s0:
        s0_ref = next(it)
    y_ref, sfin_ref = next(it), next(it)
    of_scr, ob_scr, q_scr, k_scr, v_scr, qe_scr, oc_scr, m_scr, c_scr, gt_scr = (next(it) for _ in range(10))
    nch = seq // CHUNK
    npair = seq // PAIR
    hcols = lambda hh: slice(hh * DN_DK, (hh + 1) * DN_DK)

    xq = _silu(_conv3(q_ref[...].astype(F32), wq_ref[...]))
    xk = _silu(_conv3(k_ref[...].astype(F32), wk_ref[...]))
    v_scr[...] = _silu(_conv3(v_ref[...].astype(F32), wv_ref[...]))
    for hh in range(DN_H):
        q_scr[:, hcols(hh)] = _l2n(xq[:, hcols(hh)]) * (DN_DK ** -0.5)
        k_scr[:, hcols(hh)] = _l2n(xk[:, hcols(hh)])

    ti = lax.broadcasted_iota(I32, (PAIR, PAIR), 0)
    si = lax.broadcasted_iota(I32, (PAIR, PAIR), 1)
    same = (ti >> 6) == (si >> 6)
    s_in = si & (CHUNK - 1)
    eye = ti == si
    eye_f = eye.astype(F32)
    lane = lax.broadcasted_iota(I32, (1, 2 * LANE), 1)

    def ab_col(blk, idx):
        return jnp.sum(jnp.where(lane == AB_LANE + idx, blk, 0.0), axis=1, keepdims=True)

    masks = []
    for d in range(2):
        masks.append((same & ((si <= ti) if d == 0 else (si >= ti)),
                      same & ((si < ti) if d == 0 else (si > ti)),
                      same & ((ti <= si) if d == 0 else (ti >= si)),
                      same & (s_in == (CHUNK - 1 if d == 0 else 0))))

    def prep(p, carry):
        rows = pl.ds(pl.multiple_of(p * PAIR, PAIR), PAIR)
        blk = ab_ref[rows, :]
        chains = []
        for hh in range(DN_H):
            cs = hcols(hh)
            qn, kn, vn = q_scr[rows, cs], k_scr[rows, cs], v_scr[rows, cs]
            kb = kn.astype(BF16)
            kk = _dot_nt(kb, kb)
            qkr = _dot_nt(qn.astype(BF16), kb)
            for d in range(2):
                incl, strict, incl_t, last_s = masks[d]
                neg_a = -jnp.exp(alog_ref[d, hh, 0:1, 0:1])
                dtb = dtb_ref[d, hh, 0:1, 0:1]
                la = neg_a * _softplus(ab_col(blk, d * DN_H + hh) + dtb)
                beta = _sigmoid(ab_col(blk, 2 * DN_H + d * DN_H + hh))
                g_row = jnp.sum(jnp.where(incl_t, la, 0.0), axis=0, keepdims=True)
                g_col = jnp.sum(jnp.where(eye, g_row, 0.0), axis=1, keepdims=True)
                g_end = jnp.sum(jnp.where(last_s, g_row, 0.0), axis=1, keepdims=True)
                decay = jnp.where(incl, jnp.exp(jnp.where(incl, g_col - g_row, 0.0)), 0.0)
                nmat = jnp.where(strict, -(beta * decay * kk), 0.0)
                eg = jnp.exp(g_col)
                chains.append(dict(
                    hh=hh, d=d, cs=cs, g_row=g_row, tinv=eye_f + nmat, pw=_split(nmat),
                    bv=_split(beta * vn), bk=_split((beta * eg) * kn), egq=eg * qn,
                    qk=(qkr * decay).astype(BF16), kd=(jnp.exp(g_end - g_col) * kn).astype(BF16)))
        for _ in range(5):
            sq = [_dot(c['pw'][0], c['pw'][0]) for c in chains]
            for c, s in zip(chains, sq):
                c['pw'] = _split(s)
            up = [_dot(c['tinv'].astype(BF16), c['pw'][0]) for c in chains]
            for c, u in zip(chains, up):
                c['tinv'] = c['tinv'] + u
        for c in chains:
            c['ts'] = _split(c['tinv'])
        for c in chains:
            c['uvb'] = _dot3s(c['ts'], c['bv']).astype(BF16)
        for c in chains:
            c['wkb'] = _dot3s(c['ts'], c['bk']).astype(BF16)
        for c in chains:
            qe_scr[c['d'], rows, c['cs']] = (c['egq'] - _dot(c['qk'], c['wkb'])).astype(BF16)
            oc_scr[c['d'], rows, c['cs']] = _dot(c['qk'], c['uvb'])
        for c in chains:
            d, j = c['d'], c['d'] * DN_H + c['hh']
            for half in range(2):
                sl = slice(half * CHUNK, (half + 1) * CHUNK)
                n = p * 2 + half
                mrows = pl.ds(pl.multiple_of(n * DN_DK, DN_DK), DN_DK)
                m_scr[j, mrows, :] = _dot_tn(c['kd'][sl], c['wkb'][sl]).astype(BF16)
                c_scr[j, mrows, :] = _dot_tn(c['kd'][sl], c['uvb'][sl])
                e = half * CHUNK + (CHUNK - 1 if d == 0 else 0)
                gt_scr[j, pl.ds(pl.multiple_of(n * 8, 8), 8), :] = jnp.broadcast_to(
                    jnp.exp(c['g_row'][:, e:e + 1]), (8, LANE))
        return carry

    lax.fori_loop(0, npair, prep, 0)

    for d in range(2):
        for hh in range(DN_H):
            sfin_ref[0, d, hh] = s0_ref[0, 0, d, hh] if has_s0 else jnp.zeros((DN_DK, DN_DK), F32)

    def step(i, carry):
        for d in range(2):
            n = i if d == 0 else nch - 1 - i
            rows = pl.ds(pl.multiple_of(n * CHUNK, CHUNK), CHUNK)
            mrows = pl.ds(pl.multiple_of(n * DN_DK, DN_DK), DN_DK)
            o_scr = of_scr if d == 0 else ob_scr
            for hh in range(DN_H):
                st = sfin_ref[0, d, hh]
                sb = st.astype(BF16)
                o_scr[rows, hcols(hh)] = _dot(qe_scr[d, rows, hcols(hh)], sb) + oc_scr[d, rows, hcols(hh)]
                gt = gt_scr[d * DN_H + hh, pl.ds(pl.multiple_of(n * 8, 8), 8), :][0:1, 0:1]
                sfin_ref[0, d, hh] = (gt * st - _dot(m_scr[d * DN_H + hh, mrows, :], sb)
                                      + c_scr[d * DN_H + hh, mrows, :])
        return carry

    lax.fori_loop(0, nch, step, 0)

    for hh in range(DN_H):
        o = of_scr[:, hcols(hh)] + ob_scr[:, hcols(hh)]
        y = o * lax.rsqrt(jnp.mean(o * o, axis=-1, keepdims=True) + EPS) * nrm_ref[...]
        y_ref[:, hcols(hh)] = y * _silu(z_ref[:, hcols(hh)].astype(F32))


def _deltanet(proj_a, proj, conv_w, alog_b, dtb_b, nrm, row_off, nseq, seq, s0, layer, t_total, prev):
    r0 = row_off // seq
    w = DN_H * DN_DK
    spec = lambda col: pl.BlockSpec((seq, w), lambda b: (r0 + b, col // w))
    wspec = lambda j: pl.BlockSpec((3, w), lambda b: (0, j))
    bspec = pl.BlockSpec((2, DN_H, 8, LANE), lambda b: (0, 0, 0, 0))
    in_specs = [spec(COL_DNQKV), spec(COL_DNQKV + w), spec(COL_DNQKV + 2 * w), spec(COL_DNZ),
                pl.BlockSpec((seq, 2 * LANE), lambda b: (r0 + b, COL_KV // (2 * LANE))),
                wspec(0), wspec(1), wspec(2), bspec, bspec,
                pl.BlockSpec((1, DN_DK), lambda b: (0, 0))]
    args = [proj, proj, proj, proj, proj_a, conv_w, conv_w, conv_w, alog_b, dtb_b, nrm.reshape(1, DN_DK)]
    if s0 is not None:
        in_specs.append(pl.BlockSpec((1, 1, 2, DN_H, DN_DK, DN_DK), lambda b: (b, layer, 0, 0, 0, 0)))
        args.append(s0)
    nch = seq // CHUNK
    scratch = [pltpu.VMEM((seq, w), F32), pltpu.VMEM((seq, w), F32),
               pltpu.VMEM((seq, w), F32), pltpu.VMEM((seq, w), F32), pltpu.VMEM((seq, w), F32),
               pltpu.VMEM((2, seq, w), BF16), pltpu.VMEM((2, seq, w), F32),
               pltpu.VMEM((2 * DN_H, nch * DN_DK, DN_DK), BF16), pltpu.VMEM((2 * DN_H, nch * DN_DK, DN_DK), F32),
               pltpu.VMEM((2 * DN_H, nch * 8, LANE), F32)]
    kern, alias = _into(functools.partial(_dn_kernel, seq=seq, has_s0=s0 is not None), in_specs, args, prev)
    return pl.pallas_call(
        kern,
        out_shape=(jax.ShapeDtypeStruct((t_total, w), F32),
                   jax.ShapeDtypeStruct((nseq, 2, DN_H, DN_DK, DN_DK), F32)),
        grid=(nseq,),
        in_specs=in_specs,
        out_specs=(pl.BlockSpec((seq, w), lambda b: (r0 + b, 0)),
                   pl.BlockSpec((1, 2, DN_H, DN_DK, DN_DK), lambda b: (b, 0, 0, 0, 0))),
        scratch_shapes=scratch,
        input_output_aliases=alias,
        compiler_params=_cparams(("arbitrary",)),
    )(*args)


def _pack_bf16_pairs(x):
    n = x.shape[1] // 2
    bits = lax.bitcast_convert_type(x.astype(BF16).astype(F32), U32)
    return (bits[:, n:] & jnp.uint32(0xFFFF0000)) | (bits[:, :n] >> 16)


def _unpack_bf16_pairs(u):
    lo = lax.bitcast_convert_type(u << 16, F32)
    hi = lax.bitcast_convert_type(u & jnp.uint32(0xFFFF0000), F32)
    return lo, hi


def _branch_out_kernel(b0, b1, b2, b3, g0, g1, g2, g3, x_ref, mod_ref, nf_ref, wb_ref, wo_ref,
                       xo_ref, h_ref, hp_ref):
    acc = None
    for n, (br, gl) in enumerate(((b0, g0), (b1, g1), (b2, g2), (b3, g3))):
        p = _sigmoid(gl[...].astype(F32)) * _dot(br[...].astype(BF16), wb_ref[n])
        acc = p if acc is None else acc + p
    y = _dot(acc.astype(BF16), wo_ref[...])
    m = mod_ref[0]
    x = x_ref[...] + m[:, 2 * D:3 * D] * y
    xo_ref[...] = x
    h = _rms(x, nf_ref[...]) * (1.0 + m[:, 4 * D:5 * D]) + m[:, 3 * D:4 * D]
    h_ref[...] = h
    hp_ref[...] = _pack_bf16_pairs(h)


def _branch_out(branches, proj, x, mod3, nf, wb, wo, grp_of_tile, tm):
    t = x.shape[0]
    gb = COL_GATE // D
    bspec = pl.BlockSpec((tm, MIX_W), lambda i: (i, 0))
    gspec = lambda n: pl.BlockSpec((tm, D), lambda i: (i, gb + n))
    row = pl.BlockSpec((tm, D), lambda i: (i, 0))
    return pl.pallas_call(
        _branch_out_kernel,
        out_shape=(jax.ShapeDtypeStruct((t, D), F32), jax.ShapeDtypeStruct((t, D), F32),
                   jax.ShapeDtypeStruct((t, D // 2), U32)),
        grid=(t // tm,),
        in_specs=[bspec] * 4 + [gspec(n) for n in range(4)] + [
            row,
            pl.BlockSpec((1, 1, 6 * D), lambda i: (grp_of_tile(tm)(i), 0, 0)),
            pl.BlockSpec((1, D), lambda i: (0, 0)),
            pl.BlockSpec((4, MIX_W, D), lambda i: (0, 0, 0)),
            pl.BlockSpec((D, D), lambda i: (0, 0))],
        out_specs=(row, row, pl.BlockSpec((tm, D // 2), lambda i: (i, 0))),
        compiler_params=_cparams(("arbitrary",)),
    )(*branches, proj, proj, proj, proj, x, mod3, nf.reshape(1, D), wb, wo)


def _route_kernel(h_ref, wr_ref, bias_ref, idx_ref, w_ref, rank_ref, cnt_ref, run_scr, *, tm):
    @pl.when(pl.program_id(0) == 0)
    def _():
        run_scr[...] = jnp.zeros_like(run_scr)

    neg = -jnp.inf
    gsz = N_EXP // N_GRP
    scores = _sigmoid(_dot3_nt(wr_ref[...], h_ref[...]))
    choice = scores + bias_ref[:, 0:1]
    row8 = lax.broadcasted_iota(I32, (gsz, tm), 0)
    gscore = []
    for g in range(N_GRP):
        blk = choice[g * gsz:(g + 1) * gsz]
        m1 = jnp.max(blk, axis=0, keepdims=True)
        i1 = jnp.min(jnp.where(blk == m1, row8, gsz), axis=0, keepdims=True)
        m2 = jnp.max(jnp.where(row8 == i1, neg, blk), axis=0, keepdims=True)
        gscore.append(m1 + m2)
    masked = []
    for g in range(N_GRP):
        rank = jnp.zeros((1, tm), I32)
        for g2 in range(N_GRP):
            if g2 == g:
                continue
            ahead = (gscore[g2] >= gscore[g]) if g2 < g else (gscore[g2] > gscore[g])
            rank = rank + ahead.astype(I32)
        masked.append(jnp.where(rank < TOPK_GRP, choice[g * gsz:(g + 1) * gsz], neg))
    cur = jnp.concatenate(masked, axis=0)
    row = lax.broadcasted_iota(I32, (N_EXP, tm), 0)
    sel = jnp.zeros((N_EXP, tm), jnp.bool_)
    idxs, scs = [], []
    for _ in range(TOP_K):
        m = jnp.max(cur, axis=0, keepdims=True)
        ik = jnp.min(jnp.where(cur == m, row, N_EXP), axis=0, keepdims=True)
        hit = row == ik
        scs.append(jnp.sum(jnp.where(hit, scores, 0.0), axis=0, keepdims=True))
        idxs.append(ik)
        cur = jnp.where(hit, neg, cur)
        sel = sel | hit
    tot = functools.reduce(jnp.add, scs)
    self_f = jnp.where(sel, 1.0, 0.0)
    tri = (lax.broadcasted_iota(I32, (tm, tm), 0) <= lax.broadcasted_iota(I32, (tm, tm), 1))
    csum = _dot(self_f.astype(BF16), jnp.where(tri, 1.0, 0.0).astype(BF16))
    run = run_scr[:, 0:1]
    rank_all = run + csum - self_f
    ranks = [jnp.sum(jnp.where(row == ik, rank_all, 0.0), axis=0, keepdims=True) for ik in idxs]
    idx_ref[...] = jnp.concatenate(idxs, axis=0)
    w_ref[...] = jnp.concatenate([s / tot * ROUTED_SCALE for s in scs], axis=0)
    rank_ref[...] = jnp.concatenate(ranks, axis=0).astype(I32)
    new_run = run + csum[:, tm - 1:tm]
    run_scr[...] = jnp.broadcast_to(new_run, run_scr.shape)
    cnt_ref[...] = jnp.broadcast_to(new_run, cnt_ref.shape).astype(I32)


def _route(h, wr_t, bias_b, tm):
    t = h.shape[0]
    tok = pl.BlockSpec((TOP_K, tm), lambda i: (0, i))
    return pl.pallas_call(
        functools.partial(_route_kernel, tm=tm),
        out_shape=(jax.ShapeDtypeStruct((TOP_K, t), I32), jax.ShapeDtypeStruct((TOP_K, t), F32),
                   jax.ShapeDtypeStruct((TOP_K, t), I32), jax.ShapeDtypeStruct((N_EXP, LANE), I32)),
        grid=(t // tm,),
        in_specs=[pl.BlockSpec((tm, D), lambda i: (i, 0)),
                  pl.BlockSpec((N_EXP, D), lambda i: (0, 0)),
                  pl.BlockSpec((N_EXP, LANE), lambda i: (0, 0))],
        out_specs=(tok, tok, tok, pl.BlockSpec((N_EXP, LANE), lambda i: (0, 0))),
        scratch_shapes=[pltpu.VMEM((N_EXP, LANE), F32)],
        compiler_params=_cparams(("arbitrary",)),
    )(h, wr_t, bias_b)


def _slot_pos_kernel(ps_ref, idx_ref, rank_ref, pos_ref):
    idx = idx_ref[...]
    pos = rank_ref[...]
    for e in range(N_EXP):
        pos = pos + jnp.where(idx == e, ps_ref[e], 0)
    pos_ref[...] = pos


def _slot_pos(pstart, idx_t, rank_t, tm):
    t = idx_t.shape[1]
    tok = pl.BlockSpec((TOP_K, tm), lambda i, *_: (0, i))
    return pl.pallas_call(
        _slot_pos_kernel,
        out_shape=jax.ShapeDtypeStruct((TOP_K, t), I32),
        grid_spec=pltpu.PrefetchScalarGridSpec(
            num_scalar_prefetch=1, grid=(t // tm,), in_specs=[tok, tok], out_specs=tok),
        compiler_params=_cparams(("arbitrary",)),
    )(pstart, idx_t, rank_t)


def _dispatch_kernel(pstart_ref, padded_ref, pos_ref, h_ref, xs_ref, zero_scr, sem, zsem, *, tm):
    def zero_copy(e):
        start = pl.multiple_of(pstart_ref[e] + padded_ref[e] - MOE_BM, MOE_BM)
        return pltpu.make_async_copy(zero_scr, xs_ref.at[pl.ds(start, MOE_BM)], zsem)

    @pl.when(pl.program_id(0) == 0)
    def _():
        zero_scr[...] = jnp.zeros_like(zero_scr)

        def zstart(e, c):
            @pl.when(padded_ref[e] > 0)
            def _():
                zero_copy(e).start()
            return c

        def zwait(e, c):
            @pl.when(padded_ref[e] > 0)
            def _():
                zero_copy(e).wait()
            return c

        lax.fori_loop(0, N_EXP, zstart, 0)
        lax.fori_loop(0, N_EXP, zwait, 0)

    def row_copy(t, k):
        return pltpu.make_async_copy(h_ref.at[pl.ds(t, 1)], xs_ref.at[pl.ds(pos_ref[k, t], 1)], sem)

    def issue(t, c):
        for k in range(TOP_K):
            row_copy(t, k).start(priority=k % 2)
        return c

    def drain(t, c):
        for k in range(TOP_K):
            row_copy(t, k).wait()
        return c

    lax.fori_loop(0, tm, issue, 0)
    lax.fori_loop(0, tm, drain, 0)


def _dispatch(pstart, padded, pos_t, h, n_slots, tm):
    t = h.shape[0]
    smem_tok = pl.BlockSpec((TOP_K, tm), lambda i, *_: (0, i), memory_space=pltpu.SMEM)
    return pl.pallas_call(
        functools.partial(_dispatch_kernel, tm=tm),
        out_shape=jax.ShapeDtypeStruct((n_slots, h.shape[1]), h.dtype),
        grid_spec=pltpu.PrefetchScalarGridSpec(
            num_scalar_prefetch=2,
            grid=(t // tm,),
            in_specs=[smem_tok, pl.BlockSpec((tm, h.shape[1]), lambda i, *_: (i, 0))],
            out_specs=pl.BlockSpec(memory_space=pl.ANY),
            scratch_shapes=[pltpu.VMEM((MOE_BM, h.shape[1]), h.dtype), pltpu.SemaphoreType.DMA(()),
                            pltpu.SemaphoreType.DMA(())]),
        compiler_params=_cparams(("arbitrary",)),
    )(pstart, padded, pos_t, h)


def _experts_kernel(be_ref, nu_ref, x_ref, wg_ref, wu_ref, wd_ref, y_ref, wg_s, wu_s, wd_s):
    b = pl.program_id(0)

    @pl.when(jnp.logical_or(b == 0, be_ref[b] != be_ref[jnp.maximum(b - 1, 0)]))
    def _():
        wg_s[...] = wg_ref[0].astype(BF16)
        wu_s[...] = wu_ref[0].astype(BF16)
        wd_s[...] = wd_ref[0].astype(BF16)

    @pl.when(b < nu_ref[0])
    def _():
        x = jnp.concatenate(_unpack_bf16_pairs(x_ref[...]), axis=1).astype(BF16)
        a = _silu(_dot(x, wg_s[...])) * _dot(x, wu_s[...])
        y_ref[...] = _pack_bf16_pairs(_dot(a.astype(BF16), wd_s[...]))


def _experts(block_e, n_used, xs, wg, wu, wd):
    nb = xs.shape[0] // MOE_BM
    blk = lambda b, be, nu: (jnp.minimum(b, nu[0] - 1), 0)
    wsel = lambda b, be, nu: (be[b], 0, 0)
    return pl.pallas_call(
        _experts_kernel,
        out_shape=jax.ShapeDtypeStruct(xs.shape, U32),
        grid_spec=pltpu.PrefetchScalarGridSpec(
            num_scalar_prefetch=2,
            grid=(nb,),
            in_specs=[pl.BlockSpec((MOE_BM, D // 2), blk),
                      pl.BlockSpec((1, D, D_EXP), wsel),
                      pl.BlockSpec((1, D, D_EXP), wsel),
                      pl.BlockSpec((1, D_EXP, D), wsel)],
            out_specs=pl.BlockSpec((MOE_BM, D // 2), blk),
            scratch_shapes=[pltpu.VMEM((D, D_EXP), BF16), pltpu.VMEM((D, D_EXP), BF16),
                            pltpu.VMEM((D_EXP, D), BF16)]),
        compiler_params=_cparams(("arbitrary",)),
    )(block_e, n_used, xs, wg, wu, wd)


def _combine_kernel(*refs, tm, final):
    it = iter(refs)
    pos_ref, w_ref, h_ref, x_ref, mod_ref, wsg_ref, wsu_ref, wsd_ref = (next(it) for _ in range(8))
    if final:
        fn_ref = next(it)
    ys_ref = next(it)
    xo_ref = next(it)
    if final:
        yo_ref = next(it)
    buf, sem = next(it), next(it)

    def row_copy(t, k):
        return pltpu.make_async_copy(ys_ref.at[pl.ds(pos_ref[k, t], 1)], buf.at[k, pl.ds(t, 1)], sem)

    def issue(t, c):
        for k in range(TOP_K):
            row_copy(t, k).start(priority=k % 2)
        return c

    def drain(t, c):
        for k in range(TOP_K):
            row_copy(t, k).wait()
        return c

    lax.fori_loop(0, tm, issue, 0)
    hb = h_ref[...].astype(BF16)
    a = _silu(_dot(hb, wsg_ref[...])) * _dot(hb, wsu_ref[...])
    shared = _dot(a.astype(BF16), wsd_ref[...])
    lax.fori_loop(0, tm, drain, 0)
    w = w_ref[...]
    lo, hi = None, None
    for k in range(TOP_K):
        yl, yh = _unpack_bf16_pairs(buf[k])
        wk = w[:, k:k + 1]
        lo = yl * wk if lo is None else lo + yl * wk
        hi = yh * wk if hi is None else hi + yh * wk
    routed = jnp.concatenate([lo, hi], axis=1)
    m = mod_ref[0]
    x = x_ref[...] + m[:, 5 * D:6 * D] * (routed + shared)
    xo_ref[...] = x
    if final:
        yo_ref[...] = _rms(x, fn_ref[...])


def _combine(pos_t, w_tok, h, x, mod3, wsg, wsu, wsd, fn, ys, grp_of_tile, tm):
    t = h.shape[0]
    final = fn is not None
    row = pl.BlockSpec((tm, D), lambda i: (i, 0))
    full = lambda shp: pl.BlockSpec(shp, lambda i: (0,) * len(shp))
    in_specs = [pl.BlockSpec((TOP_K, tm), lambda i: (0, i), memory_space=pltpu.SMEM),
                pl.BlockSpec((tm, TOP_K), lambda i: (i, 0)), row, row,
                pl.BlockSpec((1, 1, 6 * D), lambda i: (grp_of_tile(tm)(i), 0, 0)),
                full((D, D_EXP)), full((D, D_EXP)), full((D_EXP, D))]
    args = [pos_t, w_tok, h, x, mod3, wsg, wsu, wsd]
    if final:
        in_specs.append(full((1, D)))
        args.append(fn.reshape(1, D))
    in_specs.append(pl.BlockSpec(memory_space=pl.ANY))
    args.append(ys)
    out_shape = [jax.ShapeDtypeStruct((t, D), F32)]
    out_specs = [row]
    if final:
        out_shape.append(jax.ShapeDtypeStruct((t, D), F32))
        out_specs.append(row)
    return pl.pallas_call(
        functools.partial(_combine_kernel, tm=tm, final=final),
        out_shape=tuple(out_shape),
        grid=(t // tm,),
        in_specs=in_specs,
        out_specs=tuple(out_specs),
        scratch_shapes=[pltpu.VMEM((TOP_K, tm, D // 2), U32), pltpu.SemaphoreType.DMA(())],
        compiler_params=_cparams(("arbitrary",)),
    )(*args)


def _pack_w_in(w):
    cuts = np.cumsum([0, 256, 160, 1536, 512, 16, 2048, 1536, 4096])
    seg = lambda i: w[:, cuts[i]:cuts[i + 1]]
    pad = jnp.zeros((D, 2 * LANE - 160 - 16), w.dtype)
    wa = jnp.concatenate([seg(0), seg(1), seg(4), pad], axis=1).astype(BF16)
    wb = jnp.concatenate([seg(7), seg(2), seg(3), seg(5), seg(6)], axis=1).astype(BF16)
    return wa, wb


def _pack_w_uq(w):
    w = w.reshape(MLA_QR, MLA_H, MLA_NOPE + MLA_ROPE)
    w = jnp.pad(w, ((0, 0), (0, 0), (0, LANE - MLA_NOPE - MLA_ROPE)))
    return w.reshape(MLA_QR, MLA_H * LANE).astype(BF16)


def _pack_w_ukv(w):
    w = w.reshape(MLA_KVR, MLA_H, MLA_NOPE + MLA_V)
    k_nope = jnp.pad(w[:, :, :MLA_NOPE], ((0, 0), (0, 0), (0, LANE - MLA_NOPE)))
    eye = jnp.eye(MLA_ROPE, dtype=w.dtype)[:, None, :]
    k_pe = jnp.pad(jnp.broadcast_to(eye, (MLA_ROPE, MLA_H, MLA_ROPE)),
                   ((0, LANE - MLA_ROPE), (0, 0), (MLA_NOPE, LANE - MLA_NOPE - MLA_ROPE)))
    wk = jnp.concatenate([k_nope, k_pe], axis=0).reshape(2 * LANE, MLA_H * LANE)
    wv = w[:, :, MLA_NOPE:].reshape(MLA_KVR, MLA_H * MLA_V)
    return wk.astype(BF16), wv.astype(BF16)


def _axial_angles(n_tok, dim):
    nf = dim // 4
    inv = ROPE_BASE ** (-jnp.arange(nf, dtype=F32) / nf)
    r = jnp.repeat(jnp.arange(n_tok // GRID_W, dtype=F32), GRID_W)
    cc = jnp.tile(jnp.arange(GRID_W, dtype=F32), n_tok // GRID_W)
    ang = jnp.concatenate([r[:, None] * inv, cc[:, None] * inv], axis=-1)
    return jnp.cos(ang), jnp.sin(ang)


def _rope_tables_mla(n_tok, lane0):
    cos, sin = _axial_angles(n_tok, MLA_ROPE)
    half = MLA_ROPE // 2
    z = lambda w: jnp.zeros((n_tok, w), F32)
    o = lambda w: jnp.ones((n_tok, w), F32)
    rest = LANE - lane0 - MLA_ROPE
    c = jnp.concatenate([o(lane0), cos, cos, o(rest)], axis=1)
    s1 = jnp.concatenate([z(lane0), -sin, z(half), z(rest)], axis=1)
    s2 = jnp.concatenate([z(lane0), z(half), sin, z(rest)], axis=1)
    return c, s1, s2


def _rope_tables_ret(n_tok):
    cos, sin = _axial_angles(n_tok, RET_DK)
    return jnp.concatenate([cos, cos], axis=1), jnp.concatenate([-sin, sin], axis=1)


def _bcast_dh(a):
    return jnp.broadcast_to(a.astype(F32)[:, :, None, None], a.shape + (8, LANE))


def _forward(x_prompt, x_sample, c, cache_ckv, cache_kpe, state_dn, state_ret, c_ctx, P, tiles):
    nb, sc, _ = x_prompt.shape
    nl, sl, _ = x_sample.shape
    past = cache_ckv.shape[2]
    depth = P['w_in'].shape[0]
    t_ctx, t_lat = nb * sc, nl * sl
    t = t_ctx + t_lat
    tm = tiles['tm']
    assert sc % tm == 0 and sl % tm == 0 and t_ctx % sl == 0

    assert sl % tiles['tc'] == 0 and t_ctx % tiles['tc'] == 0

    def grp_of_tile(rows_per_tile):
        nct = t_ctx // rows_per_tile
        return lambda i: jnp.where(i < nct, 0, 1 + (i - nct) // (sl // rows_per_tile))

    x = jnp.concatenate([x_prompt.reshape(t_ctx, D), x_sample.reshape(t_lat, D)], axis=0)
    ngrp = 1 + nl
    cvec = jnp.concatenate([c_ctx[None], c, jnp.zeros((-(ngrp) % 8, D), F32)], axis=0)

    tab_q = _rope_tables_mla(sl, MLA_NOPE)
    tab_k = _rope_tables_mla(sl, 0)
    tab_r = _rope_tables_ret(sl)

    ckv_l, kpe_l, dn_l, ret_l = [], [], [], []
    y_final = None
    for l in range(depth):
        mod3 = _ada(cvec, P['w_ada'][l], P['b_ada'][l]).reshape(cvec.shape[0], 1, 6 * D)
        w_a, w_b = _pack_w_in(P['w_in'][l])
        tp = tiles['tp']
        proj_a = _in_proj(x, mod3, P['norm_mix'][l], w_a, grp_of_tile, tp, N_PROJ_A, F32)
        proj = _in_proj(x, mod3, P['norm_mix'][l], w_b, grp_of_tile, tp, tiles['tn'], BF16)

        w_uq = _pack_w_uq(P['mla_w_uq'][l])
        wk, wv = _pack_w_ukv(P['mla_w_ukv'][l])
        gq, gkv = P['mla_q_norm'][l], P['mla_kv_norm'][l]
        q_c = _mla_q(proj_a, gq, w_uq, 0, t_ctx, tm, None, sc)
        q_l = _mla_q(proj_a, gq, w_uq, t_ctx, t_lat, tm, tab_q, sl)
        kvb = COL_KV // (2 * LANE)
        ckv_c, kpe_c, k_c, v_c = _mla_kv(proj_a, kvb, gkv, wk, wv, 0, t_ctx, tm, True, None, sc)
        _, _, k_l, v_l = _mla_kv(proj_a, kvb, gkv, wk, wv, t_ctx, t_lat, tm, True, tab_k, sl)
        cached = jnp.concatenate([cache_ckv[:, l], cache_kpe[:, l],
                                  jnp.zeros((nl, past, 2 * LANE - MLA_KVR - MLA_ROPE), F32)], axis=-1)
        pt = min(tm, past)
        _, _, k_p, v_p = _mla_kv(cached.reshape(nl * past, 2 * LANE), 0, gkv, wk, wv, 0, nl * past, pt,
                                 False, None, past)
        r3 = lambda a, n, s: a.reshape(n, s, a.shape[-1])
        y_mla = _attn(r3(q_c, nb, sc), [r3(k_c, nb, sc)], [r3(v_c, nb, sc)], min(sc, tiles['tq']), t, 0, None)
        y_mla = _attn(r3(q_l, nl, sl), [r3(k_p, nl, past), r3(k_l, nl, sl)],
                      [r3(v_p, nl, past), r3(v_l, nl, sl)], min(sl, tiles['tq']), t, t_ctx, y_mla)

        alog_b, dtb_b = _bcast_dh(P['dn_A_log'][l]), _bcast_dh(P['dn_dt_bias'][l])
        dn_args = (proj_a, proj, P['dn_conv'][l], alog_b, dtb_b, P['dn_norm'][l])
        y_dn, dn_fin = _deltanet(*dn_args, 0, nb, sc, None, l, t, None)
        y_dn, _ = _deltanet(*dn_args, t_ctx, nl, sl, state_dn, l, t, y_dn)

        dec_b = _bcast_dh(P['ret_decay'][l])
        y_ret, ret_fin = _retention(proj, dec_b, P['ret_gn'][l], 0, nb, sc, None, None, l, t, None)
        y_ret, _ = _retention(proj, dec_b, P['ret_gn'][l], t_ctx, nl, sl, tab_r, state_ret, l, t, y_ret)

        y_sc = _sconv(proj, P['sc_conv'][l], 0, nb, sc, t, None)
        y_sc = _sconv(proj, P['sc_conv'][l], t_ctx, nl, sl, t, y_sc)

        x_mid, h2, h2p = _branch_out((y_mla, y_dn, y_ret, y_sc), proj, x, mod3, P['norm_ffn'][l],
                                P['w_branch'][l].astype(BF16), P['w_out'][l].astype(BF16), grp_of_tile,
                                tiles['tb'])

        bias_b = jnp.broadcast_to(P['router_bias'][l].astype(F32)[:, None], (N_EXP, LANE))
        idx_t, w_t, rank_t, cnt = _route(h2, P['router'][l].T, bias_b, tiles['tr'])
        counts = cnt[:, 0]
        padded = (counts + MOE_BM - 1) // MOE_BM * MOE_BM
        pad_end = jnp.cumsum(padded)
        pstart = (pad_end - padded).astype(I32)
        n_blocks = (t * TOP_K) // MOE_BM + N_EXP
        blk_row0 = jnp.arange(n_blocks, dtype=I32) * MOE_BM
        block_e = jnp.minimum(jnp.sum((pad_end[None, :] <= blk_row0[:, None]).astype(I32), axis=1), N_EXP - 1)
        n_used = (pad_end[-1:] // MOE_BM).astype(I32)
        pos_t = _slot_pos(pstart, idx_t, rank_t, tiles['tr'])
        xs = _dispatch(pstart, padded.astype(I32), pos_t, h2p, n_blocks * MOE_BM, tiles['td'])
        ys = _experts(block_e, n_used, xs, P['w_eg'][l], P['w_eu'][l], P['w_ed'][l])
        fn = P['final_norm'] if l == depth - 1 else None
        outs = _combine(pos_t, w_t.T, h2, x_mid, mod3, P['w_sg'][l].astype(BF16),
                        P['w_su'][l].astype(BF16), P['w_sd'][l].astype(BF16), fn, ys, grp_of_tile, tiles['tc'])
        x = outs[0]
        if fn is not None:
            y_final = outs[1]

        ckv_l.append(ckv_c.reshape(nb, sc, MLA_KVR))
        kpe_l.append(kpe_c.reshape(nb, sc, MLA_ROPE))
        dn_l.append(dn_fin)
        ret_l.append(ret_fin)

    y_prompt = y_final[:t_ctx].reshape(nb, sc, D)
    y_sample = y_final[t_ctx:].reshape(nl, sl, D)
    return (y_prompt, y_sample, jnp.stack(ckv_l, axis=1), jnp.stack(kpe_l, axis=1),
            jnp.stack(dn_l, axis=1), jnp.stack(ret_l, axis=1))


_TILES = dict(tm=256, tp=1024, tn=2432, tb=512, tq=256, tr=512, td=256, tc=128)


def kernel(x_prompt, x_sample, c, cache_ckv, cache_kpe, state_dn, state_ret, c_ctx, w_ada, b_ada, norm_mix, norm_ffn, w_in, mla_q_norm, mla_w_uq, mla_kv_norm, mla_w_ukv, dn_conv, dn_A_log, dn_dt_bias, dn_norm, ret_decay, ret_gn, sc_conv, w_branch, w_out, router, router_bias, w_eg, w_eu, w_ed, w_sg, w_su, w_sd, final_norm):
    P = dict(w_ada=w_ada, b_ada=b_ada, norm_mix=norm_mix, norm_ffn=norm_ffn, w_in=w_in,
             mla_q_norm=mla_q_norm, mla_w_uq=mla_w_uq, mla_kv_norm=mla_kv_norm, mla_w_ukv=mla_w_ukv,
             dn_conv=dn_conv, dn_A_log=dn_A_log, dn_dt_bias=dn_dt_bias, dn_norm=dn_norm,
             ret_decay=ret_decay, ret_gn=ret_gn, sc_conv=sc_conv, w_branch=w_branch, w_out=w_out,
             router=router, router_bias=router_bias, w_eg=w_eg, w_eu=w_eu, w_ed=w_ed,
             w_sg=w_sg, w_su=w_su, w_sd=w_sd, final_norm=final_norm)
    return _forward(x_prompt, x_sample, c, cache_ckv, cache_kpe, state_dn, state_ret, c_ctx, P, _TILES)
```

```python
import functools
import math

import jax
import jax.numpy as jnp
import numpy as np
from jax import lax
from jax.experimental import pallas as pl
from jax.experimental.pallas import tpu as pltpu

F32 = jnp.float32
BF16 = jnp.bfloat16
I32 = jnp.int32
U32 = jnp.uint32

D = 1024
EPS = 1e-6
CHUNK = 64
PAIR = 2 * CHUNK
GRID_W = 64
ROPE_BASE = 10000.0

MLA_H, MLA_NOPE, MLA_ROPE, MLA_V, MLA_QR, MLA_KVR = 8, 64, 32, 64, 256, 128
DN_H, DN_DK = 4, 128
RET_H, RET_DK = 4, 128
MIX_W = 512
N_EXP, TOP_K, N_GRP, TOPK_GRP, D_EXP = 64, 8, 8, 4, 256
ROUTED_SCALE = 2.5

LANE = 128
COL_CQ = 0
COL_KV = 256
N_PROJ_A = 512
AB_LANE = 160
COL_GATE = 0
COL_DNQKV = 4096
COL_DNZ = 5632
COL_RET = 6144
COL_SC = 8192
N_PROJ_B = 9728

MOE_BM = 512
VMEM_LIMIT = 56 * 1024 * 1024


def _cparams(sem, vmem=None):
    return pltpu.CompilerParams(dimension_semantics=sem, vmem_limit_bytes=vmem or VMEM_LIMIT)


def _dot(a, b):
    return jnp.dot(a, b, preferred_element_type=F32)


def _dot_nt(a, b):
    return lax.dot_general(a, b, (((1,), (1,)), ((), ())), preferred_element_type=F32)


def _dot_tn(a, b):
    return lax.dot_general(a, b, (((0,), (0,)), ((), ())), preferred_element_type=F32)


def _split(a):
    hi = a.astype(BF16)
    lo = (a - hi.astype(F32)).astype(BF16)
    return hi, lo


def _dot3(a, b):
    ah, al = _split(a)
    bh, bl = _split(b)
    return _dot(ah, bh) + (_dot(ah, bl) + _dot(al, bh))


def _dot3s(a, b):
    (ah, al), (bh, bl) = a, b
    return _dot(jnp.concatenate([ah, ah, al], axis=1), jnp.concatenate([bh, bl, bh], axis=0))


def _dot3_nt(a, b):
    ah, al = _split(a)
    bh, bl = _split(b)
    return _dot_nt(ah, bh) + (_dot_nt(ah, bl) + _dot_nt(al, bh))


def _sigmoid(x):
    return 1.0 / (1.0 + jnp.exp(-x))


def _silu(x):
    return x * _sigmoid(x)


def _rms(x, g):
    return x * lax.rsqrt(jnp.mean(x * x, axis=-1, keepdims=True) + EPS) * g


def _ada_kernel(c_ref, w_ref, b_ref, o_ref):
    o_ref[...] = _dot3(_silu(c_ref[...]), w_ref[...]) + b_ref[...]


def _ada(cvec, w, b):
    n = w.shape[1]
    tn = 1024
    return pl.pallas_call(
        _ada_kernel,
        out_shape=jax.ShapeDtypeStruct((cvec.shape[0], n), F32),
        grid=(n // tn,),
        in_specs=[pl.BlockSpec(cvec.shape, lambda j: (0, 0)),
                  pl.BlockSpec((D, tn), lambda j: (0, j)),
                  pl.BlockSpec((1, tn), lambda j: (0, j))],
        out_specs=pl.BlockSpec((cvec.shape[0], tn), lambda j: (0, j)),
        compiler_params=_cparams(("arbitrary",)),
    )(cvec, w, b.reshape(1, n))


def _in_proj_kernel(x_ref, mod_ref, g_ref, w_ref, o_ref, h_scr):
    @pl.when(pl.program_id(1) == 0)
    def _():
        m = mod_ref[0]
        y = _rms(x_ref[...], g_ref[...])
        h_scr[...] = (y * (1.0 + m[:, D:2 * D]) + m[:, 0:D]).astype(BF16)

    o_ref[...] = _dot(h_scr[...], w_ref[...]).astype(o_ref.dtype)


def _in_proj(x, mod3, g, w, grp_of_tile, tm, tn, out_dtype):
    t = x.shape[0]
    n = w.shape[1]
    return pl.pallas_call(
        _in_proj_kernel,
        out_shape=jax.ShapeDtypeStruct((t, n), out_dtype),
        grid=(t // tm, n // tn),
        in_specs=[pl.BlockSpec((tm, D), lambda i, j: (i, 0)),
                  pl.BlockSpec((1, 1, 6 * D), lambda i, j: (grp_of_tile(tm)(i), 0, 0)),
                  pl.BlockSpec((1, D), lambda i, j: (0, 0)),
                  pl.BlockSpec((D, tn), lambda i, j: (0, j))],
        out_specs=pl.BlockSpec((tm, tn), lambda i, j: (i, j)),
        scratch_shapes=[pltpu.VMEM((tm, D), BF16)],
        compiler_params=_cparams(("arbitrary", "arbitrary")),
    )(x, mod3, g.reshape(1, D), w)


def _rope3(x, c, s1, s2, width):
    return x * c + pltpu.roll(x, width - 16, 1) * s1 + pltpu.roll(x, 16, 1) * s2


def _mla_q_kernel(*refs, rope):
    if rope:
        p_ref, g_ref, w_ref, c_ref, s1_ref, s2_ref, o_ref = refs
    else:
        p_ref, g_ref, w_ref, o_ref = refs
    y = _rms(p_ref[...], g_ref[...])
    q = _dot(y.astype(BF16), w_ref[...])
    if rope:
        tile = lambda r: jnp.concatenate([r[...]] * MLA_H, axis=1)
        q = _rope3(q, tile(c_ref), tile(s1_ref), tile(s2_ref), MLA_H * LANE)
    o_ref[...] = q.astype(BF16)


def _mla_q(proj, g, w, row_off, rows, tm, rope_tabs, seq):
    nrow = rows // tm
    r0 = row_off // tm
    in_specs = [pl.BlockSpec((tm, MLA_QR), lambda i: (r0 + i, COL_CQ // MLA_QR)),
                pl.BlockSpec((1, MLA_QR), lambda i: (0, 0)),
                pl.BlockSpec((MLA_QR, MLA_H * LANE), lambda i: (0, 0))]
    args = [proj, g.reshape(1, MLA_QR), w]
    if rope_tabs is not None:
        per = seq // tm
        in_specs += [pl.BlockSpec((tm, LANE), lambda i: (i % per, 0))] * 3
        args += list(rope_tabs)
    return pl.pallas_call(
        functools.partial(_mla_q_kernel, rope=rope_tabs is not None),
        out_shape=jax.ShapeDtypeStruct((rows, MLA_H * LANE), BF16),
        grid=(nrow,),
        in_specs=in_specs,
        out_specs=pl.BlockSpec((tm, MLA_H * LANE), lambda i: (i, 0)),
        compiler_params=_cparams(("arbitrary",)),
    )(*args)


def _mla_kv_kernel(*refs, norm, rope):
    if rope:
        p_ref, g_ref, wk_ref, wv_ref, c_ref, s1_ref, s2_ref, ckv_ref, kpe_ref, k_ref, v_ref = refs
    else:
        p_ref, g_ref, wk_ref, wv_ref, ckv_ref, kpe_ref, k_ref, v_ref = refs
    blk = p_ref[...]
    ckv = blk[:, :MLA_KVR]
    if norm:
        ckv = _rms(ckv, g_ref[...])
    kp = blk[:, MLA_KVR:]
    ckv_ref[...] = ckv
    kpe_ref[...] = kp[:, :MLA_ROPE]
    if rope:
        kp = _rope3(kp, c_ref[...], s1_ref[...], s2_ref[...], LANE)
    a = jnp.concatenate([ckv, kp], axis=1).astype(BF16)
    k_ref[...] = _dot(a, wk_ref[...]).astype(BF16)
    v_ref[...] = _dot(ckv.astype(BF16), wv_ref[...]).astype(BF16)


def _mla_kv(src, col_blk, g, wk, wv, row_off, rows, tm, norm, rope_tabs, seq):
    nrow = rows // tm
    r0 = row_off // tm
    in_specs = [pl.BlockSpec((tm, 2 * LANE), lambda i: (r0 + i, col_blk)),
                pl.BlockSpec((1, MLA_KVR), lambda i: (0, 0)),
                pl.BlockSpec((2 * LANE, MLA_H * LANE), lambda i: (0, 0)),
                pl.BlockSpec((MLA_KVR, MLA_H * MLA_V), lambda i: (0, 0))]
    args = [src, g.reshape(1, MLA_KVR), wk, wv]
    if rope_tabs is not None:
        per = seq // tm
        in_specs += [pl.BlockSpec((tm, LANE), lambda i: (i % per, 0))] * 3
        args += list(rope_tabs)
    return pl.pallas_call(
        functools.partial(_mla_kv_kernel, norm=norm, rope=rope_tabs is not None),
        out_shape=(jax.ShapeDtypeStruct((rows, MLA_KVR), F32),
                   jax.ShapeDtypeStruct((rows, MLA_ROPE), F32),
                   jax.ShapeDtypeStruct((rows, MLA_H * LANE), BF16),
                   jax.ShapeDtypeStruct((rows, MLA_H * MLA_V), BF16)),
        grid=(nrow,),
        in_specs=in_specs,
        out_specs=(pl.BlockSpec((tm, MLA_KVR), lambda i: (i, 0)),
                   pl.BlockSpec((tm, MLA_ROPE), lambda i: (i, 0)),
                   pl.BlockSpec((tm, MLA_H * LANE), lambda i: (i, 0)),
                   pl.BlockSpec((tm, MLA_H * MLA_V), lambda i: (i, 0))),
        compiler_params=_cparams(("arbitrary",)),
    )(*args)


def _attn_kernel(*refs, nseg):
    q_ref = refs[0]
    k_refs = refs[1:1 + nseg]
    v_refs = refs[1 + nseg:1 + 2 * nseg]
    o_ref = refs[1 + 2 * nseg]
    scale = (MLA_NOPE + MLA_ROPE) ** -0.5
    outs = []
    for h in range(MLA_H):
        qh = q_ref[0, :, h * LANE:(h + 1) * LANE]
        ss = [_dot_nt(qh, k_ref[0, :, h * LANE:(h + 1) * LANE]) * scale for k_ref in k_refs]
        m = functools.reduce(jnp.maximum, [jnp.max(s, axis=-1, keepdims=True) for s in ss])
        es = [jnp.exp(s - m) for s in ss]
        inv = 1.0 / functools.reduce(jnp.add, [jnp.sum(e, axis=-1, keepdims=True) for e in es])
        o = functools.reduce(jnp.add, [
            _dot((e * inv).astype(BF16), v_ref[0, :, h * MLA_V:(h + 1) * MLA_V])
            for e, v_ref in zip(es, v_refs)])
        outs.append(o)
    o_ref[...] = jnp.concatenate(outs, axis=1)


def _into(kernel, in_specs, args, prev):
    if prev is None:
        return kernel, {}
    pos = len(args)
    in_specs.append(pl.BlockSpec(memory_space=pl.ANY))
    args.append(prev)
    return (lambda *refs: kernel(*refs[:pos], *refs[pos + 1:])), {pos: 0}


def _attn(q, ks, vs, tq, t_total, row_off, prev):
    b, s, _ = q.shape
    nseg = len(ks)
    in_specs = [pl.BlockSpec((1, tq, MLA_H * LANE), lambda i, j: (i, j, 0))]
    in_specs += [pl.BlockSpec((1,) + k.shape[1:], lambda i, j: (i, 0, 0)) for k in ks]
    in_specs += [pl.BlockSpec((1,) + v.shape[1:], lambda i, j: (i, 0, 0)) for v in vs]
    args = [q, *ks, *vs]
    kern, alias = _into(functools.partial(_attn_kernel, nseg=nseg), in_specs, args, prev)
    r0, per = row_off // tq, s // tq
    return pl.pallas_call(
        kern,
        out_shape=jax.ShapeDtypeStruct((t_total, MLA_H * MLA_V), F32),
        grid=(b, per),
        in_specs=in_specs,
        out_specs=pl.BlockSpec((tq, MLA_H * MLA_V), lambda i, j: (r0 + i * per + j, 0)),
        input_output_aliases=alias,
        compiler_params=_cparams(("arbitrary", "arbitrary")),
    )(*args)


def _conv3(x, w):
    s = x.shape[0]
    row = lax.broadcasted_iota(I32, x.shape, 0)
    prev = jnp.where(row == 0, 0.0, pltpu.roll(x, 1, 0))
    nxt = jnp.where(row == s - 1, 0.0, pltpu.roll(x, s - 1, 0))
    return prev * w[0:1] + x * w[1:2] + nxt * w[2:3]


def _sconv_kernel(b_ref, c_ref, x_ref, w_ref, o_ref):
    f = lambda r: r[...].astype(F32)
    o_ref[...] = f(b_ref) * _conv3(f(c_ref) * f(x_ref), w_ref[...])


def _sconv(proj, w, row_off, nseq, seq, t_total, prev):
    r0 = row_off // seq
    cb = COL_SC // MIX_W
    spec = lambda off: pl.BlockSpec((seq, MIX_W), lambda b: (r0 + b, cb + off))
    in_specs = [spec(0), spec(1), spec(2), pl.BlockSpec((3, MIX_W), lambda b: (0, 0))]
    args = [proj, proj, proj, w]
    kern, alias = _into(_sconv_kernel, in_specs, args, prev)
    return pl.pallas_call(
        kern,
        out_shape=jax.ShapeDtypeStruct((t_total, MIX_W), F32),
        grid=(nseq,),
        in_specs=in_specs,
        out_specs=pl.BlockSpec((seq, MIX_W), lambda b: (r0 + b, 0)),
        input_output_aliases=alias,
        compiler_params=_cparams(("arbitrary",)),
    )(*args)


def _ret_kernel(*refs, seq, rope, has_s0):
    it = iter(refs)
    q_ref, k_ref, v_ref, g_ref, dec_ref, gn_ref = (next(it) for _ in range(6))
    if rope:
        c_ref, s_ref = next(it), next(it)
    if has_s0:
        s0_ref = next(it)
    y_ref, sfin_ref, of_scr, ob_scr = (next(it) for _ in range(4))
    nch = seq // CHUNK

    q = q_ref[...].astype(F32)
    k = k_ref[...].astype(F32) * (RET_DK ** -0.5)
    if rope:
        c, sn = c_ref[...], s_ref[...]
        q = q * c + pltpu.roll(q, RET_DK // 2, 1) * sn
        k = k * c + pltpu.roll(k, RET_DK // 2, 1) * sn
    v = v_ref[...].astype(F32)

    ti = lax.broadcasted_iota(I32, (CHUNK, CHUNK), 0)
    si = lax.broadcasted_iota(I32, (CHUNK, CHUNK), 1)
    pos = lax.broadcasted_iota(I32, (CHUNK, 1), 0).astype(F32)
    for d in range(2):
        lg = -jnp.exp(dec_ref[d, 0, 0:1, 0:1])
        if d == 0:
            dist = (ti - si).astype(F32)
            qpow, kpow = pos + 1.0, (CHUNK - 1.0) - pos
        else:
            dist = (si - ti).astype(F32)
            qpow, kpow = CHUNK - pos, pos
        decay = jnp.where(dist >= 0, jnp.exp(jnp.maximum(dist, 0.0) * lg), 0.0)
        qs = jnp.exp(qpow * lg)
        ks = jnp.exp(kpow * lg)
        gtot = jnp.exp(CHUNK * lg)
        st = s0_ref[0, 0, d, 0] if has_s0 else jnp.zeros((RET_DK, RET_DK), F32)
        o_scr = of_scr if d == 0 else ob_scr
        order = range(nch) if d == 0 else range(nch - 1, -1, -1)
        for n in order:
            sl = slice(n * CHUNK, (n + 1) * CHUNK)
            qn, kn, vn = q[sl], k[sl], v[sl]
            vb = vn.astype(BF16)
            a = _dot_nt(qn.astype(BF16), kn.astype(BF16)) * decay
            o = _dot(a.astype(BF16), vb) + _dot((qn * qs).astype(BF16), st.astype(BF16))
            st = gtot * st + _dot_tn((kn * ks).astype(BF16), vb)
            o_scr[sl, :] = o
        sfin_ref[0, d, 0] = st

    o = of_scr[...] + ob_scr[...]
    dlt = o - jnp.mean(o, axis=-1, keepdims=True)
    y = dlt * lax.rsqrt(jnp.mean(dlt * dlt, axis=-1, keepdims=True) + EPS) * gn_ref[...]
    y_ref[...] = _silu(g_ref[...].astype(F32)) * y


def _retention(proj, dec_b, gn, row_off, nseq, seq, rope_tabs, s0, layer, t_total, prev):
    r0 = row_off // seq
    cb = COL_RET // LANE
    spec = lambda off: pl.BlockSpec((seq, LANE), lambda b, h: (r0 + b, cb + off + h))
    in_specs = [spec(0), spec(RET_H), spec(2 * RET_H), spec(3 * RET_H),
                pl.BlockSpec((2, 1, 8, LANE), lambda b, h: (0, h, 0, 0)),
                pl.BlockSpec((1, LANE), lambda b, h: (0, h))]
    args = [proj, proj, proj, proj, dec_b, gn.reshape(1, RET_H * RET_DK)]
    if rope_tabs is not None:
        in_specs += [pl.BlockSpec((seq, LANE), lambda b, h: (0, 0))] * 2
        args += list(rope_tabs)
    if s0 is not None:
        in_specs.append(pl.BlockSpec((1, 1, 2, 1, RET_DK, RET_DK), lambda b, h: (b, layer, 0, h, 0, 0)))
        args.append(s0)
    kern, alias = _into(functools.partial(_ret_kernel, seq=seq, rope=rope_tabs is not None, has_s0=s0 is not None),
                        in_specs, args, prev)
    return pl.pallas_call(
        kern,
        out_shape=(jax.ShapeDtypeStruct((t_total, RET_H * RET_DK), F32),
                   jax.ShapeDtypeStruct((nseq, 2, RET_H, RET_DK, RET_DK), F32)),
        grid=(nseq, RET_H),
        in_specs=in_specs,
        out_specs=(pl.BlockSpec((seq, LANE), lambda b, h: (r0 + b, h)),
                   pl.BlockSpec((1, 2, 1, RET_DK, RET_DK), lambda b, h: (b, 0, h, 0, 0))),
        scratch_shapes=[pltpu.VMEM((seq, RET_DK), F32), pltpu.VMEM((seq, RET_DK), F32)],
        input_output_aliases=alias,
        compiler_params=_cparams(("arbitrary", "arbitrary")),
    )(*args)


def _softplus(x):
    return jnp.maximum(x, 0.0) + jnp.log1p(jnp.exp(-jnp.abs(x)))


def _l2n(x):
    return x * lax.rsqrt(jnp.sum(x * x, axis=-1, keepdims=True) + EPS)


def _dn_kernel(*refs, seq, has_s0):
    it = iter(refs)
    q_ref, k_ref, v_ref, z_ref, ab_ref, wq_ref, wk_ref, wv_ref, alog_ref, dtb_ref, nrm_ref = (
        next(it) for _ in range(11))
    if has_s0:
        s0_ref = next(it)
    y_ref, sfin_ref = next(it), next(it)
    of_scr, ob_scr, q_scr, k_scr, v_scr, qe_scr, oc_scr, m_scr, c_scr, gt_scr = (next(it) for _ in range(10))
    nch = seq // CHUNK
    npair = seq // PAIR
    hcols = lambda hh: slice(hh * DN_DK, (hh + 1) * DN_DK)

    xq = _silu(_conv3(q_ref[...].astype(F32), wq_ref[...]))
    xk = _silu(_conv3(k_ref[...].astype(F32), wk_ref[...]))
    v_scr[...] = _silu(_conv3(v_ref[...].astype(F32), wv_ref[...]))
    for hh in range(DN_H):
        q_scr[:, hcols(hh)] = _l2n(xq[:, hcols(hh)]) * (DN_DK ** -0.5)
        k_scr[:, hcols(hh)] = _l2n(xk[:, hcols(hh)])

    ti = lax.broadcasted_iota(I32, (PAIR, PAIR), 0)
    si = lax.broadcasted_iota(I32, (PAIR, PAIR), 1)
    same = (ti >> 6) == (si >> 6)
    s_in = si & (CHUNK - 1)
    eye = ti == si
    eye_f = eye.astype(F32)
    lane = lax.broadcasted_iota(I32, (1, 2 * LANE), 1)

    def ab_col(blk, idx):
        return jnp.sum(jnp.where(lane == AB_LANE + idx, blk, 0.0), axis=1, keepdims=True)

    masks = []
    for d in range(2):
        masks.append((same & ((si <= ti) if d == 0 else (si >= ti)),
                      same & ((si < ti) if d == 0 else (si > ti)),
                      same & ((ti <= si) if d == 0 else (ti >= si)),
                      same & (s_in == (CHUNK - 1 if d == 0 else 0))))

    def prep(p, carry):
        rows = pl.ds(pl.multiple_of(p * PAIR, PAIR), PAIR)
        blk = ab_ref[rows, :]
        chains = []
        for hh in range(DN_H):
            cs = hcols(hh)
            qn, kn, vn = q_scr[rows, cs], k_scr[rows, cs], v_scr[rows, cs]
            kb = kn.astype(BF16)
            kk = _dot_nt(kb, kb)
            qkr = _dot_nt(qn.astype(BF16), kb)
            for d in range(2):
                incl, strict, incl_t, last_s = masks[d]
                neg_a = -jnp.exp(alog_ref[d, hh, 0:1, 0:1])
                dtb = dtb_ref[d, hh, 0:1, 0:1]
                la = neg_a * _softplus(ab_col(blk, d * DN_H + hh) + dtb)
                beta = _sigmoid(ab_col(blk, 2 * DN_H + d * DN_H + hh))
                g_row = jnp.sum(jnp.where(incl_t, la, 0.0), axis=0, keepdims=True)
                g_col = jnp.sum(jnp.where(eye, g_row, 0.0), axis=1, keepdims=True)
                g_end = jnp.sum(jnp.where(last_s, g_row, 0.0), axis=1, keepdims=True)
                decay = jnp.where(incl, jnp.exp(jnp.where(incl, g_col - g_row, 0.0)), 0.0)
                nmat = jnp.where(strict, -(beta * decay * kk), 0.0)
                eg = jnp.exp(g_col)
                chains.append(dict(
                    hh=hh, d=d, cs=cs, g_row=g_row, tinv=eye_f + nmat, pw=_split(nmat),
                    bv=_split(beta * vn), bk=_split((beta * eg) * kn), egq=eg * qn,
                    qk=(qkr * decay).astype(BF16), kd=(jnp.exp(g_end - g_col) * kn).astype(BF16)))
        for it in range(5):
            fine = it < 2
            sq = [_dot3s(c['pw'], c['pw']) if fine else _dot(c['pw'][0], c['pw'][0]) for c in chains]
            for c, s in zip(chains, sq):
                c['pw'] = _split(s)
            up = [_dot3s(_split(c['tinv']), c['pw']) if fine else _dot(c['tinv'].astype(BF16), c['pw'][0])
                  for c in chains]
            for c, u in zip(chains, up):
                c['tinv'] = c['tinv'] + u
        for c in chains:
            c['ts'] = _split(c['tinv'])
        for c in chains:
            c['uvb'] = _dot3s(c['ts'], c['bv']).astype(BF16)
        for c in chains:
            c['wkb'] = _dot3s(c['ts'], c['bk']).astype(BF16)
        for c in chains:
            qe_scr[c['d'], rows, c['cs']] = (c['egq'] - _dot(c['qk'], c['wkb'])).astype(BF16)
            oc_scr[c['d'], rows, c['cs']] = _dot(c['qk'], c['uvb'])
        for c in chains:
            d, j = c['d'], c['d'] * DN_H + c['hh']
            for half in range(2):
                sl = slice(half * CHUNK, (half + 1) * CHUNK)
                n = p * 2 + half
                mrows = pl.ds(pl.multiple_of(n * DN_DK, DN_DK), DN_DK)
                m_scr[j, mrows, :] = _dot_tn(c['kd'][sl], c['wkb'][sl]).astype(BF16)
                c_scr[j, mrows, :] = _dot_tn(c['kd'][sl], c['uvb'][sl])
                e = half * CHUNK + (CHUNK - 1 if d == 0 else 0)
                gt_scr[j, pl.ds(pl.multiple_of(n * 8, 8), 8), :] = jnp.broadcast_to(
                    jnp.exp(c['g_row'][:, e:e + 1]), (8, LANE))
        return carry

    lax.fori_loop(0, npair, prep, 0)

    for d in range(2):
        for hh in range(DN_H):
            sfin_ref[0, d, hh] = s0_ref[0, 0, d, hh] if has_s0 else jnp.zeros((DN_DK, DN_DK), F32)

    def step(i, carry):
        for d in range(2):
            n = i if d == 0 else nch - 1 - i
            rows = pl.ds(pl.multiple_of(n * CHUNK, CHUNK), CHUNK)
            mrows = pl.ds(pl.multiple_of(n * DN_DK, DN_DK), DN_DK)
            o_scr = of_scr if d == 0 else ob_scr
            for hh in range(DN_H):
                st = sfin_ref[0, d, hh]
                sb = st.astype(BF16)
                o_scr[rows, hcols(hh)] = _dot(qe_scr[d, rows, hcols(hh)], sb) + oc_scr[d, rows, hcols(hh)]
                gt = gt_scr[d * DN_H + hh, pl.ds(pl.multiple_of(n * 8, 8), 8), :][0:1, 0:1]
                sfin_ref[0, d, hh] = (gt * st - _dot(m_scr[d * DN_H + hh, mrows, :], sb)
                                      + c_scr[d * DN_H + hh, mrows, :])
        return carry

    lax.fori_loop(0, nch, step, 0)

    for hh in range(DN_H):
        o = of_scr[:, hcols(hh)] + ob_scr[:, hcols(hh)]
        y = o * lax.rsqrt(jnp.mean(o * o, axis=-1, keepdims=True) + EPS) * nrm_ref[...]
        y_ref[:, hcols(hh)] = y * _silu(z_ref[:, hcols(hh)].astype(F32))


def _deltanet(proj_a, proj, conv_w, alog_b, dtb_b, nrm, row_off, nseq, seq, s0, layer, t_total, prev):
    r0 = row_off // seq
    w = DN_H * DN_DK
    spec = lambda col: pl.BlockSpec((seq, w), lambda b: (r0 + b, col // w))
    wspec = lambda j: pl.BlockSpec((3, w), lambda b: (0, j))
    bspec = pl.BlockSpec((2, DN_H, 8, LANE), lambda b: (0, 0, 0, 0))
    in_specs = [spec(COL_DNQKV), spec(COL_DNQKV + w), spec(COL_DNQKV + 2 * w), spec(COL_DNZ),
                pl.BlockSpec((seq, 2 * LANE), lambda b: (r0 + b, COL_KV // (2 * LANE))),
                wspec(0), wspec(1), wspec(2), bspec, bspec,
                pl.BlockSpec((1, DN_DK), lambda b: (0, 0))]
    args = [proj, proj, proj, proj, proj_a, conv_w, conv_w, conv_w, alog_b, dtb_b, nrm.reshape(1, DN_DK)]
    if s0 is not None:
        in_specs.append(pl.BlockSpec((1, 1, 2, DN_H, DN_DK, DN_DK), lambda b: (b, layer, 0, 0, 0, 0)))
        args.append(s0)
    nch = seq // CHUNK
    scratch = [pltpu.VMEM((seq, w), F32), pltpu.VMEM((seq, w), F32),
               pltpu.VMEM((seq, w), F32), pltpu.VMEM((seq, w), F32), pltpu.VMEM((seq, w), F32),
               pltpu.VMEM((2, seq, w), BF16), pltpu.VMEM((2, seq, w), F32),
               pltpu.VMEM((2 * DN_H, nch * DN_DK, DN_DK), BF16), pltpu.VMEM((2 * DN_H, nch * DN_DK, DN_DK), F32),
               pltpu.VMEM((2 * DN_H, nch * 8, LANE), F32)]
    kern, alias = _into(functools.partial(_dn_kernel, seq=seq, has_s0=s0 is not None), in_specs, args, prev)
    return pl.pallas_call(
        kern,
        out_shape=(jax.ShapeDtypeStruct((t_total, w), F32),
                   jax.ShapeDtypeStruct((nseq, 2, DN_H, DN_DK, DN_DK), F32)),
        grid=(nseq,),
        in_specs=in_specs,
        out_specs=(pl.BlockSpec((seq, w), lambda b: (r0 + b, 0)),
                   pl.BlockSpec((1, 2, DN_H, DN_DK, DN_DK), lambda b: (b, 0, 0, 0, 0))),
        scratch_shapes=scratch,
        input_output_aliases=alias,
        compiler_params=_cparams(("arbitrary",)),
    )(*args)


def _pack_bf16_pairs(x):
    n = x.shape[1] // 2
    bits = lax.bitcast_convert_type(x.astype(BF16).astype(F32), U32)
    return (bits[:, n:] & jnp.uint32(0xFFFF0000)) | (bits[:, :n] >> 16)


def _unpack_bf16_pairs(u):
    lo = lax.bitcast_convert_type(u << 16, F32)
    hi = lax.bitcast_convert_type(u & jnp.uint32(0xFFFF0000), F32)
    return lo, hi


ROW_T = (D // 2) // LANE


def _store_row_tiles(ref, x):
    for s in range(ROW_T):
        ref[:, s, :] = x[:, s * LANE:(s + 1) * LANE]


def _load_row_tiles(ref):
    return jnp.concatenate([ref[:, s, :] for s in range(ROW_T)], axis=1)


def _branch_out_kernel(b0, b1, b2, b3, g0, g1, g2, g3, x_ref, mod_ref, nf_ref, wb_ref, wo_ref,
                       xo_ref, h_ref, hp_ref):
    acc = None
    for n, (br, gl) in enumerate(((b0, g0), (b1, g1), (b2, g2), (b3, g3))):
        p = _sigmoid(gl[...].astype(F32)) * _dot(br[...].astype(BF16), wb_ref[n])
        acc = p if acc is None else acc + p
    y = _dot(acc.astype(BF16), wo_ref[...])
    m = mod_ref[0]
    x = x_ref[...] + m[:, 2 * D:3 * D] * y
    xo_ref[...] = x
    h = _rms(x, nf_ref[...]) * (1.0 + m[:, 4 * D:5 * D]) + m[:, 3 * D:4 * D]
    h_ref[...] = h
    _store_row_tiles(hp_ref, _pack_bf16_pairs(h))


def _branch_out(branches, proj, x, mod3, nf, wb, wo, grp_of_tile, tm):
    t = x.shape[0]
    gb = COL_GATE // D
    bspec = pl.BlockSpec((tm, MIX_W), lambda i: (i, 0))
    gspec = lambda n: pl.BlockSpec((tm, D), lambda i: (i, gb + n))
    row = pl.BlockSpec((tm, D), lambda i: (i, 0))
    return pl.pallas_call(
        _branch_out_kernel,
        out_shape=(jax.ShapeDtypeStruct((t, D), F32), jax.ShapeDtypeStruct((t, D), F32),
                   jax.ShapeDtypeStruct((t, ROW_T, LANE), U32)),
        grid=(t // tm,),
        in_specs=[bspec] * 4 + [gspec(n) for n in range(4)] + [
            row,
            pl.BlockSpec((1, 1, 6 * D), lambda i: (grp_of_tile(tm)(i), 0, 0)),
            pl.BlockSpec((1, D), lambda i: (0, 0)),
            pl.BlockSpec((4, MIX_W, D), lambda i: (0, 0, 0)),
            pl.BlockSpec((D, D), lambda i: (0, 0))],
        out_specs=(row, row, pl.BlockSpec((tm, ROW_T, LANE), lambda i: (i, 0, 0))),
        compiler_params=_cparams(("arbitrary",)),
    )(*branches, proj, proj, proj, proj, x, mod3, nf.reshape(1, D), wb, wo)


def _route_kernel(h_ref, wr_ref, bias_ref, idx_ref, w_ref, rank_ref, cnt_ref, run_scr, *, tm):
    @pl.when(pl.program_id(0) == 0)
    def _():
        run_scr[...] = jnp.zeros_like(run_scr)

    neg = -jnp.inf
    gsz = N_EXP // N_GRP
    scores = _sigmoid(_dot3_nt(wr_ref[...], h_ref[...]))
    choice = scores + bias_ref[:, 0:1]
    row8 = lax.broadcasted_iota(I32, (gsz, tm), 0)
    gscore = []
    for g in range(N_GRP):
        blk = choice[g * gsz:(g + 1) * gsz]
        m1 = jnp.max(blk, axis=0, keepdims=True)
        i1 = jnp.min(jnp.where(blk == m1, row8, gsz), axis=0, keepdims=True)
        m2 = jnp.max(jnp.where(row8 == i1, neg, blk), axis=0, keepdims=True)
        gscore.append(m1 + m2)
    masked = []
    for g in range(N_GRP):
        rank = jnp.zeros((1, tm), I32)
        for g2 in range(N_GRP):
            if g2 == g:
                continue
            ahead = (gscore[g2] >= gscore[g]) if g2 < g else (gscore[g2] > gscore[g])
            rank = rank + ahead.astype(I32)
        masked.append(jnp.where(rank < TOPK_GRP, choice[g * gsz:(g + 1) * gsz], neg))
    cur = jnp.concatenate(masked, axis=0)
    row = lax.broadcasted_iota(I32, (N_EXP, tm), 0)
    sel = jnp.zeros((N_EXP, tm), jnp.bool_)
    idxs, scs = [], []
    for _ in range(TOP_K):
        m = jnp.max(cur, axis=0, keepdims=True)
        ik = jnp.min(jnp.where(cur == m, row, N_EXP), axis=0, keepdims=True)
        hit = row == ik
        scs.append(jnp.sum(jnp.where(hit, scores, 0.0), axis=0, keepdims=True))
        idxs.append(ik)
        cur = jnp.where(hit, neg, cur)
        sel = sel | hit
    tot = functools.reduce(jnp.add, scs)
    self_f = jnp.where(sel, 1.0, 0.0)
    tri = (lax.broadcasted_iota(I32, (tm, tm), 0) <= lax.broadcasted_iota(I32, (tm, tm), 1))
    csum = _dot(self_f.astype(BF16), jnp.where(tri, 1.0, 0.0).astype(BF16))
    run = run_scr[:, 0:1]
    rank_all = run + csum - self_f
    ranks = [jnp.sum(jnp.where(row == ik, rank_all, 0.0), axis=0, keepdims=True) for ik in idxs]
    idx_ref[...] = jnp.concatenate(idxs, axis=0)
    w_ref[...] = jnp.concatenate([s / tot * ROUTED_SCALE for s in scs], axis=0)
    rank_ref[...] = jnp.concatenate(ranks, axis=0).astype(I32)
    new_run = run + csum[:, tm - 1:tm]
    run_scr[...] = jnp.broadcast_to(new_run, run_scr.shape)
    cnt_ref[...] = jnp.broadcast_to(new_run, cnt_ref.shape).astype(I32)


def _route(h, wr_t, bias_b, tm):
    t = h.shape[0]
    tok = pl.BlockSpec((TOP_K, tm), lambda i: (0, i))
    return pl.pallas_call(
        functools.partial(_route_kernel, tm=tm),
        out_shape=(jax.ShapeDtypeStruct((TOP_K, t), I32), jax.ShapeDtypeStruct((TOP_K, t), F32),
                   jax.ShapeDtypeStruct((TOP_K, t), I32), jax.ShapeDtypeStruct((N_EXP, LANE), I32)),
        grid=(t // tm,),
        in_specs=[pl.BlockSpec((tm, D), lambda i: (i, 0)),
                  pl.BlockSpec((N_EXP, D), lambda i: (0, 0)),
                  pl.BlockSpec((N_EXP, LANE), lambda i: (0, 0))],
        out_specs=(tok, tok, tok, pl.BlockSpec((N_EXP, LANE), lambda i: (0, 0))),
        scratch_shapes=[pltpu.VMEM((N_EXP, LANE), F32)],
        compiler_params=_cparams(("arbitrary",)),
    )(h, wr_t, bias_b)


def _slot_pos_kernel(ps_ref, idx_ref, rank_ref, pos_ref):
    idx = idx_ref[...]
    pos = rank_ref[...]
    for e in range(N_EXP):
        pos = pos + jnp.where(idx == e, ps_ref[e], 0)
    pos_ref[...] = pos


def _slot_pos(pstart, idx_t, rank_t, tm):
    t = idx_t.shape[1]
    tok = pl.BlockSpec((TOP_K, tm), lambda i, *_: (0, i))
    return pl.pallas_call(
        _slot_pos_kernel,
        out_shape=jax.ShapeDtypeStruct((TOP_K, t), I32),
        grid_spec=pltpu.PrefetchScalarGridSpec(
            num_scalar_prefetch=1, grid=(t // tm,), in_specs=[tok, tok], out_specs=tok),
        compiler_params=_cparams(("arbitrary",)),
    )(pstart, idx_t, rank_t)


def _dispatch_kernel(pstart_ref, padded_ref, pos_ref, h_ref, xs_ref, zero_scr, sem, zsem, *, tm):
    def zero_copy(e):
        start = pl.multiple_of(pstart_ref[e] + padded_ref[e] - MOE_BM, MOE_BM)
        return pltpu.make_async_copy(zero_scr, xs_ref.at[pl.ds(start, MOE_BM)], zsem)

    @pl.when(pl.program_id(0) == 0)
    def _():
        zero_scr[...] = jnp.zeros_like(zero_scr)

        def zstart(e, c):
            @pl.when(padded_ref[e] > 0)
            def _():
                zero_copy(e).start()
            return c

        def zwait(e, c):
            @pl.when(padded_ref[e] > 0)
            def _():
                zero_copy(e).wait()
            return c

        lax.fori_loop(0, N_EXP, zstart, 0)
        lax.fori_loop(0, N_EXP, zwait, 0)

    def row_copy(t, k):
        return pltpu.make_async_copy(h_ref.at[pl.ds(t, 1)], xs_ref.at[pl.ds(pos_ref[k, t], 1)], sem)

    def issue(t, c):
        for k in range(TOP_K):
            row_copy(t, k).start(priority=k % 2)
        return c

    def drain(t, c):
        for k in range(TOP_K):
            row_copy(t, k).wait()
        return c

    lax.fori_loop(0, tm, issue, 0)
    lax.fori_loop(0, tm, drain, 0)


def _dispatch(pstart, padded, pos_t, h, n_slots, tm):
    t = h.shape[0]
    smem_tok = pl.BlockSpec((TOP_K, tm), lambda i, *_: (0, i), memory_space=pltpu.SMEM)
    return pl.pallas_call(
        functools.partial(_dispatch_kernel, tm=tm),
        out_shape=jax.ShapeDtypeStruct((n_slots,) + h.shape[1:], h.dtype),
        grid_spec=pltpu.PrefetchScalarGridSpec(
            num_scalar_prefetch=2,
            grid=(t // tm,),
            in_specs=[smem_tok, pl.BlockSpec((tm,) + h.shape[1:], lambda i, *_: (i, 0, 0))],
            out_specs=pl.BlockSpec(memory_space=pl.ANY),
            scratch_shapes=[pltpu.VMEM((MOE_BM,) + h.shape[1:], h.dtype), pltpu.SemaphoreType.DMA(()),
                            pltpu.SemaphoreType.DMA(())]),
        compiler_params=_cparams(("arbitrary",)),
    )(pstart, padded, pos_t, h)


def _experts_kernel(be_ref, nu_ref, x_ref, wg_ref, wu_ref, wd_ref, y_ref, wg_s, wu_s, wd_s):
    b = pl.program_id(0)

    @pl.when(jnp.logical_or(b == 0, be_ref[b] != be_ref[jnp.maximum(b - 1, 0)]))
    def _():
        wg_s[...] = wg_ref[0].astype(BF16)
        wu_s[...] = wu_ref[0].astype(BF16)
        wd_s[...] = wd_ref[0].astype(BF16)

    @pl.when(b < nu_ref[0])
    def _():
        x = jnp.concatenate(_unpack_bf16_pairs(_load_row_tiles(x_ref)), axis=1).astype(BF16)
        a = _silu(_dot(x, wg_s[...])) * _dot(x, wu_s[...])
        _store_row_tiles(y_ref, _pack_bf16_pairs(_dot(a.astype(BF16), wd_s[...])))


def _experts(block_e, n_used, xs, wg, wu, wd):
    nb = xs.shape[0] // MOE_BM
    blk = lambda b, be, nu: (jnp.minimum(b, nu[0] - 1), 0, 0)
    wsel = lambda b, be, nu: (be[b], 0, 0)
    return pl.pallas_call(
        _experts_kernel,
        out_shape=jax.ShapeDtypeStruct(xs.shape, U32),
        grid_spec=pltpu.PrefetchScalarGridSpec(
            num_scalar_prefetch=2,
            grid=(nb,),
            in_specs=[pl.BlockSpec((MOE_BM, ROW_T, LANE), blk),
                      pl.BlockSpec((1, D, D_EXP), wsel),
                      pl.BlockSpec((1, D, D_EXP), wsel),
                      pl.BlockSpec((1, D_EXP, D), wsel)],
            out_specs=pl.BlockSpec((MOE_BM, ROW_T, LANE), blk),
            scratch_shapes=[pltpu.VMEM((D, D_EXP), BF16), pltpu.VMEM((D, D_EXP), BF16),
                            pltpu.VMEM((D_EXP, D), BF16)]),
        compiler_params=_cparams(("arbitrary",)),
    )(block_e, n_used, xs, wg, wu, wd)


def _combine_kernel(*refs, tm, final):
    it = iter(refs)
    pos_ref, w_ref, h_ref, x_ref, mod_ref, wsg_ref, wsu_ref, wsd_ref = (next(it) for _ in range(8))
    if final:
        fn_ref = next(it)
    ys_ref = next(it)
    xo_ref = next(it)
    if final:
        yo_ref = next(it)
    buf, sem = next(it), next(it)

    def row_copy(t, k):
        return pltpu.make_async_copy(ys_ref.at[pl.ds(pos_ref[k, t], 1)], buf.at[k, pl.ds(t, 1)], sem)

    def issue(t, c):
        for k in range(TOP_K):
            row_copy(t, k).start(priority=k % 2)
        return c

    def drain(t, c):
        for k in range(TOP_K):
            row_copy(t, k).wait()
        return c

    lax.fori_loop(0, tm, issue, 0)
    hb = h_ref[...].astype(BF16)
    a = _silu(_dot(hb, wsg_ref[...])) * _dot(hb, wsu_ref[...])
    shared = _dot(a.astype(BF16), wsd_ref[...])
    lax.fori_loop(0, tm, drain, 0)
    w = w_ref[...]
    lo, hi = None, None
    for k in range(TOP_K):
        yl, yh = _unpack_bf16_pairs(_load_row_tiles(buf.at[k]))
        wk = w[:, k:k + 1]
        lo = yl * wk if lo is None else lo + yl * wk
        hi = yh * wk if hi is None else hi + yh * wk
    routed = jnp.concatenate([lo, hi], axis=1)
    m = mod_ref[0]
    x = x_ref[...] + m[:, 5 * D:6 * D] * (routed + shared)
    xo_ref[...] = x
    if final:
        yo_ref[...] = _rms(x, fn_ref[...])


def _combine(pos_t, w_tok, h, x, mod3, wsg, wsu, wsd, fn, ys, grp_of_tile, tm):
    t = h.shape[0]
    final = fn is not None
    row = pl.BlockSpec((tm, D), lambda i: (i, 0))
    full = lambda shp: pl.BlockSpec(shp, lambda i: (0,) * len(shp))
    in_specs = [pl.BlockSpec((TOP_K, tm), lambda i: (0, i), memory_space=pltpu.SMEM),
                pl.BlockSpec((tm, TOP_K), lambda i: (i, 0)), row, row,
                pl.BlockSpec((1, 1, 6 * D), lambda i: (grp_of_tile(tm)(i), 0, 0)),
                full((D, D_EXP)), full((D, D_EXP)), full((D_EXP, D))]
    args = [pos_t, w_tok, h, x, mod3, wsg, wsu, wsd]
    if final:
        in_specs.append(full((1, D)))
        args.append(fn.reshape(1, D))
    in_specs.append(pl.BlockSpec(memory_space=pl.ANY))
    args.append(ys)
    out_shape = [jax.ShapeDtypeStruct((t, D), F32)]
    out_specs = [row]
    if final:
        out_shape.append(jax.ShapeDtypeStruct((t, D), F32))
        out_specs.append(row)
    return pl.pallas_call(
        functools.partial(_combine_kernel, tm=tm, final=final),
        out_shape=tuple(out_shape),
        grid=(t // tm,),
        in_specs=in_specs,
        out_specs=tuple(out_specs),
        scratch_shapes=[pltpu.VMEM((TOP_K, tm, ROW_T, LANE), U32), pltpu.SemaphoreType.DMA(())],
        compiler_params=_cparams(("arbitrary",)),
    )(*args)


_W_IN_CUTS = np.cumsum([0, 256, 160, 1536, 512, 16, 2048, 1536, 4096])


def _pack_w_in_kernel(w_ref, wa_ref, wb_ref):
    seg = lambda i: w_ref[:, int(_W_IN_CUTS[i]):int(_W_IN_CUTS[i + 1])]
    pad = jnp.zeros((w_ref.shape[0], 2 * LANE - 160 - 16), F32)
    wa_ref[...] = jnp.concatenate([seg(0), seg(1), seg(4), pad], axis=1).astype(BF16)
    wb_ref[...] = jnp.concatenate([seg(7), seg(2), seg(3), seg(5), seg(6)], axis=1).astype(BF16)


def _pack_w_in(w):
    rows = 128
    return pl.pallas_call(
        _pack_w_in_kernel,
        out_shape=(jax.ShapeDtypeStruct((D, N_PROJ_A), BF16), jax.ShapeDtypeStruct((D, N_PROJ_B), BF16)),
        grid=(D // rows,),
        in_specs=[pl.BlockSpec((rows, w.shape[1]), lambda i: (i, 0))],
        out_specs=(pl.BlockSpec((rows, N_PROJ_A), lambda i: (i, 0)),
                   pl.BlockSpec((rows, N_PROJ_B), lambda i: (i, 0))),
        compiler_params=_cparams(("arbitrary",)),
    )(w)


def _pack_w_uq(w):
    w = w.reshape(MLA_QR, MLA_H, MLA_NOPE + MLA_ROPE)
    w = jnp.pad(w, ((0, 0), (0, 0), (0, LANE - MLA_NOPE - MLA_ROPE)))
    return w.reshape(MLA_QR, MLA_H * LANE).astype(BF16)


def _pack_w_ukv(w):
    w = w.reshape(MLA_KVR, MLA_H, MLA_NOPE + MLA_V)
    k_nope = jnp.pad(w[:, :, :MLA_NOPE], ((0, 0), (0, 0), (0, LANE - MLA_NOPE)))
    eye = jnp.eye(MLA_ROPE, dtype=w.dtype)[:, None, :]
    k_pe = jnp.pad(jnp.broadcast_to(eye, (MLA_ROPE, MLA_H, MLA_ROPE)),
                   ((0, LANE - MLA_ROPE), (0, 0), (MLA_NOPE, LANE - MLA_NOPE - MLA_ROPE)))
    wk = jnp.concatenate([k_nope, k_pe], axis=0).reshape(2 * LANE, MLA_H * LANE)
    wv = w[:, :, MLA_NOPE:].reshape(MLA_KVR, MLA_H * MLA_V)
    return wk.astype(BF16), wv.astype(BF16)


def _axial_angles(n_tok, dim):
    nf = dim // 4
    inv = ROPE_BASE ** (-jnp.arange(nf, dtype=F32) / nf)
    r = jnp.repeat(jnp.arange(n_tok // GRID_W, dtype=F32), GRID_W)
    cc = jnp.tile(jnp.arange(GRID_W, dtype=F32), n_tok // GRID_W)
    ang = jnp.concatenate([r[:, None] * inv, cc[:, None] * inv], axis=-1)
    return jnp.cos(ang), jnp.sin(ang)


def _rope_tables_mla(n_tok, lane0):
    cos, sin = _axial_angles(n_tok, MLA_ROPE)
    half = MLA_ROPE // 2
    z = lambda w: jnp.zeros((n_tok, w), F32)
    o = lambda w: jnp.ones((n_tok, w), F32)
    rest = LANE - lane0 - MLA_ROPE
    c = jnp.concatenate([o(lane0), cos, cos, o(rest)], axis=1)
    s1 = jnp.concatenate([z(lane0), -sin, z(half), z(rest)], axis=1)
    s2 = jnp.concatenate([z(lane0), z(half), sin, z(rest)], axis=1)
    return c, s1, s2


def _rope_tables_ret(n_tok):
    cos, sin = _axial_angles(n_tok, RET_DK)
    return jnp.concatenate([cos, cos], axis=1), jnp.concatenate([-sin, sin], axis=1)


def _bcast_dh(a):
    return jnp.broadcast_to(a.astype(F32)[:, :, None, None], a.shape + (8, LANE))


def _forward(x_prompt, x_sample, c, cache_ckv, cache_kpe, state_dn, state_ret, c_ctx, P, tiles):
    nb, sc, _ = x_prompt.shape
    nl, sl, _ = x_sample.shape
    past = cache_ckv.shape[2]
    depth = P['w_in'].shape[0]
    t_ctx, t_lat = nb * sc, nl * sl
    t = t_ctx + t_lat
    tm = tiles['tm']
    assert sc % tm == 0 and sl % tm == 0 and t_ctx % sl == 0

    assert sl % tiles['tc'] == 0 and t_ctx % tiles['tc'] == 0

    def grp_of_tile(rows_per_tile):
        nct = t_ctx // rows_per_tile
        return lambda i: jnp.where(i < nct, 0, 1 + (i - nct) // (sl // rows_per_tile))

    x = jnp.concatenate([x_prompt.reshape(t_ctx, D), x_sample.reshape(t_lat, D)], axis=0)
    ngrp = 1 + nl
    cvec = jnp.concatenate([c_ctx[None], c, jnp.zeros((-(ngrp) % 8, D), F32)], axis=0)

    tab_q = _rope_tables_mla(sl, MLA_NOPE)
    tab_k = _rope_tables_mla(sl, 0)
    tab_r = _rope_tables_ret(sl)

    ckv_l, kpe_l, dn_l, ret_l = [], [], [], []
    y_final = None
    for l in range(depth):
        mod3 = _ada(cvec, P['w_ada'][l], P['b_ada'][l]).reshape(cvec.shape[0], 1, 6 * D)
        w_a, w_b = _pack_w_in(P['w_in'][l])
        tp = tiles['tp']
        proj_a = _in_proj(x, mod3, P['norm_mix'][l], w_a, grp_of_tile, tp, N_PROJ_A, F32)
        proj = _in_proj(x, mod3, P['norm_mix'][l], w_b, grp_of_tile, tp, tiles['tn'], BF16)

        w_uq = _pack_w_uq(P['mla_w_uq'][l])
        wk, wv = _pack_w_ukv(P['mla_w_ukv'][l])
        gq, gkv = P['mla_q_norm'][l], P['mla_kv_norm'][l]
        q_c = _mla_q(proj_a, gq, w_uq, 0, t_ctx, tm, None, sc)
        q_l = _mla_q(proj_a, gq, w_uq, t_ctx, t_lat, tm, tab_q, sl)
        kvb = COL_KV // (2 * LANE)
        ckv_c, kpe_c, k_c, v_c = _mla_kv(proj_a, kvb, gkv, wk, wv, 0, t_ctx, tm, True, None, sc)
        _, _, k_l, v_l = _mla_kv(proj_a, kvb, gkv, wk, wv, t_ctx, t_lat, tm, True, tab_k, sl)
        cached = jnp.concatenate([cache_ckv[:, l], cache_kpe[:, l],
                                  jnp.zeros((nl, past, 2 * LANE - MLA_KVR - MLA_ROPE), F32)], axis=-1)
        pt = min(tm, past)
        _, _, k_p, v_p = _mla_kv(cached.reshape(nl * past, 2 * LANE), 0, gkv, wk, wv, 0, nl * past, pt,
                                 False, None, past)
        r3 = lambda a, n, s: a.reshape(n, s, a.shape[-1])
        y_mla = _attn(r3(q_c, nb, sc), [r3(k_c, nb, sc)], [r3(v_c, nb, sc)], min(sc, tiles['tq']), t, 0, None)
        y_mla = _attn(r3(q_l, nl, sl), [r3(k_p, nl, past), r3(k_l, nl, sl)],
                      [r3(v_p, nl, past), r3(v_l, nl, sl)], min(sl, tiles['tq']), t, t_ctx, y_mla)

        alog_b, dtb_b = _bcast_dh(P['dn_A_log'][l]), _bcast_dh(P['dn_dt_bias'][l])
        dn_args = (proj_a, proj, P['dn_conv'][l], alog_b, dtb_b, P['dn_norm'][l])
        y_dn, dn_fin = _deltanet(*dn_args, 0, nb, sc, None, l, t, None)
        y_dn, _ = _deltanet(*dn_args, t_ctx, nl, sl, state_dn, l, t, y_dn)

        dec_b = _bcast_dh(P['ret_decay'][l])
        y_ret, ret_fin = _retention(proj, dec_b, P['ret_gn'][l], 0, nb, sc, None, None, l, t, None)
        y_ret, _ = _retention(proj, dec_b, P['ret_gn'][l], t_ctx, nl, sl, tab_r, state_ret, l, t, y_ret)

        y_sc = _sconv(proj, P['sc_conv'][l], 0, nb, sc, t, None)
        y_sc = _sconv(proj, P['sc_conv'][l], t_ctx, nl, sl, t, y_sc)

        x_mid, h2, h2p = _branch_out((y_mla, y_dn, y_ret, y_sc), proj, x, mod3, P['norm_ffn'][l],
                                P['w_branch'][l].astype(BF16), P['w_out'][l].astype(BF16), grp_of_tile,
                                tiles['tb'])

        bias_b = jnp.broadcast_to(P['router_bias'][l].astype(F32)[:, None], (N_EXP, LANE))
        idx_t, w_t, rank_t, cnt = _route(h2, P['router'][l].T, bias_b, tiles['tr'])
        counts = cnt[:, 0]
        padded = (counts + MOE_BM - 1) // MOE_BM * MOE_BM
        pad_end = jnp.cumsum(padded)
        pstart = (pad_end - padded).astype(I32)
        n_blocks = (t * TOP_K) // MOE_BM + N_EXP
        blk_row0 = jnp.arange(n_blocks, dtype=I32) * MOE_BM
        block_e = jnp.minimum(jnp.sum((pad_end[None, :] <= blk_row0[:, None]).astype(I32), axis=1), N_EXP - 1)
        n_used = (pad_end[-1:] // MOE_BM).astype(I32)
        pos_t = _slot_pos(pstart, idx_t, rank_t, tiles['tr'])
        xs = _dispatch(pstart, padded.astype(I32), pos_t, h2p, n_blocks * MOE_BM, tiles['td'])
        ys = _experts(block_e, n_used, xs, P['w_eg'][l], P['w_eu'][l], P['w_ed'][l])
        fn = P['final_norm'] if l == depth - 1 else None
        outs = _combine(pos_t, w_t.T, h2, x_mid, mod3, P['w_sg'][l].astype(BF16),
                        P['w_su'][l].astype(BF16), P['w_sd'][l].astype(BF16), fn, ys, grp_of_tile, tiles['tc'])
        x = outs[0]
        if fn is not None:
            y_final = outs[1]

        ckv_l.append(ckv_c.reshape(nb, sc, MLA_KVR))
        kpe_l.append(kpe_c.reshape(nb, sc, MLA_ROPE))
        dn_l.append(dn_fin)
        ret_l.append(ret_fin)

    y_prompt = y_final[:t_ctx].reshape(nb, sc, D)
    y_sample = y_final[t_ctx:].reshape(nl, sl, D)
    return (y_prompt, y_sample, jnp.stack(ckv_l, axis=1), jnp.stack(kpe_l, axis=1),
            jnp.stack(dn_l, axis=1), jnp.stack(ret_l, axis=1))


_TILES = dict(tm=256, tp=1024, tn=2432, tb=512, tq=256, tr=512, td=256, tc=128)


def kernel(x_prompt, x_sample, c, cache_ckv, cache_kpe, state_dn, state_ret, c_ctx, w_ada, b_ada, norm_mix, norm_ffn, w_in, mla_q_norm, mla_w_uq, mla_kv_norm, mla_w_ukv, dn_conv, dn_A_log, dn_dt_bias, dn_norm, ret_decay, ret_gn, sc_conv, w_branch, w_out, router, router_bias, w_eg, w_eu, w_ed, w_sg, w_su, w_sd, final_norm):
    P = dict(w_ada=w_ada, b_ada=b_ada, norm_mix=norm_mix, norm_ffn=norm_ffn, w_in=w_in,
             mla_q_norm=mla_q_norm, mla_w_uq=mla_w_uq, mla_kv_norm=mla_kv_norm, mla_w_ukv=mla_w_ukv,
             dn_conv=dn_conv, dn_A_log=dn_A_log, dn_dt_bias=dn_dt_bias, dn_norm=dn_norm,
             ret_decay=ret_decay, ret_gn=ret_gn, sc_conv=sc_conv, w_branch=w_branch, w_out=w_out,
             router=router, router_bias=router_bias, w_eg=w_eg, w_eu=w_eu, w_ed=w_ed,
             w_sg=w_sg, w_su=w_su, w_sd=w_sd, final_norm=final_norm)
    return _forward(x_prompt, x_sample, c, cache_ckv, cache_kpe, state_dn, state_ret, c_ctx, P, _TILES)
```

```python
import functools
import math

import jax
import jax.numpy as jnp
import numpy as np
from jax import lax
from jax.experimental import pallas as pl
from jax.experimental.pallas import tpu as pltpu

F32 = jnp.float32
BF16 = jnp.bfloat16
I32 = jnp.int32
U32 = jnp.uint32

D = 1024
EPS = 1e-6
CHUNK = 64
PAIR = 2 * CHUNK
GRID_W = 64
ROPE_BASE = 10000.0

MLA_H, MLA_NOPE, MLA_ROPE, MLA_V, MLA_QR, MLA_KVR = 8, 64, 32, 64, 256, 128
DN_H, DN_DK = 4, 128
RET_H, RET_DK = 4, 128
MIX_W = 512
N_EXP, TOP_K, N_GRP, TOPK_GRP, D_EXP = 64, 8, 8, 4, 256
ROUTED_SCALE = 2.5

LANE = 128
COL_CQ = 0
COL_KV = 256
N_PROJ_A = 512
AB_LANE = 160
COL_GATE = 0
COL_DNQKV = 4096
COL_DNZ = 5632
COL_RET = 6144
COL_SC = 8192
N_PROJ_B = 9728

MOE_BM = 512
VMEM_LIMIT = 56 * 1024 * 1024


def _cparams(sem, vmem=None):
    return pltpu.CompilerParams(dimension_semantics=sem, vmem_limit_bytes=vmem or VMEM_LIMIT)


def _dot(a, b):
    return jnp.dot(a, b, preferred_element_type=F32)


def _dot_nt(a, b):
    return lax.dot_general(a, b, (((1,), (1,)), ((), ())), preferred_element_type=F32)


def _dot_tn(a, b):
    return lax.dot_general(a, b, (((0,), (0,)), ((), ())), preferred_element_type=F32)


def _split(a):
    hi = a.astype(BF16)
    lo = (a - hi.astype(F32)).astype(BF16)
    return hi, lo


def _dot3(a, b):
    ah, al = _split(a)
    bh, bl = _split(b)
    return _dot(ah, bh) + (_dot(ah, bl) + _dot(al, bh))


def _dot3s(a, b):
    (ah, al), (bh, bl) = a, b
    return _dot(jnp.concatenate([ah, ah, al], axis=1), jnp.concatenate([bh, bl, bh], axis=0))


def _dot3_nt(a, b):
    ah, al = _split(a)
    bh, bl = _split(b)
    return _dot_nt(ah, bh) + (_dot_nt(ah, bl) + _dot_nt(al, bh))


def _sigmoid(x):
    return 1.0 / (1.0 + jnp.exp(-x))


def _silu(x):
    return x * _sigmoid(x)


def _rms(x, g):
    return x * lax.rsqrt(jnp.mean(x * x, axis=-1, keepdims=True) + EPS) * g


def _ada_kernel(c_ref, w_ref, b_ref, o_ref):
    o_ref[...] = _dot3(_silu(c_ref[...]), w_ref[0]) + b_ref[...]


def _ada(cvec, w_all, b, layer):
    n = w_all.shape[2]
    tn = 1024
    return pl.pallas_call(
        _ada_kernel,
        out_shape=jax.ShapeDtypeStruct((cvec.shape[0], n), F32),
        grid=(n // tn,),
        in_specs=[pl.BlockSpec(cvec.shape, lambda j: (0, 0)),
                  pl.BlockSpec((1, D, tn), lambda j: (layer, 0, j)),
                  pl.BlockSpec((1, tn), lambda j: (0, j))],
        out_specs=pl.BlockSpec((cvec.shape[0], tn), lambda j: (0, j)),
        compiler_params=_cparams(("arbitrary",)),
    )(cvec, w_all, b.reshape(1, n))


def _in_proj_kernel(x_ref, mod_ref, g_ref, w_ref, o_ref, h_scr):
    @pl.when(pl.program_id(1) == 0)
    def _():
        m = mod_ref[0]
        y = _rms(x_ref[...], g_ref[...])
        h_scr[...] = (y * (1.0 + m[:, D:2 * D]) + m[:, 0:D]).astype(BF16)

    o_ref[...] = _dot(h_scr[...], w_ref[...]).astype(o_ref.dtype)


def _in_proj(x, mod3, g, w, grp_of_tile, tm, tn, out_dtype):
    t = x.shape[0]
    n = w.shape[1]
    return pl.pallas_call(
        _in_proj_kernel,
        out_shape=jax.ShapeDtypeStruct((t, n), out_dtype),
        grid=(t // tm, n // tn),
        in_specs=[pl.BlockSpec((tm, D), lambda i, j: (i, 0)),
                  pl.BlockSpec((1, 1, 6 * D), lambda i, j: (grp_of_tile(tm)(i), 0, 0)),
                  pl.BlockSpec((1, D), lambda i, j: (0, 0)),
                  pl.BlockSpec((D, tn), lambda i, j: (0, j))],
        out_specs=pl.BlockSpec((tm, tn), lambda i, j: (i, j)),
        scratch_shapes=[pltpu.VMEM((tm, D), BF16)],
        compiler_params=_cparams(("arbitrary", "arbitrary")),
    )(x, mod3, g.reshape(1, D), w)


def _rope3(x, c, s1, s2, width):
    return x * c + pltpu.roll(x, width - 16, 1) * s1 + pltpu.roll(x, 16, 1) * s2


def _mla_q_kernel(*refs, rope):
    if rope:
        p_ref, g_ref, w_ref, c_ref, s1_ref, s2_ref, o_ref = refs
    else:
        p_ref, g_ref, w_ref, o_ref = refs
    y = _rms(p_ref[...], g_ref[...])
    q = _dot(y.astype(BF16), w_ref[...])
    if rope:
        tile = lambda r: jnp.concatenate([r[...]] * MLA_H, axis=1)
        q = _rope3(q, tile(c_ref), tile(s1_ref), tile(s2_ref), MLA_H * LANE)
    o_ref[...] = q.astype(BF16)


def _mla_q(proj, g, w, row_off, rows, tm, rope_tabs, seq):
    nrow = rows // tm
    r0 = row_off // tm
    in_specs = [pl.BlockSpec((tm, MLA_QR), lambda i: (r0 + i, COL_CQ // MLA_QR)),
                pl.BlockSpec((1, MLA_QR), lambda i: (0, 0)),
                pl.BlockSpec((MLA_QR, MLA_H * LANE), lambda i: (0, 0))]
    args = [proj, g.reshape(1, MLA_QR), w]
    if rope_tabs is not None:
        per = seq // tm
        in_specs += [pl.BlockSpec((tm, LANE), lambda i: (i % per, 0))] * 3
        args += list(rope_tabs)
    return pl.pallas_call(
        functools.partial(_mla_q_kernel, rope=rope_tabs is not None),
        out_shape=jax.ShapeDtypeStruct((rows, MLA_H * LANE), BF16),
        grid=(nrow,),
        in_specs=in_specs,
        out_specs=pl.BlockSpec((tm, MLA_H * LANE), lambda i: (i, 0)),
        compiler_params=_cparams(("arbitrary",)),
    )(*args)


def _mla_kv_kernel(*refs, norm, rope):
    if rope:
        p_ref, g_ref, wk_ref, wv_ref, c_ref, s1_ref, s2_ref, ckv_ref, kpe_ref, k_ref, v_ref = refs
    else:
        p_ref, g_ref, wk_ref, wv_ref, ckv_ref, kpe_ref, k_ref, v_ref = refs
    blk = p_ref[...]
    ckv = blk[:, :MLA_KVR]
    if norm:
        ckv = _rms(ckv, g_ref[...])
    kp = blk[:, MLA_KVR:]
    ckv_ref[...] = ckv
    kpe_ref[...] = kp[:, :MLA_ROPE]
    if rope:
        kp = _rope3(kp, c_ref[...], s1_ref[...], s2_ref[...], LANE)
    a = jnp.concatenate([ckv, kp], axis=1).astype(BF16)
    k_ref[...] = _dot(a, wk_ref[...]).astype(BF16)
    v_ref[...] = _dot(ckv.astype(BF16), wv_ref[...]).astype(BF16)


def _mla_kv(src, col_blk, g, wk, wv, row_off, rows, tm, norm, rope_tabs, seq):
    nrow = rows // tm
    r0 = row_off // tm
    in_specs = [pl.BlockSpec((tm, 2 * LANE), lambda i: (r0 + i, col_blk)),
                pl.BlockSpec((1, MLA_KVR), lambda i: (0, 0)),
                pl.BlockSpec((2 * LANE, MLA_H * LANE), lambda i: (0, 0)),
                pl.BlockSpec((MLA_KVR, MLA_H * MLA_V), lambda i: (0, 0))]
    args = [src, g.reshape(1, MLA_KVR), wk, wv]
    if rope_tabs is not None:
        per = seq // tm
        in_specs += [pl.BlockSpec((tm, LANE), lambda i: (i % per, 0))] * 3
        args += list(rope_tabs)
    return pl.pallas_call(
        functools.partial(_mla_kv_kernel, norm=norm, rope=rope_tabs is not None),
        out_shape=(jax.ShapeDtypeStruct((rows, MLA_KVR), F32),
                   jax.ShapeDtypeStruct((rows, MLA_ROPE), F32),
                   jax.ShapeDtypeStruct((rows, MLA_H * LANE), BF16),
                   jax.ShapeDtypeStruct((rows, MLA_H * MLA_V), BF16)),
        grid=(nrow,),
        in_specs=in_specs,
        out_specs=(pl.BlockSpec((tm, MLA_KVR), lambda i: (i, 0)),
                   pl.BlockSpec((tm, MLA_ROPE), lambda i: (i, 0)),
                   pl.BlockSpec((tm, MLA_H * LANE), lambda i: (i, 0)),
                   pl.BlockSpec((tm, MLA_H * MLA_V), lambda i: (i, 0))),
        compiler_params=_cparams(("arbitrary",)),
    )(*args)


def _attn_kernel(*refs, nseg):
    q_ref = refs[0]
    k_refs = refs[1:1 + nseg]
    v_refs = refs[1 + nseg:1 + 2 * nseg]
    o_ref = refs[1 + 2 * nseg]
    scale = (MLA_NOPE + MLA_ROPE) ** -0.5
    def scores(h):
        qh = q_ref[0, :, h * LANE:(h + 1) * LANE]
        return [_dot_nt(qh, k_ref[0, :, h * LANE:(h + 1) * LANE]) * scale for k_ref in k_refs]

    outs = []
    nxt = scores(0)
    for h in range(MLA_H):
        ss, nxt = nxt, (scores(h + 1) if h + 1 < MLA_H else None)
        m = functools.reduce(jnp.maximum, [jnp.max(s, axis=-1, keepdims=True) for s in ss])
        es = [jnp.exp(s - m) for s in ss]
        inv = 1.0 / functools.reduce(jnp.add, [jnp.sum(e, axis=-1, keepdims=True) for e in es])
        o = functools.reduce(jnp.add, [
            _dot((e * inv).astype(BF16), v_ref[0, :, h * MLA_V:(h + 1) * MLA_V])
            for e, v_ref in zip(es, v_refs)])
        outs.append(o)
    o_ref[...] = jnp.concatenate(outs, axis=1)


def _into(kernel, in_specs, args, prev):
    if prev is None:
        return kernel, {}
    pos = len(args)
    in_specs.append(pl.BlockSpec(memory_space=pl.ANY))
    args.append(prev)
    return (lambda *refs: kernel(*refs[:pos], *refs[pos + 1:])), {pos: 0}


def _attn(q, ks, vs, tq, t_total, row_off, prev):
    b, s, _ = q.shape
    nseg = len(ks)
    in_specs = [pl.BlockSpec((1, tq, MLA_H * LANE), lambda i, j: (i, j, 0))]
    in_specs += [pl.BlockSpec((1,) + k.shape[1:], lambda i, j: (i, 0, 0)) for k in ks]
    in_specs += [pl.BlockSpec((1,) + v.shape[1:], lambda i, j: (i, 0, 0)) for v in vs]
    args = [q, *ks, *vs]
    kern, alias = _into(functools.partial(_attn_kernel, nseg=nseg), in_specs, args, prev)
    r0, per = row_off // tq, s // tq
    return pl.pallas_call(
        kern,
        out_shape=jax.ShapeDtypeStruct((t_total, MLA_H * MLA_V), F32),
        grid=(b, per),
        in_specs=in_specs,
        out_specs=pl.BlockSpec((tq, MLA_H * MLA_V), lambda i, j: (r0 + i * per + j, 0)),
        input_output_aliases=alias,
        compiler_params=_cparams(("arbitrary", "arbitrary")),
    )(*args)


def _conv3(x, w):
    s = x.shape[0]
    row = lax.broadcasted_iota(I32, x.shape, 0)
    prev = jnp.where(row == 0, 0.0, pltpu.roll(x, 1, 0))
    nxt = jnp.where(row == s - 1, 0.0, pltpu.roll(x, s - 1, 0))
    return prev * w[0:1] + x * w[1:2] + nxt * w[2:3]


def _sconv_kernel(b_ref, c_ref, x_ref, w_ref, o_ref):
    f = lambda r: r[...].astype(F32)
    o_ref[...] = f(b_ref) * _conv3(f(c_ref) * f(x_ref), w_ref[...])


def _sconv(proj, w, row_off, nseq, seq, t_total, prev):
    r0 = row_off // seq
    cb = COL_SC // MIX_W
    spec = lambda off: pl.BlockSpec((seq, MIX_W), lambda b: (r0 + b, cb + off))
    in_specs = [spec(0), spec(1), spec(2), pl.BlockSpec((3, MIX_W), lambda b: (0, 0))]
    args = [proj, proj, proj, w]
    kern, alias = _into(_sconv_kernel, in_specs, args, prev)
    return pl.pallas_call(
        kern,
        out_shape=jax.ShapeDtypeStruct((t_total, MIX_W), F32),
        grid=(nseq,),
        in_specs=in_specs,
        out_specs=pl.BlockSpec((seq, MIX_W), lambda b: (r0 + b, 0)),
        input_output_aliases=alias,
        compiler_params=_cparams(("arbitrary",)),
    )(*args)


def _ret_kernel(*refs, seq, rope, has_s0):
    it = iter(refs)
    q_ref, k_ref, v_ref, g_ref, dec_ref, gn_ref = (next(it) for _ in range(6))
    if rope:
        c_ref, s_ref = next(it), next(it)
    if has_s0:
        s0_ref = next(it)
    y_ref, sfin_ref, of_scr, ob_scr = (next(it) for _ in range(4))
    nch = seq // CHUNK

    q = q_ref[...].astype(F32)
    k = k_ref[...].astype(F32) * (RET_DK ** -0.5)
    if rope:
        c, sn = c_ref[...], s_ref[...]
        q = q * c + pltpu.roll(q, RET_DK // 2, 1) * sn
        k = k * c + pltpu.roll(k, RET_DK // 2, 1) * sn
    v = v_ref[...].astype(F32)

    ti = lax.broadcasted_iota(I32, (CHUNK, CHUNK), 0)
    si = lax.broadcasted_iota(I32, (CHUNK, CHUNK), 1)
    pos = lax.broadcasted_iota(I32, (CHUNK, 1), 0).astype(F32)
    sls = [slice(n * CHUNK, (n + 1) * CHUNK) for n in range(nch)]
    qb, kb, vb = q.astype(BF16), k.astype(BF16), v.astype(BF16)
    qk = [_dot_nt(qb[sl], kb[sl]) for sl in sls]
    dirs = []
    for d in range(2):
        lg = -jnp.exp(dec_ref[d, 0, 0:1, 0:1])
        if d == 0:
            dist = (ti - si).astype(F32)
            qpow, kpow = pos + 1.0, (CHUNK - 1.0) - pos
        else:
            dist = (si - ti).astype(F32)
            qpow, kpow = CHUNK - pos, pos
        decay = jnp.where(dist >= 0, jnp.exp(jnp.maximum(dist, 0.0) * lg), 0.0)
        qs = jnp.exp(qpow * lg)
        ks = jnp.exp(kpow * lg)
        o_intra = [_dot((qk[n] * decay).astype(BF16), vb[sls[n]]) for n in range(nch)]
        kv = [_dot_tn((k[sls[n]] * ks).astype(BF16), vb[sls[n]]) for n in range(nch)]
        dirs.append((qs, jnp.exp(CHUNK * lg), o_intra, kv))
    for d in range(2):
        qs, gtot, o_intra, kv = dirs[d]
        st = s0_ref[0, 0, d, 0] if has_s0 else jnp.zeros((RET_DK, RET_DK), F32)
        o_scr = of_scr if d == 0 else ob_scr
        order = range(nch) if d == 0 else range(nch - 1, -1, -1)
        for n in order:
            o_scr[sls[n], :] = o_intra[n] + _dot((q[sls[n]] * qs).astype(BF16), st.astype(BF16))
            st = gtot * st + kv[n]
        sfin_ref[0, d, 0] = st

    o = of_scr[...] + ob_scr[...]
    dlt = o - jnp.mean(o, axis=-1, keepdims=True)
    y = dlt * lax.rsqrt(jnp.mean(dlt * dlt, axis=-1, keepdims=True) + EPS) * gn_ref[...]
    y_ref[...] = _silu(g_ref[...].astype(F32)) * y


def _retention(proj, dec_b, gn, row_off, nseq, seq, rope_tabs, s0, layer, t_total, prev):
    r0 = row_off // seq
    cb = COL_RET // LANE
    spec = lambda off: pl.BlockSpec((seq, LANE), lambda b, h: (r0 + b, cb + off + h))
    in_specs = [spec(0), spec(RET_H), spec(2 * RET_H), spec(3 * RET_H),
                pl.BlockSpec((2, 1, 8, LANE), lambda b, h: (0, h, 0, 0)),
                pl.BlockSpec((1, LANE), lambda b, h: (0, h))]
    args = [proj, proj, proj, proj, dec_b, gn.reshape(1, RET_H * RET_DK)]
    if rope_tabs is not None:
        in_specs += [pl.BlockSpec((seq, LANE), lambda b, h: (0, 0))] * 2
        args += list(rope_tabs)
    if s0 is not None:
        in_specs.append(pl.BlockSpec((1, 1, 2, 1, RET_DK, RET_DK), lambda b, h: (b, layer, 0, h, 0, 0)))
        args.append(s0)
    kern, alias = _into(functools.partial(_ret_kernel, seq=seq, rope=rope_tabs is not None, has_s0=s0 is not None),
                        in_specs, args, prev)
    return pl.pallas_call(
        kern,
        out_shape=(jax.ShapeDtypeStruct((t_total, RET_H * RET_DK), F32),
                   jax.ShapeDtypeStruct((nseq, 2, RET_H, RET_DK, RET_DK), F32)),
        grid=(nseq, RET_H),
        in_specs=in_specs,
        out_specs=(pl.BlockSpec((seq, LANE), lambda b, h: (r0 + b, h)),
                   pl.BlockSpec((1, 2, 1, RET_DK, RET_DK), lambda b, h: (b, 0, h, 0, 0))),
        scratch_shapes=[pltpu.VMEM((seq, RET_DK), F32), pltpu.VMEM((seq, RET_DK), F32)],
        input_output_aliases=alias,
        compiler_params=_cparams(("arbitrary", "arbitrary")),
    )(*args)


def _softplus(x):
    return jnp.maximum(x, 0.0) + jnp.log1p(jnp.exp(-jnp.abs(x)))


def _l2n(x):
    return x * lax.rsqrt(jnp.sum(x * x, axis=-1, keepdims=True) + EPS)


def _dn_kernel(*refs, seq, has_s0):
    it = iter(refs)
    q_ref, k_ref, v_ref, z_ref, ab_ref, wq_ref, wk_ref, wv_ref, alog_ref, dtb_ref, nrm_ref = (
        next(it) for _ in range(11))
    if has_s0:
        s0_ref = next(it)
    y_ref, sfin_ref = next(it), next(it)
    of_scr, ob_scr, q_scr, k_scr, v_scr, qe_scr, oc_scr, m_scr, c_scr, gt_scr = (next(it) for _ in range(10))
    nch = seq // CHUNK
    npair = seq // PAIR
    hcols = lambda hh: slice(hh * DN_DK, (hh + 1) * DN_DK)

    xq = _silu(_conv3(q_ref[...].astype(F32), wq_ref[...]))
    xk = _silu(_conv3(k_ref[...].astype(F32), wk_ref[...]))
    v_scr[...] = _silu(_conv3(v_ref[...].astype(F32), wv_ref[...]))
    for hh in range(DN_H):
        q_scr[:, hcols(hh)] = _l2n(xq[:, hcols(hh)]) * (DN_DK ** -0.5)
        k_scr[:, hcols(hh)] = _l2n(xk[:, hcols(hh)])

    ti = lax.broadcasted_iota(I32, (PAIR, PAIR), 0)
    si = lax.broadcasted_iota(I32, (PAIR, PAIR), 1)
    same = (ti >> 6) == (si >> 6)
    s_in = si & (CHUNK - 1)
    eye = ti == si
    eye_f = eye.astype(F32)
    lane = lax.broadcasted_iota(I32, (1, 2 * LANE), 1)

    def ab_col(blk, idx):
        return jnp.sum(jnp.where(lane == AB_LANE + idx, blk, 0.0), axis=1, keepdims=True)

    masks = []
    for d in range(2):
        masks.append((same & ((si <= ti) if d == 0 else (si >= ti)),
                      same & ((si < ti) if d == 0 else (si > ti)),
                      same & ((ti <= si) if d == 0 else (ti >= si)),
                      same & (s_in == (CHUNK - 1 if d == 0 else 0))))

    def prep(p, carry):
        rows = pl.ds(pl.multiple_of(p * PAIR, PAIR), PAIR)
        blk = ab_ref[rows, :]
        chains = []
        for hh in range(DN_H):
            cs = hcols(hh)
            qn, kn, vn = q_scr[rows, cs], k_scr[rows, cs], v_scr[rows, cs]
            kb = kn.astype(BF16)
            kk = _dot_nt(kb, kb)
            qkr = _dot_nt(qn.astype(BF16), kb)
            for d in range(2):
                incl, strict, incl_t, last_s = masks[d]
                neg_a = -jnp.exp(alog_ref[d, hh, 0:1, 0:1])
                dtb = dtb_ref[d, hh, 0:1, 0:1]
                la = neg_a * _softplus(ab_col(blk, d * DN_H + hh) + dtb)
                beta = _sigmoid(ab_col(blk, 2 * DN_H + d * DN_H + hh))
                g_row = jnp.sum(jnp.where(incl_t, la, 0.0), axis=0, keepdims=True)
                g_col = jnp.sum(jnp.where(eye, g_row, 0.0), axis=1, keepdims=True)
                g_end = jnp.sum(jnp.where(last_s, g_row, 0.0), axis=1, keepdims=True)
                decay = jnp.where(incl, jnp.exp(jnp.where(incl, g_col - g_row, 0.0)), 0.0)
                nmat = jnp.where(strict, -(beta * decay * kk), 0.0)
                eg = jnp.exp(g_col)
                chains.append(dict(
                    hh=hh, d=d, cs=cs, g_row=g_row, tinv=eye_f + nmat, pw=_split(nmat),
                    bv=_split(beta * vn), bk=_split((beta * eg) * kn), egq=eg * qn,
                    qk=(qkr * decay).astype(BF16), kd=(jnp.exp(g_end - g_col) * kn).astype(BF16)))
        for it in range(5):
            fine = it < 2
            sq = [_dot3s(c['pw'], c['pw']) if fine else _dot(c['pw'][0], c['pw'][0]) for c in chains]
            for c, s in zip(chains, sq):
                c['pw'] = _split(s)
            up = [_dot3s(_split(c['tinv']), c['pw']) if fine else _dot(c['tinv'].astype(BF16), c['pw'][0])
                  for c in chains]
            for c, u in zip(chains, up):
                c['tinv'] = c['tinv'] + u
        for c in chains:
            c['ts'] = _split(c['tinv'])
        for c in chains:
            c['uvb'] = _dot3s(c['ts'], c['bv']).astype(BF16)
        for c in chains:
            c['wkb'] = _dot3s(c['ts'], c['bk']).astype(BF16)
        for c in chains:
            qe_scr[c['d'], rows, c['cs']] = (c['egq'] - _dot(c['qk'], c['wkb'])).astype(BF16)
            oc_scr[c['d'], rows, c['cs']] = _dot(c['qk'], c['uvb'])
        for c in chains:
            d, j = c['d'], c['d'] * DN_H + c['hh']
            for half in range(2):
                sl = slice(half * CHUNK, (half + 1) * CHUNK)
                n = p * 2 + half
                mrows = pl.ds(pl.multiple_of(n * DN_DK, DN_DK), DN_DK)
                m_scr[j, mrows, :] = _dot_tn(c['kd'][sl], c['wkb'][sl]).astype(BF16)
                c_scr[j, mrows, :] = _dot_tn(c['kd'][sl], c['uvb'][sl])
                e = half * CHUNK + (CHUNK - 1 if d == 0 else 0)
                gt_scr[j, pl.ds(pl.multiple_of(n * 8, 8), 8), :] = jnp.broadcast_to(
                    jnp.exp(c['g_row'][:, e:e + 1]), (8, LANE))
        return carry

    lax.fori_loop(0, npair, prep, 0)

    for d in range(2):
        for hh in range(DN_H):
            sfin_ref[0, d, hh] = s0_ref[0, 0, d, hh] if has_s0 else jnp.zeros((DN_DK, DN_DK), F32)

    def step(i, carry):
        for d in range(2):
            n = i if d == 0 else nch - 1 - i
            rows = pl.ds(pl.multiple_of(n * CHUNK, CHUNK), CHUNK)
            mrows = pl.ds(pl.multiple_of(n * DN_DK, DN_DK), DN_DK)
            o_scr = of_scr if d == 0 else ob_scr
            for hh in range(DN_H):
                st = sfin_ref[0, d, hh]
                sb = st.astype(BF16)
                o_scr[rows, hcols(hh)] = _dot(qe_scr[d, rows, hcols(hh)], sb) + oc_scr[d, rows, hcols(hh)]
                gt = gt_scr[d * DN_H + hh, pl.ds(pl.multiple_of(n * 8, 8), 8), :][0:1, 0:1]
                sfin_ref[0, d, hh] = (gt * st - _dot(m_scr[d * DN_H + hh, mrows, :], sb)
                                      + c_scr[d * DN_H + hh, mrows, :])
        return carry

    lax.fori_loop(0, nch, step, 0)

    for hh in range(DN_H):
        o = of_scr[:, hcols(hh)] + ob_scr[:, hcols(hh)]
        y = o * lax.rsqrt(jnp.mean(o * o, axis=-1, keepdims=True) + EPS) * nrm_ref[...]
        y_ref[:, hcols(hh)] = y * _silu(z_ref[:, hcols(hh)].astype(F32))


def _deltanet(proj_a, proj, conv_w, alog_b, dtb_b, nrm, row_off, nseq, seq, s0, layer, t_total, prev):
    r0 = row_off // seq
    w = DN_H * DN_DK
    spec = lambda col: pl.BlockSpec((seq, w), lambda b: (r0 + b, col // w))
    wspec = lambda j: pl.BlockSpec((3, w), lambda b: (0, j))
    bspec = pl.BlockSpec((2, DN_H, 8, LANE), lambda b: (0, 0, 0, 0))
    in_specs = [spec(COL_DNQKV), spec(COL_DNQKV + w), spec(COL_DNQKV + 2 * w), spec(COL_DNZ),
                pl.BlockSpec((seq, 2 * LANE), lambda b: (r0 + b, COL_KV // (2 * LANE))),
                wspec(0), wspec(1), wspec(2), bspec, bspec,
                pl.BlockSpec((1, DN_DK), lambda b: (0, 0))]
    args = [proj, proj, proj, proj, proj_a, conv_w, conv_w, conv_w, alog_b, dtb_b, nrm.reshape(1, DN_DK)]
    if s0 is not None:
        in_specs.append(pl.BlockSpec((1, 1, 2, DN_H, DN_DK, DN_DK), lambda b: (b, layer, 0, 0, 0, 0)))
        args.append(s0)
    nch = seq // CHUNK
    scratch = [pltpu.VMEM((seq, w), F32), pltpu.VMEM((seq, w), F32),
               pltpu.VMEM((seq, w), F32), pltpu.VMEM((seq, w), F32), pltpu.VMEM((seq, w), F32),
               pltpu.VMEM((2, seq, w), BF16), pltpu.VMEM((2, seq, w), F32),
               pltpu.VMEM((2 * DN_H, nch * DN_DK, DN_DK), BF16), pltpu.VMEM((2 * DN_H, nch * DN_DK, DN_DK), F32),
               pltpu.VMEM((2 * DN_H, nch * 8, LANE), F32)]
    kern, alias = _into(functools.partial(_dn_kernel, seq=seq, has_s0=s0 is not None), in_specs, args, prev)
    return pl.pallas_call(
        kern,
        out_shape=(jax.ShapeDtypeStruct((t_total, w), F32),
                   jax.ShapeDtypeStruct((nseq, 2, DN_H, DN_DK, DN_DK), F32)),
        grid=(nseq,),
        in_specs=in_specs,
        out_specs=(pl.BlockSpec((seq, w), lambda b: (r0 + b, 0)),
                   pl.BlockSpec((1, 2, DN_H, DN_DK, DN_DK), lambda b: (b, 0, 0, 0, 0))),
        scratch_shapes=scratch,
        input_output_aliases=alias,
        compiler_params=_cparams(("arbitrary",)),
    )(*args)


def _pack_bf16_pairs(x):
    n = x.shape[1] // 2
    bits = lax.bitcast_convert_type(x.astype(BF16).astype(F32), U32)
    return (bits[:, n:] & jnp.uint32(0xFFFF0000)) | (bits[:, :n] >> 16)


def _unpack_bf16_pairs(u):
    lo = lax.bitcast_convert_type(u << 16, F32)
    hi = lax.bitcast_convert_type(u & jnp.uint32(0xFFFF0000), F32)
    return lo, hi


ROW_T = (D // 2) // LANE


def _store_row_tiles(ref, x):
    for s in range(ROW_T):
        ref[:, s, :] = x[:, s * LANE:(s + 1) * LANE]


def _load_row_tiles(ref):
    return jnp.concatenate([ref[:, s, :] for s in range(ROW_T)], axis=1)


def _branch_out_kernel(b0, b1, b2, b3, g0, g1, g2, g3, x_ref, mod_ref, nf_ref, wb_ref, wo_ref,
                       xo_ref, h_ref, hp_ref):
    acc = None
    for n, (br, gl) in enumerate(((b0, g0), (b1, g1), (b2, g2), (b3, g3))):
        p = _sigmoid(gl[...].astype(F32)) * _dot(br[...].astype(BF16), wb_ref[n])
        acc = p if acc is None else acc + p
    y = _dot(acc.astype(BF16), wo_ref[...])
    m = mod_ref[0]
    x = x_ref[...] + m[:, 2 * D:3 * D] * y
    xo_ref[...] = x
    h = _rms(x, nf_ref[...]) * (1.0 + m[:, 4 * D:5 * D]) + m[:, 3 * D:4 * D]
    h_ref[...] = h
    _store_row_tiles(hp_ref, _pack_bf16_pairs(h))


def _branch_out(branches, proj, x, mod3, nf, wb, wo, grp_of_tile, tm):
    t = x.shape[0]
    gb = COL_GATE // D
    bspec = pl.BlockSpec((tm, MIX_W), lambda i: (i, 0))
    gspec = lambda n: pl.BlockSpec((tm, D), lambda i: (i, gb + n))
    row = pl.BlockSpec((tm, D), lambda i: (i, 0))
    return pl.pallas_call(
        _branch_out_kernel,
        out_shape=(jax.ShapeDtypeStruct((t, D), F32), jax.ShapeDtypeStruct((t, D), F32),
                   jax.ShapeDtypeStruct((t, ROW_T, LANE), U32)),
        grid=(t // tm,),
        in_specs=[bspec] * 4 + [gspec(n) for n in range(4)] + [
            row,
            pl.BlockSpec((1, 1, 6 * D), lambda i: (grp_of_tile(tm)(i), 0, 0)),
            pl.BlockSpec((1, D), lambda i: (0, 0)),
            pl.BlockSpec((4, MIX_W, D), lambda i: (0, 0, 0)),
            pl.BlockSpec((D, D), lambda i: (0, 0))],
        out_specs=(row, row, pl.BlockSpec((tm, ROW_T, LANE), lambda i: (i, 0, 0))),
        compiler_params=_cparams(("arbitrary",)),
    )(*branches, proj, proj, proj, proj, x, mod3, nf.reshape(1, D), wb, wo)


def _route_kernel(h_ref, wr_ref, bias_ref, idx_ref, w_ref, rank_ref, cnt_ref, run_scr, *, tm):
    @pl.when(pl.program_id(0) == 0)
    def _():
        run_scr[...] = jnp.zeros_like(run_scr)

    neg = -jnp.inf
    gsz = N_EXP // N_GRP
    scores = _sigmoid(_dot3_nt(wr_ref[...], h_ref[...]))
    choice = scores + bias_ref[:, 0:1]
    row8 = lax.broadcasted_iota(I32, (gsz, tm), 0)
    gscore = []
    for g in range(N_GRP):
        blk = choice[g * gsz:(g + 1) * gsz]
        m1 = jnp.max(blk, axis=0, keepdims=True)
        i1 = jnp.min(jnp.where(blk == m1, row8, gsz), axis=0, keepdims=True)
        m2 = jnp.max(jnp.where(row8 == i1, neg, blk), axis=0, keepdims=True)
        gscore.append(m1 + m2)
    masked = []
    for g in range(N_GRP):
        rank = jnp.zeros((1, tm), I32)
        for g2 in range(N_GRP):
            if g2 == g:
                continue
            ahead = (gscore[g2] >= gscore[g]) if g2 < g else (gscore[g2] > gscore[g])
            rank = rank + ahead.astype(I32)
        masked.append(jnp.where(rank < TOPK_GRP, choice[g * gsz:(g + 1) * gsz], neg))
    cur = jnp.concatenate(masked, axis=0)
    row = lax.broadcasted_iota(I32, (N_EXP, tm), 0)
    sel = jnp.zeros((N_EXP, tm), jnp.bool_)
    idxs, scs = [], []
    for _ in range(TOP_K):
        m = jnp.max(cur, axis=0, keepdims=True)
        ik = jnp.min(jnp.where(cur == m, row, N_EXP), axis=0, keepdims=True)
        hit = row == ik
        scs.append(jnp.sum(jnp.where(hit, scores, 0.0), axis=0, keepdims=True))
        idxs.append(ik)
        cur = jnp.where(hit, neg, cur)
        sel = sel | hit
    tot = functools.reduce(jnp.add, scs)
    self_f = jnp.where(sel, 1.0, 0.0)
    tri = (lax.broadcasted_iota(I32, (tm, tm), 0) <= lax.broadcasted_iota(I32, (tm, tm), 1))
    csum = _dot(self_f.astype(BF16), jnp.where(tri, 1.0, 0.0).astype(BF16))
    run = run_scr[:, 0:1]
    rank_all = run + csum - self_f
    ranks = [jnp.sum(jnp.where(row == ik, rank_all, 0.0), axis=0, keepdims=True) for ik in idxs]
    idx_ref[...] = jnp.concatenate(idxs, axis=0)
    w_ref[...] = jnp.concatenate([s / tot * ROUTED_SCALE for s in scs], axis=0)
    rank_ref[...] = jnp.concatenate(ranks, axis=0).astype(I32)
    new_run = run + csum[:, tm - 1:tm]
    run_scr[...] = jnp.broadcast_to(new_run, run_scr.shape)
    cnt_ref[...] = jnp.broadcast_to(new_run, cnt_ref.shape).astype(I32)


def _route(h, wr_t, bias_b, tm):
    t = h.shape[0]
    tok = pl.BlockSpec((TOP_K, tm), lambda i: (0, i))
    return pl.pallas_call(
        functools.partial(_route_kernel, tm=tm),
        out_shape=(jax.ShapeDtypeStruct((TOP_K, t), I32), jax.ShapeDtypeStruct((TOP_K, t), F32),
                   jax.ShapeDtypeStruct((TOP_K, t), I32), jax.ShapeDtypeStruct((N_EXP, LANE), I32)),
        grid=(t // tm,),
        in_specs=[pl.BlockSpec((tm, D), lambda i: (i, 0)),
                  pl.BlockSpec((N_EXP, D), lambda i: (0, 0)),
                  pl.BlockSpec((N_EXP, LANE), lambda i: (0, 0))],
        out_specs=(tok, tok, tok, pl.BlockSpec((N_EXP, LANE), lambda i: (0, 0))),
        scratch_shapes=[pltpu.VMEM((N_EXP, LANE), F32)],
        compiler_params=_cparams(("arbitrary",)),
    )(h, wr_t, bias_b)


def _slot_pos_kernel(ps_ref, idx_ref, rank_ref, pos_ref):
    idx = idx_ref[...]
    pos = rank_ref[...]
    for e in range(N_EXP):
        pos = pos + jnp.where(idx == e, ps_ref[e], 0)
    pos_ref[...] = pos


def _slot_pos(pstart, idx_t, rank_t, tm):
    t = idx_t.shape[1]
    tok = pl.BlockSpec((TOP_K, tm), lambda i, *_: (0, i))
    return pl.pallas_call(
        _slot_pos_kernel,
        out_shape=jax.ShapeDtypeStruct((TOP_K, t), I32),
        grid_spec=pltpu.PrefetchScalarGridSpec(
            num_scalar_prefetch=1, grid=(t // tm,), in_specs=[tok, tok], out_specs=tok),
        compiler_params=_cparams(("arbitrary",)),
    )(pstart, idx_t, rank_t)


def _dispatch_kernel(pstart_ref, padded_ref, pos_ref, h_ref, xs_ref, zero_scr, sem, zsem, *, tm):
    def zero_copy(e):
        start = pl.multiple_of(pstart_ref[e] + padded_ref[e] - MOE_BM, MOE_BM)
        return pltpu.make_async_copy(zero_scr, xs_ref.at[pl.ds(start, MOE_BM)], zsem)

    @pl.when(pl.program_id(0) == 0)
    def _():
        zero_scr[...] = jnp.zeros_like(zero_scr)

        def zstart(e, c):
            @pl.when(padded_ref[e] > 0)
            def _():
                zero_copy(e).start()
            return c

        def zwait(e, c):
            @pl.when(padded_ref[e] > 0)
            def _():
                zero_copy(e).wait()
            return c

        lax.fori_loop(0, N_EXP, zstart, 0)
        lax.fori_loop(0, N_EXP, zwait, 0)

    def row_copy(t, k):
        return pltpu.make_async_copy(h_ref.at[pl.ds(t, 1)], xs_ref.at[pl.ds(pos_ref[k, t], 1)], sem)

    def issue(t, c):
        for k in range(TOP_K):
            row_copy(t, k).start(priority=k % 2)
        return c

    def drain(t, c):
        for k in range(TOP_K):
            row_copy(t, k).wait()
        return c

    lax.fori_loop(0, tm, issue, 0)
    lax.fori_loop(0, tm, drain, 0)


def _dispatch(pstart, padded, pos_t, h, n_slots, tm):
    t = h.shape[0]
    smem_tok = pl.BlockSpec((TOP_K, tm), lambda i, *_: (0, i), memory_space=pltpu.SMEM)
    return pl.pallas_call(
        functools.partial(_dispatch_kernel, tm=tm),
        out_shape=jax.ShapeDtypeStruct((n_slots,) + h.shape[1:], h.dtype),
        grid_spec=pltpu.PrefetchScalarGridSpec(
            num_scalar_prefetch=2,
            grid=(t // tm,),
            in_specs=[smem_tok, pl.BlockSpec((tm,) + h.shape[1:], lambda i, *_: (i, 0, 0))],
            out_specs=pl.BlockSpec(memory_space=pl.ANY),
            scratch_shapes=[pltpu.VMEM((MOE_BM,) + h.shape[1:], h.dtype), pltpu.SemaphoreType.DMA(()),
                            pltpu.SemaphoreType.DMA(())]),
        compiler_params=_cparams(("arbitrary",)),
    )(pstart, padded, pos_t, h)


def _experts_kernel(be_ref, nu_ref, x_ref, wg_ref, wu_ref, wd_ref, y_ref, wg_s, wu_s, wd_s):
    b = pl.program_id(0)

    @pl.when(jnp.logical_or(b == 0, be_ref[b] != be_ref[jnp.maximum(b - 1, 0)]))
    def _():
        wg_s[...] = wg_ref[0, 0].astype(BF16)
        wu_s[...] = wu_ref[0, 0].astype(BF16)
        wd_s[...] = wd_ref[0, 0].astype(BF16)

    @pl.when(b < nu_ref[0])
    def _():
        x = jnp.concatenate(_unpack_bf16_pairs(_load_row_tiles(x_ref)), axis=1).astype(BF16)
        a = _silu(_dot(x, wg_s[...])) * _dot(x, wu_s[...])
        _store_row_tiles(y_ref, _pack_bf16_pairs(_dot(a.astype(BF16), wd_s[...])))


def _experts(block_e, n_used, xs, wg, wu, wd, layer):
    nb = xs.shape[0] // MOE_BM
    blk = lambda b, be, nu: (jnp.minimum(b, nu[0] - 1), 0, 0)
    wsel = lambda b, be, nu: (layer, be[b], 0, 0)
    return pl.pallas_call(
        _experts_kernel,
        out_shape=jax.ShapeDtypeStruct(xs.shape, U32),
        grid_spec=pltpu.PrefetchScalarGridSpec(
            num_scalar_prefetch=2,
            grid=(nb,),
            in_specs=[pl.BlockSpec((MOE_BM, ROW_T, LANE), blk),
                      pl.BlockSpec((1, 1, D, D_EXP), wsel),
                      pl.BlockSpec((1, 1, D, D_EXP), wsel),
                      pl.BlockSpec((1, 1, D_EXP, D), wsel)],
            out_specs=pl.BlockSpec((MOE_BM, ROW_T, LANE), blk),
            scratch_shapes=[pltpu.VMEM((D, D_EXP), BF16), pltpu.VMEM((D, D_EXP), BF16),
                            pltpu.VMEM((D_EXP, D), BF16)]),
        compiler_params=_cparams(("arbitrary",)),
    )(block_e, n_used, xs, wg, wu, wd)


def _combine_kernel(*refs, tm, final):
    it = iter(refs)
    pos_ref, w_ref, h_ref, x_ref, mod_ref, wsg_ref, wsu_ref, wsd_ref = (next(it) for _ in range(8))
    if final:
        fn_ref = next(it)
    ys_ref = next(it)
    xo_ref = next(it)
    if final:
        yo_ref = next(it)
    buf, sem = next(it), next(it)

    def row_copy(t, k):
        return pltpu.make_async_copy(ys_ref.at[pl.ds(pos_ref[k, t], 1)], buf.at[k, pl.ds(t, 1)], sem)

    def issue(t, c):
        for k in range(TOP_K):
            row_copy(t, k).start(priority=k % 2)
        return c

    def drain(t, c):
        for k in range(TOP_K):
            row_copy(t, k).wait()
        return c

    lax.fori_loop(0, tm, issue, 0)
    hb = h_ref[...].astype(BF16)
    a = _silu(_dot(hb, wsg_ref[...])) * _dot(hb, wsu_ref[...])
    shared = _dot(a.astype(BF16), wsd_ref[...])
    lax.fori_loop(0, tm, drain, 0)
    w = w_ref[...]
    lo, hi = None, None
    for k in range(TOP_K):
        yl, yh = _unpack_bf16_pairs(_load_row_tiles(buf.at[k]))
        wk = w[:, k:k + 1]
        lo = yl * wk if lo is None else lo + yl * wk
        hi = yh * wk if hi is None else hi + yh * wk
    routed = jnp.concatenate([lo, hi], axis=1)
    m = mod_ref[0]
    x = x_ref[...] + m[:, 5 * D:6 * D] * (routed + shared)
    xo_ref[...] = x
    if final:
        yo_ref[...] = _rms(x, fn_ref[...])


def _combine(pos_t, w_tok, h, x, mod3, wsg, wsu, wsd, fn, ys, grp_of_tile, tm):
    t = h.shape[0]
    final = fn is not None
    row = pl.BlockSpec((tm, D), lambda i: (i, 0))
    full = lambda shp: pl.BlockSpec(shp, lambda i: (0,) * len(shp))
    in_specs = [pl.BlockSpec((TOP_K, tm), lambda i: (0, i), memory_space=pltpu.SMEM),
                pl.BlockSpec((tm, TOP_K), lambda i: (i, 0)), row, row,
                pl.BlockSpec((1, 1, 6 * D), lambda i: (grp_of_tile(tm)(i), 0, 0)),
                full((D, D_EXP)), full((D, D_EXP)), full((D_EXP, D))]
    args = [pos_t, w_tok, h, x, mod3, wsg, wsu, wsd]
    if final:
        in_specs.append(full((1, D)))
        args.append(fn.reshape(1, D))
    in_specs.append(pl.BlockSpec(memory_space=pl.ANY))
    args.append(ys)
    out_shape = [jax.ShapeDtypeStruct((t, D), F32)]
    out_specs = [row]
    if final:
        out_shape.append(jax.ShapeDtypeStruct((t, D), F32))
        out_specs.append(row)
    return pl.pallas_call(
        functools.partial(_combine_kernel, tm=tm, final=final),
        out_shape=tuple(out_shape),
        grid=(t // tm,),
        in_specs=in_specs,
        out_specs=tuple(out_specs),
        scratch_shapes=[pltpu.VMEM((TOP_K, tm, ROW_T, LANE), U32), pltpu.SemaphoreType.DMA(())],
        compiler_params=_cparams(("arbitrary",)),
    )(*args)


_W_IN_CUTS = np.cumsum([0, 256, 160, 1536, 512, 16, 2048, 1536, 4096])


def _pack_w_in_kernel(w_ref, wa_ref, wb_ref):
    seg = lambda i: w_ref[0, :, int(_W_IN_CUTS[i]):int(_W_IN_CUTS[i + 1])]
    pad = jnp.zeros((w_ref.shape[1], 2 * LANE - 160 - 16), F32)
    wa_ref[...] = jnp.concatenate([seg(0), seg(1), seg(4), pad], axis=1).astype(BF16)
    wb_ref[...] = jnp.concatenate([seg(7), seg(2), seg(3), seg(5), seg(6)], axis=1).astype(BF16)


def _pack_w_in(w_all, layer):
    rows = 128
    return pl.pallas_call(
        _pack_w_in_kernel,
        out_shape=(jax.ShapeDtypeStruct((D, N_PROJ_A), BF16), jax.ShapeDtypeStruct((D, N_PROJ_B), BF16)),
        grid=(D // rows,),
        in_specs=[pl.BlockSpec((1, rows, w_all.shape[2]), lambda i: (layer, i, 0))],
        out_specs=(pl.BlockSpec((rows, N_PROJ_A), lambda i: (i, 0)),
                   pl.BlockSpec((rows, N_PROJ_B), lambda i: (i, 0))),
        compiler_params=_cparams(("arbitrary",)),
    )(w_all)


def _pack_w_uq(w):
    w = w.reshape(MLA_QR, MLA_H, MLA_NOPE + MLA_ROPE)
    w = jnp.pad(w, ((0, 0), (0, 0), (0, LANE - MLA_NOPE - MLA_ROPE)))
    return w.reshape(MLA_QR, MLA_H * LANE).astype(BF16)


def _pack_w_ukv(w):
    w = w.reshape(MLA_KVR, MLA_H, MLA_NOPE + MLA_V)
    k_nope = jnp.pad(w[:, :, :MLA_NOPE], ((0, 0), (0, 0), (0, LANE - MLA_NOPE)))
    eye = jnp.eye(MLA_ROPE, dtype=w.dtype)[:, None, :]
    k_pe = jnp.pad(jnp.broadcast_to(eye, (MLA_ROPE, MLA_H, MLA_ROPE)),
                   ((0, LANE - MLA_ROPE), (0, 0), (MLA_NOPE, LANE - MLA_NOPE - MLA_ROPE)))
    wk = jnp.concatenate([k_nope, k_pe], axis=0).reshape(2 * LANE, MLA_H * LANE)
    wv = w[:, :, MLA_NOPE:].reshape(MLA_KVR, MLA_H * MLA_V)
    return wk.astype(BF16), wv.astype(BF16)


def _axial_angles(n_tok, dim):
    nf = dim // 4
    inv = ROPE_BASE ** (-jnp.arange(nf, dtype=F32) / nf)
    r = jnp.repeat(jnp.arange(n_tok // GRID_W, dtype=F32), GRID_W)
    cc = jnp.tile(jnp.arange(GRID_W, dtype=F32), n_tok // GRID_W)
    ang = jnp.concatenate([r[:, None] * inv, cc[:, None] * inv], axis=-1)
    return jnp.cos(ang), jnp.sin(ang)


def _rope_tables_mla(n_tok, lane0):
    cos, sin = _axial_angles(n_tok, MLA_ROPE)
    half = MLA_ROPE // 2
    z = lambda w: jnp.zeros((n_tok, w), F32)
    o = lambda w: jnp.ones((n_tok, w), F32)
    rest = LANE - lane0 - MLA_ROPE
    c = jnp.concatenate([o(lane0), cos, cos, o(rest)], axis=1)
    s1 = jnp.concatenate([z(lane0), -sin, z(half), z(rest)], axis=1)
    s2 = jnp.concatenate([z(lane0), z(half), sin, z(rest)], axis=1)
    return c, s1, s2


def _rope_tables_ret(n_tok):
    cos, sin = _axial_angles(n_tok, RET_DK)
    return jnp.concatenate([cos, cos], axis=1), jnp.concatenate([-sin, sin], axis=1)


def _bcast_dh(a):
    return jnp.broadcast_to(a.astype(F32)[:, :, None, None], a.shape + (8, LANE))


def _forward(x_prompt, x_sample, c, cache_ckv, cache_kpe, state_dn, state_ret, c_ctx, P, tiles):
    nb, sc, _ = x_prompt.shape
    nl, sl, _ = x_sample.shape
    past = cache_ckv.shape[2]
    depth = P['w_in'].shape[0]
    t_ctx, t_lat = nb * sc, nl * sl
    t = t_ctx + t_lat
    tm = tiles['tm']
    assert sc % tm == 0 and sl % tm == 0 and t_ctx % sl == 0

    assert sl % tiles['tc'] == 0 and t_ctx % tiles['tc'] == 0

    def grp_of_tile(rows_per_tile):
        nct = t_ctx // rows_per_tile
        return lambda i: jnp.where(i < nct, 0, 1 + (i - nct) // (sl // rows_per_tile))

    x = jnp.concatenate([x_prompt.reshape(t_ctx, D), x_sample.reshape(t_lat, D)], axis=0)
    ngrp = 1 + nl
    cvec = jnp.concatenate([c_ctx[None], c, jnp.zeros((-(ngrp) % 8, D), F32)], axis=0)

    tab_q = _rope_tables_mla(sl, MLA_NOPE)
    tab_k = _rope_tables_mla(sl, 0)
    tab_r = _rope_tables_ret(sl)

    ckv_l, kpe_l, dn_l, ret_l = [], [], [], []
    y_final = None
    for l in range(depth):
        mod3 = _ada(cvec, P['w_ada'], P['b_ada'][l], l).reshape(cvec.shape[0], 1, 6 * D)
        w_a, w_b = _pack_w_in(P['w_in'], l)
        tp = tiles['tp']
        proj_a = _in_proj(x, mod3, P['norm_mix'][l], w_a, grp_of_tile, tp, N_PROJ_A, F32)
        proj = _in_proj(x, mod3, P['norm_mix'][l], w_b, grp_of_tile, tp, tiles['tn'], BF16)

        w_uq = _pack_w_uq(P['mla_w_uq'][l])
        wk, wv = _pack_w_ukv(P['mla_w_ukv'][l])
        gq, gkv = P['mla_q_norm'][l], P['mla_kv_norm'][l]
        q_c = _mla_q(proj_a, gq, w_uq, 0, t_ctx, tm, None, sc)
        q_l = _mla_q(proj_a, gq, w_uq, t_ctx, t_lat, tm, tab_q, sl)
        kvb = COL_KV // (2 * LANE)
        ckv_c, kpe_c, k_c, v_c = _mla_kv(proj_a, kvb, gkv, wk, wv, 0, t_ctx, tm, True, None, sc)
        _, _, k_l, v_l = _mla_kv(proj_a, kvb, gkv, wk, wv, t_ctx, t_lat, tm, True, tab_k, sl)
        cached = jnp.concatenate([cache_ckv[:, l], cache_kpe[:, l],
                                  jnp.zeros((nl, past, 2 * LANE - MLA_KVR - MLA_ROPE), F32)], axis=-1)
        pt = min(tm, past)
        _, _, k_p, v_p = _mla_kv(cached.reshape(nl * past, 2 * LANE), 0, gkv, wk, wv, 0, nl * past, pt,
                                 False, None, past)
        r3 = lambda a, n, s: a.reshape(n, s, a.shape[-1])
        y_mla = _attn(r3(q_c, nb, sc), [r3(k_c, nb, sc)], [r3(v_c, nb, sc)], min(sc, tiles['tq']), t, 0, None)
        y_mla = _attn(r3(q_l, nl, sl), [r3(k_p, nl, past), r3(k_l, nl, sl)],
                      [r3(v_p, nl, past), r3(v_l, nl, sl)], min(sl, tiles['tq']), t, t_ctx, y_mla)

        alog_b, dtb_b = _bcast_dh(P['dn_A_log'][l]), _bcast_dh(P['dn_dt_bias'][l])
        dn_args = (proj_a, proj, P['dn_conv'][l], alog_b, dtb_b, P['dn_norm'][l])
        y_dn, dn_fin = _deltanet(*dn_args, 0, nb, sc, None, l, t, None)
        y_dn, _ = _deltanet(*dn_args, t_ctx, nl, sl, state_dn, l, t, y_dn)

        dec_b = _bcast_dh(P['ret_decay'][l])
        y_ret, ret_fin = _retention(proj, dec_b, P['ret_gn'][l], 0, nb, sc, None, None, l, t, None)
        y_ret, _ = _retention(proj, dec_b, P['ret_gn'][l], t_ctx, nl, sl, tab_r, state_ret, l, t, y_ret)

        y_sc = _sconv(proj, P['sc_conv'][l], 0, nb, sc, t, None)
        y_sc = _sconv(proj, P['sc_conv'][l], t_ctx, nl, sl, t, y_sc)

        x_mid, h2, h2p = _branch_out((y_mla, y_dn, y_ret, y_sc), proj, x, mod3, P['norm_ffn'][l],
                                P['w_branch'][l].astype(BF16), P['w_out'][l].astype(BF16), grp_of_tile,
                                tiles['tb'])

        bias_b = jnp.broadcast_to(P['router_bias'][l].astype(F32)[:, None], (N_EXP, LANE))
        idx_t, w_t, rank_t, cnt = _route(h2, P['router'][l].T, bias_b, tiles['tr'])
        counts = cnt[:, 0]
        padded = (counts + MOE_BM - 1) // MOE_BM * MOE_BM
        pad_end = jnp.cumsum(padded)
        pstart = (pad_end - padded).astype(I32)
        n_blocks = (t * TOP_K) // MOE_BM + N_EXP
        blk_row0 = jnp.arange(n_blocks, dtype=I32) * MOE_BM
        block_e = jnp.minimum(jnp.sum((pad_end[None, :] <= blk_row0[:, None]).astype(I32), axis=1), N_EXP - 1)
        n_used = (pad_end[-1:] // MOE_BM).astype(I32)
        pos_t = _slot_pos(pstart, idx_t, rank_t, tiles['tr'])
        xs = _dispatch(pstart, padded.astype(I32), pos_t, h2p, n_blocks * MOE_BM, tiles['td'])
        ys = _experts(block_e, n_used, xs, P['w_eg'], P['w_eu'], P['w_ed'], l)
        fn = P['final_norm'] if l == depth - 1 else None
        outs = _combine(pos_t, w_t.T, h2, x_mid, mod3, P['w_sg'][l].astype(BF16),
                        P['w_su'][l].astype(BF16), P['w_sd'][l].astype(BF16), fn, ys, grp_of_tile, tiles['tc'])
        x = outs[0]
        if fn is not None:
            y_final = outs[1]

        ckv_l.append(ckv_c.reshape(nb, sc, MLA_KVR))
        kpe_l.append(kpe_c.reshape(nb, sc, MLA_ROPE))
        dn_l.append(dn_fin)
        ret_l.append(ret_fin)

    y_prompt = y_final[:t_ctx].reshape(nb, sc, D)
    y_sample = y_final[t_ctx:].reshape(nl, sl, D)
    return (y_prompt, y_sample, jnp.stack(ckv_l, axis=1), jnp.stack(kpe_l, axis=1),
            jnp.stack(dn_l, axis=1), jnp.stack(ret_l, axis=1))


_TILES = dict(tm=256, tp=1024, tn=2432, tb=512, tq=256, tr=512, td=256, tc=128)


def kernel(x_prompt, x_sample, c, cache_ckv, cache_kpe, state_dn, state_ret, c_ctx, w_ada, b_ada, norm_mix, norm_ffn, w_in, mla_q_norm, mla_w_uq, mla_kv_norm, mla_w_ukv, dn_conv, dn_A_log, dn_dt_bias, dn_norm, ret_decay, ret_gn, sc_conv, w_branch, w_out, router, router_bias, w_eg, w_eu, w_ed, w_sg, w_su, w_sd, final_norm):
    P = dict(w_ada=w_ada, b_ada=b_ada, norm_mix=norm_mix, norm_ffn=norm_ffn, w_in=w_in,
             mla_q_norm=mla_q_norm, mla_w_uq=mla_w_uq, mla_kv_norm=mla_kv_norm, mla_w_ukv=mla_w_ukv,
             dn_conv=dn_conv, dn_A_log=dn_A_log, dn_dt_bias=dn_dt_bias, dn_norm=dn_norm,
             ret_decay=ret_decay, ret_gn=ret_gn, sc_conv=sc_conv, w_branch=w_branch, w_out=w_out,
             router=router, router_bias=router_bias, w_eg=w_eg, w_eu=w_eu, w_ed=w_ed,
             w_sg=w_sg, w_su=w_su, w_sd=w_sd, final_norm=final_norm)
    return _forward(x_prompt, x_sample, c, cache_ckv, cache_kpe, state_dn, state_ret, c_ctx, P, _TILES)
```

```python
import functools
import math

import jax
import jax.numpy as jnp
import numpy as np
from jax import lax
from jax.experimental import pallas as pl
from jax.experimental.pallas import tpu as pltpu

F32 = jnp.float32
BF16 = jnp.bfloat16
I32 = jnp.int32
U32 = jnp.uint32

D = 1024
EPS = 1e-6
CHUNK = 64
PAIR = 2 * CHUNK
GRID_W = 64
ROPE_BASE = 10000.0

MLA_H, MLA_NOPE, MLA_ROPE, MLA_V, MLA_QR, MLA_KVR = 8, 64, 32, 64, 256, 128
DN_H, DN_DK = 4, 128
RET_H, RET_DK = 4, 128
MIX_W = 512
N_EXP, TOP_K, N_GRP, TOPK_GRP, D_EXP = 64, 8, 8, 4, 256
ROUTED_SCALE = 2.5

LANE = 128
COL_CQ = 0
COL_KV = 256
N_PROJ_A = 512
AB_LANE = 160
COL_GATE = 0
COL_DNQKV = 4096
COL_DNZ = 5632
COL_RET = 6144
COL_SC = 8192
N_PROJ_B = 9728

MOE_BM = 512
VMEM_LIMIT = 56 * 1024 * 1024


def _cparams(sem, vmem=None):
    return pltpu.CompilerParams(dimension_semantics=sem, vmem_limit_bytes=vmem or VMEM_LIMIT)


def _dot(a, b):
    return jnp.dot(a, b, preferred_element_type=F32)


def _dot_nt(a, b):
    return lax.dot_general(a, b, (((1,), (1,)), ((), ())), preferred_element_type=F32)


def _dot_tn(a, b):
    return lax.dot_general(a, b, (((0,), (0,)), ((), ())), preferred_element_type=F32)


def _split(a):
    hi = a.astype(BF16)
    lo = (a - hi.astype(F32)).astype(BF16)
    return hi, lo


def _dot3(a, b):
    ah, al = _split(a)
    bh, bl = _split(b)
    return _dot(ah, bh) + (_dot(ah, bl) + _dot(al, bh))


def _dot3s(a, b):
    (ah, al), (bh, bl) = a, b
    return _dot(jnp.concatenate([ah, ah, al], axis=1), jnp.concatenate([bh, bl, bh], axis=0))


def _dot3_nt(a, b):
    ah, al = _split(a)
    bh, bl = _split(b)
    return _dot_nt(ah, bh) + (_dot_nt(ah, bl) + _dot_nt(al, bh))


def _sigmoid(x):
    return 1.0 / (1.0 + jnp.exp(-x))


def _silu(x):
    return x * _sigmoid(x)


def _rms(x, g):
    return x * lax.rsqrt(jnp.mean(x * x, axis=-1, keepdims=True) + EPS) * g


def _ada_kernel(c_ref, w_ref, b_ref, o_ref):
    o_ref[...] = _dot3(_silu(c_ref[...]), w_ref[0]) + b_ref[...]


def _ada(cvec, w_all, b, layer):
    n = w_all.shape[2]
    tn = 1024
    return pl.pallas_call(
        _ada_kernel,
        out_shape=jax.ShapeDtypeStruct((cvec.shape[0], n), F32),
        grid=(n // tn,),
        in_specs=[pl.BlockSpec(cvec.shape, lambda j: (0, 0)),
                  pl.BlockSpec((1, D, tn), lambda j: (layer, 0, j)),
                  pl.BlockSpec((1, tn), lambda j: (0, j))],
        out_specs=pl.BlockSpec((cvec.shape[0], tn), lambda j: (0, j)),
        compiler_params=_cparams(("arbitrary",)),
    )(cvec, w_all, b.reshape(1, n))


def _in_proj_kernel(x_ref, mod_ref, g_ref, w_ref, o_ref, h_scr):
    @pl.when(pl.program_id(1) == 0)
    def _():
        m = mod_ref[0]
        y = _rms(x_ref[...], g_ref[...])
        h_scr[...] = (y * (1.0 + m[:, D:2 * D]) + m[:, 0:D]).astype(BF16)

    o_ref[...] = _dot(h_scr[...], w_ref[...]).astype(o_ref.dtype)


def _in_proj(x, mod3, g, w, grp_of_tile, tm, tn, out_dtype):
    t = x.shape[0]
    n = w.shape[1]
    return pl.pallas_call(
        _in_proj_kernel,
        out_shape=jax.ShapeDtypeStruct((t, n), out_dtype),
        grid=(t // tm, n // tn),
        in_specs=[pl.BlockSpec((tm, D), lambda i, j: (i, 0)),
                  pl.BlockSpec((1, 1, 6 * D), lambda i, j: (grp_of_tile(tm)(i), 0, 0)),
                  pl.BlockSpec((1, D), lambda i, j: (0, 0)),
                  pl.BlockSpec((D, tn), lambda i, j: (0, j))],
        out_specs=pl.BlockSpec((tm, tn), lambda i, j: (i, j)),
        scratch_shapes=[pltpu.VMEM((tm, D), BF16)],
        compiler_params=_cparams(("arbitrary", "arbitrary")),
    )(x, mod3, g.reshape(1, D), w)


def _rope3(x, c, s1, s2, width):
    return x * c + pltpu.roll(x, width - 16, 1) * s1 + pltpu.roll(x, 16, 1) * s2


def _mla_q_kernel(*refs, rope):
    if rope:
        p_ref, g_ref, w_ref, c_ref, s1_ref, s2_ref, o_ref = refs
    else:
        p_ref, g_ref, w_ref, o_ref = refs
    y = _rms(p_ref[...], g_ref[...])
    q = _dot(y.astype(BF16), w_ref[...])
    if rope:
        tile = lambda r: jnp.concatenate([r[...]] * MLA_H, axis=1)
        q = _rope3(q, tile(c_ref), tile(s1_ref), tile(s2_ref), MLA_H * LANE)
    o_ref[...] = q.astype(BF16)


def _mla_q(proj, g, w, row_off, rows, tm, rope_tabs, seq):
    nrow = rows // tm
    r0 = row_off // tm
    in_specs = [pl.BlockSpec((tm, MLA_QR), lambda i: (r0 + i, COL_CQ // MLA_QR)),
                pl.BlockSpec((1, MLA_QR), lambda i: (0, 0)),
                pl.BlockSpec((MLA_QR, MLA_H * LANE), lambda i: (0, 0))]
    args = [proj, g.reshape(1, MLA_QR), w]
    if rope_tabs is not None:
        per = seq // tm
        in_specs += [pl.BlockSpec((tm, LANE), lambda i: (i % per, 0))] * 3
        args += list(rope_tabs)
    return pl.pallas_call(
        functools.partial(_mla_q_kernel, rope=rope_tabs is not None),
        out_shape=jax.ShapeDtypeStruct((rows, MLA_H * LANE), BF16),
        grid=(nrow,),
        in_specs=in_specs,
        out_specs=pl.BlockSpec((tm, MLA_H * LANE), lambda i: (i, 0)),
        compiler_params=_cparams(("arbitrary",)),
    )(*args)


def _mla_kv_kernel(*refs, norm, rope):
    if rope:
        p_ref, g_ref, wk_ref, wv_ref, c_ref, s1_ref, s2_ref, ckv_ref, kpe_ref, k_ref, v_ref = refs
    else:
        p_ref, g_ref, wk_ref, wv_ref, ckv_ref, kpe_ref, k_ref, v_ref = refs
    blk = p_ref[...]
    ckv = blk[:, :MLA_KVR]
    if norm:
        ckv = _rms(ckv, g_ref[...])
    kp = blk[:, MLA_KVR:]
    ckv_ref[...] = ckv
    kpe_ref[...] = kp[:, :MLA_ROPE]
    if rope:
        kp = _rope3(kp, c_ref[...], s1_ref[...], s2_ref[...], LANE)
    a = jnp.concatenate([ckv, kp], axis=1).astype(BF16)
    k_ref[...] = _dot(a, wk_ref[...]).astype(BF16)
    v_ref[...] = _dot(ckv.astype(BF16), wv_ref[...]).astype(BF16)


def _mla_kv(src, col_blk, g, wk, wv, row_off, rows, tm, norm, rope_tabs, seq):
    nrow = rows // tm
    r0 = row_off // tm
    in_specs = [pl.BlockSpec((tm, 2 * LANE), lambda i: (r0 + i, col_blk)),
                pl.BlockSpec((1, MLA_KVR), lambda i: (0, 0)),
                pl.BlockSpec((2 * LANE, MLA_H * LANE), lambda i: (0, 0)),
                pl.BlockSpec((MLA_KVR, MLA_H * MLA_V), lambda i: (0, 0))]
    args = [src, g.reshape(1, MLA_KVR), wk, wv]
    if rope_tabs is not None:
        per = seq // tm
        in_specs += [pl.BlockSpec((tm, LANE), lambda i: (i % per, 0))] * 3
        args += list(rope_tabs)
    return pl.pallas_call(
        functools.partial(_mla_kv_kernel, norm=norm, rope=rope_tabs is not None),
        out_shape=(jax.ShapeDtypeStruct((rows, MLA_KVR), F32),
                   jax.ShapeDtypeStruct((rows, MLA_ROPE), F32),
                   jax.ShapeDtypeStruct((rows, MLA_H * LANE), BF16),
                   jax.ShapeDtypeStruct((rows, MLA_H * MLA_V), BF16)),
        grid=(nrow,),
        in_specs=in_specs,
        out_specs=(pl.BlockSpec((tm, MLA_KVR), lambda i: (i, 0)),
                   pl.BlockSpec((tm, MLA_ROPE), lambda i: (i, 0)),
                   pl.BlockSpec((tm, MLA_H * LANE), lambda i: (i, 0)),
                   pl.BlockSpec((tm, MLA_H * MLA_V), lambda i: (i, 0))),
        compiler_params=_cparams(("arbitrary",)),
    )(*args)


def _attn_kernel(*refs, nseg):
    q_ref = refs[0]
    k_refs = refs[1:1 + nseg]
    v_refs = refs[1 + nseg:1 + 2 * nseg]
    o_ref = refs[1 + 2 * nseg]
    scale = (MLA_NOPE + MLA_ROPE) ** -0.5
    def scores(h):
        qh = q_ref[0, :, h * LANE:(h + 1) * LANE]
        return [_dot_nt(qh, k_ref[0, :, h * LANE:(h + 1) * LANE]) * scale for k_ref in k_refs]

    outs = []
    nxt = scores(0)
    for h in range(MLA_H):
        ss, nxt = nxt, (scores(h + 1) if h + 1 < MLA_H else None)
        m = functools.reduce(jnp.maximum, [jnp.max(s, axis=-1, keepdims=True) for s in ss])
        es = [jnp.exp(s - m) for s in ss]
        inv = 1.0 / functools.reduce(jnp.add, [jnp.sum(e, axis=-1, keepdims=True) for e in es])
        o = functools.reduce(jnp.add, [
            _dot((e * inv).astype(BF16), v_ref[0, :, h * MLA_V:(h + 1) * MLA_V])
            for e, v_ref in zip(es, v_refs)])
        outs.append(o)
    o_ref[...] = jnp.concatenate(outs, axis=1)


def _into(kernel, in_specs, args, prev):
    if prev is None:
        return kernel, {}
    pos = len(args)
    in_specs.append(pl.BlockSpec(memory_space=pl.ANY))
    args.append(prev)
    return (lambda *refs: kernel(*refs[:pos], *refs[pos + 1:])), {pos: 0}


def _attn(q, ks, vs, tq, t_total, row_off, prev):
    b, s, _ = q.shape
    nseg = len(ks)
    in_specs = [pl.BlockSpec((1, tq, MLA_H * LANE), lambda i, j: (i, j, 0))]
    in_specs += [pl.BlockSpec((1,) + k.shape[1:], lambda i, j: (i, 0, 0)) for k in ks]
    in_specs += [pl.BlockSpec((1,) + v.shape[1:], lambda i, j: (i, 0, 0)) for v in vs]
    args = [q, *ks, *vs]
    kern, alias = _into(functools.partial(_attn_kernel, nseg=nseg), in_specs, args, prev)
    r0, per = row_off // tq, s // tq
    return pl.pallas_call(
        kern,
        out_shape=jax.ShapeDtypeStruct((t_total, MLA_H * MLA_V), F32),
        grid=(b, per),
        in_specs=in_specs,
        out_specs=pl.BlockSpec((tq, MLA_H * MLA_V), lambda i, j: (r0 + i * per + j, 0)),
        input_output_aliases=alias,
        compiler_params=_cparams(("arbitrary", "arbitrary")),
    )(*args)


def _conv3(x, w):
    s = x.shape[0]
    row = lax.broadcasted_iota(I32, x.shape, 0)
    prev = jnp.where(row == 0, 0.0, pltpu.roll(x, 1, 0))
    nxt = jnp.where(row == s - 1, 0.0, pltpu.roll(x, s - 1, 0))
    return prev * w[0:1] + x * w[1:2] + nxt * w[2:3]


def _sconv_kernel(b_ref, c_ref, x_ref, w_ref, o_ref):
    f = lambda r: r[...].astype(F32)
    o_ref[...] = f(b_ref) * _conv3(f(c_ref) * f(x_ref), w_ref[...])


def _sconv(proj, w, row_off, nseq, seq, t_total, prev):
    r0 = row_off // seq
    cb = COL_SC // MIX_W
    spec = lambda off: pl.BlockSpec((seq, MIX_W), lambda b: (r0 + b, cb + off))
    in_specs = [spec(0), spec(1), spec(2), pl.BlockSpec((3, MIX_W), lambda b: (0, 0))]
    args = [proj, proj, proj, w]
    kern, alias = _into(_sconv_kernel, in_specs, args, prev)
    return pl.pallas_call(
        kern,
        out_shape=jax.ShapeDtypeStruct((t_total, MIX_W), F32),
        grid=(nseq,),
        in_specs=in_specs,
        out_specs=pl.BlockSpec((seq, MIX_W), lambda b: (r0 + b, 0)),
        input_output_aliases=alias,
        compiler_params=_cparams(("arbitrary",)),
    )(*args)


def _ret_kernel(*refs, seq, rope, has_s0):
    it = iter(refs)
    q_ref, k_ref, v_ref, g_ref, dec_ref, gn_ref = (next(it) for _ in range(6))
    if rope:
        c_ref, s_ref = next(it), next(it)
    if has_s0:
        s0_ref = next(it)
    y_ref, sfin_ref, of_scr, ob_scr = (next(it) for _ in range(4))
    nch = seq // CHUNK
    ti = lax.broadcasted_iota(I32, (CHUNK, CHUNK), 0)
    si = lax.broadcasted_iota(I32, (CHUNK, CHUNK), 1)
    pos = lax.broadcasted_iota(I32, (CHUNK, 1), 0).astype(F32)
    sls = [slice(n * CHUNK, (n + 1) * CHUNK) for n in range(nch)]

    for hh in range(RET_H):
        cs = slice(hh * RET_DK, (hh + 1) * RET_DK)
        q = q_ref[:, cs].astype(F32)
        k = k_ref[:, cs].astype(F32) * (RET_DK ** -0.5)
        if rope:
            c, sn = c_ref[...], s_ref[...]
            q = q * c + pltpu.roll(q, RET_DK // 2, 1) * sn
            k = k * c + pltpu.roll(k, RET_DK // 2, 1) * sn
        qb, kb, vb = q.astype(BF16), k.astype(BF16), v_ref[:, cs]
        qk = [_dot_nt(qb[sl], kb[sl]) for sl in sls]
        dirs = []
        for d in range(2):
            lg = -jnp.exp(dec_ref[d, hh, 0:1, 0:1])
            if d == 0:
                dist = (ti - si).astype(F32)
                qpow, kpow = pos + 1.0, (CHUNK - 1.0) - pos
            else:
                dist = (si - ti).astype(F32)
                qpow, kpow = CHUNK - pos, pos
            decay = jnp.where(dist >= 0, jnp.exp(jnp.maximum(dist, 0.0) * lg), 0.0)
            qs = jnp.exp(qpow * lg)
            ks = jnp.exp(kpow * lg)
            o_intra = [_dot((qk[n] * decay).astype(BF16), vb[sls[n]]) for n in range(nch)]
            kv = [_dot_tn((k[sls[n]] * ks).astype(BF16), vb[sls[n]]) for n in range(nch)]
            dirs.append((qs, jnp.exp(CHUNK * lg), o_intra, kv))
        for d in range(2):
            qs, gtot, o_intra, kv = dirs[d]
            st = s0_ref[0, 0, d, hh] if has_s0 else jnp.zeros((RET_DK, RET_DK), F32)
            o_scr = of_scr if d == 0 else ob_scr
            order = range(nch) if d == 0 else range(nch - 1, -1, -1)
            for n in order:
                o_scr[sls[n], cs] = o_intra[n] + _dot((q[sls[n]] * qs).astype(BF16), st.astype(BF16))
                st = gtot * st + kv[n]
            sfin_ref[0, d, hh] = st

        o = of_scr[:, cs] + ob_scr[:, cs]
        dlt = o - jnp.mean(o, axis=-1, keepdims=True)
        y = dlt * lax.rsqrt(jnp.mean(dlt * dlt, axis=-1, keepdims=True) + EPS) * gn_ref[:, cs]
        y_ref[:, cs] = _silu(g_ref[:, cs].astype(F32)) * y


def _retention(proj, dec_b, gn, row_off, nseq, seq, rope_tabs, s0, layer, t_total, prev):
    r0 = row_off // seq
    w = RET_H * RET_DK
    cb = COL_RET // w
    spec = lambda off: pl.BlockSpec((seq, w), lambda b: (r0 + b, cb + off))
    in_specs = [spec(0), spec(1), spec(2), spec(3),
                pl.BlockSpec((2, RET_H, 8, LANE), lambda b: (0, 0, 0, 0)),
                pl.BlockSpec((1, w), lambda b: (0, 0))]
    args = [proj, proj, proj, proj, dec_b, gn.reshape(1, w)]
    if rope_tabs is not None:
        in_specs += [pl.BlockSpec((seq, LANE), lambda b: (0, 0))] * 2
        args += list(rope_tabs)
    if s0 is not None:
        in_specs.append(pl.BlockSpec((1, 1, 2, RET_H, RET_DK, RET_DK), lambda b: (b, layer, 0, 0, 0, 0)))
        args.append(s0)
    kern, alias = _into(functools.partial(_ret_kernel, seq=seq, rope=rope_tabs is not None, has_s0=s0 is not None),
                        in_specs, args, prev)
    return pl.pallas_call(
        kern,
        out_shape=(jax.ShapeDtypeStruct((t_total, w), F32),
                   jax.ShapeDtypeStruct((nseq, 2, RET_H, RET_DK, RET_DK), F32)),
        grid=(nseq,),
        in_specs=in_specs,
        out_specs=(pl.BlockSpec((seq, w), lambda b: (r0 + b, 0)),
                   pl.BlockSpec((1, 2, RET_H, RET_DK, RET_DK), lambda b: (b, 0, 0, 0, 0))),
        scratch_shapes=[pltpu.VMEM((seq, w), F32), pltpu.VMEM((seq, w), F32)],
        input_output_aliases=alias,
        compiler_params=_cparams(("arbitrary",)),
    )(*args)


def _softplus(x):
    return jnp.maximum(x, 0.0) + jnp.log1p(jnp.exp(-jnp.abs(x)))


def _l2n(x):
    return x * lax.rsqrt(jnp.sum(x * x, axis=-1, keepdims=True) + EPS)


def _dn_kernel(*refs, seq, has_s0):
    it = iter(refs)
    q_ref, k_ref, v_ref, z_ref, ab_ref, wq_ref, wk_ref, wv_ref, alog_ref, dtb_ref, nrm_ref = (
        next(it) for _ in range(11))
    if has_s0:
        s0_ref = next(it)
    y_ref, sfin_ref = next(it), next(it)
    of_scr, ob_scr, q_scr, k_scr, v_scr, qe_scr, oc_scr, m_scr, c_scr, gt_scr = (next(it) for _ in range(10))
    nch = seq // CHUNK
    npair = seq // PAIR
    hcols = lambda hh: slice(hh * DN_DK, (hh + 1) * DN_DK)

    xq = _silu(_conv3(q_ref[...].astype(F32), wq_ref[...]))
    xk = _silu(_conv3(k_ref[...].astype(F32), wk_ref[...]))
    v_scr[...] = _silu(_conv3(v_ref[...].astype(F32), wv_ref[...]))
    for hh in range(DN_H):
        q_scr[:, hcols(hh)] = _l2n(xq[:, hcols(hh)]) * (DN_DK ** -0.5)
        k_scr[:, hcols(hh)] = _l2n(xk[:, hcols(hh)])

    ti = lax.broadcasted_iota(I32, (PAIR, PAIR), 0)
    si = lax.broadcasted_iota(I32, (PAIR, PAIR), 1)
    same = (ti >> 6) == (si >> 6)
    s_in = si & (CHUNK - 1)
    eye = ti == si
    eye_f = eye.astype(F32)
    lane = lax.broadcasted_iota(I32, (1, 2 * LANE), 1)

    def ab_col(blk, idx):
        return jnp.sum(jnp.where(lane == AB_LANE + idx, blk, 0.0), axis=1, keepdims=True)

    masks = []
    for d in range(2):
        masks.append((same & ((si <= ti) if d == 0 else (si >= ti)),
                      same & ((si < ti) if d == 0 else (si > ti)),
                      same & ((ti <= si) if d == 0 else (ti >= si)),
                      same & (s_in == (CHUNK - 1 if d == 0 else 0))))

    def prep(p, carry):
        rows = pl.ds(pl.multiple_of(p * PAIR, PAIR), PAIR)
        blk = ab_ref[rows, :]
        chains = []
        for hh in range(DN_H):
            cs = hcols(hh)
            qn, kn, vn = q_scr[rows, cs], k_scr[rows, cs], v_scr[rows, cs]
            kb = kn.astype(BF16)
            kk = _dot_nt(kb, kb)
            qkr = _dot_nt(qn.astype(BF16), kb)
            for d in range(2):
                incl, strict, incl_t, last_s = masks[d]
                neg_a = -jnp.exp(alog_ref[d, hh, 0:1, 0:1])
                dtb = dtb_ref[d, hh, 0:1, 0:1]
                la = neg_a * _softplus(ab_col(blk, d * DN_H + hh) + dtb)
                beta = _sigmoid(ab_col(blk, 2 * DN_H + d * DN_H + hh))
                g_row = jnp.sum(jnp.where(incl_t, la, 0.0), axis=0, keepdims=True)
                g_col = jnp.sum(jnp.where(eye, g_row, 0.0), axis=1, keepdims=True)
                g_end = jnp.sum(jnp.where(last_s, g_row, 0.0), axis=1, keepdims=True)
                decay = jnp.where(incl, jnp.exp(jnp.where(incl, g_col - g_row, 0.0)), 0.0)
                nmat = jnp.where(strict, -(beta * decay * kk), 0.0)
                eg = jnp.exp(g_col)
                chains.append(dict(
                    hh=hh, d=d, cs=cs, g_row=g_row, tinv=eye_f + nmat, pw=_split(nmat),
                    bv=_split(beta * vn), bk=_split((beta * eg) * kn), egq=eg * qn,
                    qk=(qkr * decay).astype(BF16), kd=(jnp.exp(g_end - g_col) * kn).astype(BF16)))
        for it in range(5):
            fine = it < 2
            sq = [_dot3s(c['pw'], c['pw']) if fine else _dot(c['pw'][0], c['pw'][0]) for c in chains]
            for c, s in zip(chains, sq):
                c['pw'] = _split(s)
            up = [_dot3s(_split(c['tinv']), c['pw']) if fine else _dot(c['tinv'].astype(BF16), c['pw'][0])
                  for c in chains]
            for c, u in zip(chains, up):
                c['tinv'] = c['tinv'] + u
        for c in chains:
            c['ts'] = _split(c['tinv'])
        for c in chains:
            c['uvb'] = _dot3s(c['ts'], c['bv']).astype(BF16)
        for c in chains:
            c['wkb'] = _dot3s(c['ts'], c['bk']).astype(BF16)
        for c in chains:
            qe_scr[c['d'], rows, c['cs']] = (c['egq'] - _dot(c['qk'], c['wkb'])).astype(BF16)
            oc_scr[c['d'], rows, c['cs']] = _dot(c['qk'], c['uvb'])
        for c in chains:
            d, j = c['d'], c['d'] * DN_H + c['hh']
            for half in range(2):
                sl = slice(half * CHUNK, (half + 1) * CHUNK)
                n = p * 2 + half
                mrows = pl.ds(pl.multiple_of(n * DN_DK, DN_DK), DN_DK)
                m_scr[j, mrows, :] = _dot_tn(c['kd'][sl], c['wkb'][sl]).astype(BF16)
                c_scr[j, mrows, :] = _dot_tn(c['kd'][sl], c['uvb'][sl])
                e = half * CHUNK + (CHUNK - 1 if d == 0 else 0)
                gt_scr[j, pl.ds(pl.multiple_of(n * 8, 8), 8), :] = jnp.broadcast_to(
                    jnp.exp(c['g_row'][:, e:e + 1]), (8, LANE))
        return carry

    lax.fori_loop(0, npair, prep, 0)

    for d in range(2):
        for hh in range(DN_H):
            sfin_ref[0, d, hh] = s0_ref[0, 0, d, hh] if has_s0 else jnp.zeros((DN_DK, DN_DK), F32)

    def step(i, carry):
        for d in range(2):
            n = i if d == 0 else nch - 1 - i
            rows = pl.ds(pl.multiple_of(n * CHUNK, CHUNK), CHUNK)
            mrows = pl.ds(pl.multiple_of(n * DN_DK, DN_DK), DN_DK)
            o_scr = of_scr if d == 0 else ob_scr
            for hh in range(DN_H):
                st = sfin_ref[0, d, hh]
                sb = st.astype(BF16)
                o_scr[rows, hcols(hh)] = _dot(qe_scr[d, rows, hcols(hh)], sb) + oc_scr[d, rows, hcols(hh)]
                gt = gt_scr[d * DN_H + hh, pl.ds(pl.multiple_of(n * 8, 8), 8), :][0:1, 0:1]
                sfin_ref[0, d, hh] = (gt * st - _dot(m_scr[d * DN_H + hh, mrows, :], sb)
                                      + c_scr[d * DN_H + hh, mrows, :])
        return carry

    lax.fori_loop(0, nch, step, 0)

    for hh in range(DN_H):
        o = of_scr[:, hcols(hh)] + ob_scr[:, hcols(hh)]
        y = o * lax.rsqrt(jnp.mean(o * o, axis=-1, keepdims=True) + EPS) * nrm_ref[...]
        y_ref[:, hcols(hh)] = y * _silu(z_ref[:, hcols(hh)].astype(F32))


def _deltanet(proj_a, proj, conv_w, alog_b, dtb_b, nrm, row_off, nseq, seq, s0, layer, t_total, prev):
    r0 = row_off // seq
    w = DN_H * DN_DK
    spec = lambda col: pl.BlockSpec((seq, w), lambda b: (r0 + b, col // w))
    wspec = lambda j: pl.BlockSpec((3, w), lambda b: (0, j))
    bspec = pl.BlockSpec((2, DN_H, 8, LANE), lambda b: (0, 0, 0, 0))
    in_specs = [spec(COL_DNQKV), spec(COL_DNQKV + w), spec(COL_DNQKV + 2 * w), spec(COL_DNZ),
                pl.BlockSpec((seq, 2 * LANE), lambda b: (r0 + b, COL_KV // (2 * LANE))),
                wspec(0), wspec(1), wspec(2), bspec, bspec,
                pl.BlockSpec((1, DN_DK), lambda b: (0, 0))]
    args = [proj, proj, proj, proj, proj_a, conv_w, conv_w, conv_w, alog_b, dtb_b, nrm.reshape(1, DN_DK)]
    if s0 is not None:
        in_specs.append(pl.BlockSpec((1, 1, 2, DN_H, DN_DK, DN_DK), lambda b: (b, layer, 0, 0, 0, 0)))
        args.append(s0)
    nch = seq // CHUNK
    scratch = [pltpu.VMEM((seq, w), F32), pltpu.VMEM((seq, w), F32),
               pltpu.VMEM((seq, w), F32), pltpu.VMEM((seq, w), F32), pltpu.VMEM((seq, w), F32),
               pltpu.VMEM((2, seq, w), BF16), pltpu.VMEM((2, seq, w), F32),
               pltpu.VMEM((2 * DN_H, nch * DN_DK, DN_DK), BF16), pltpu.VMEM((2 * DN_H, nch * DN_DK, DN_DK), F32),
               pltpu.VMEM((2 * DN_H, nch * 8, LANE), F32)]
    kern, alias = _into(functools.partial(_dn_kernel, seq=seq, has_s0=s0 is not None), in_specs, args, prev)
    return pl.pallas_call(
        kern,
        out_shape=(jax.ShapeDtypeStruct((t_total, w), F32),
                   jax.ShapeDtypeStruct((nseq, 2, DN_H, DN_DK, DN_DK), F32)),
        grid=(nseq,),
        in_specs=in_specs,
        out_specs=(pl.BlockSpec((seq, w), lambda b: (r0 + b, 0)),
                   pl.BlockSpec((1, 2, DN_H, DN_DK, DN_DK), lambda b: (b, 0, 0, 0, 0))),
        scratch_shapes=scratch,
        input_output_aliases=alias,
        compiler_params=_cparams(("arbitrary",)),
    )(*args)


def _pack_bf16_pairs(x):
    n = x.shape[1] // 2
    bits = lax.bitcast_convert_type(x.astype(BF16).astype(F32), U32)
    return (bits[:, n:] & jnp.uint32(0xFFFF0000)) | (bits[:, :n] >> 16)


def _unpack_bf16_pairs(u):
    lo = lax.bitcast_convert_type(u << 16, F32)
    hi = lax.bitcast_convert_type(u & jnp.uint32(0xFFFF0000), F32)
    return lo, hi


ROW_T = (D // 2) // LANE


def _store_row_tiles(ref, x):
    rows = ref.shape[0]
    flat = ref.reshape(rows * ROW_T, LANE)
    for s in range(ROW_T):
        flat[pl.ds(s, rows, stride=ROW_T), :] = x[:, s * LANE:(s + 1) * LANE]


def _load_row_tiles(ref):
    rows = ref.shape[0]
    flat = ref.reshape(rows * ROW_T, LANE)
    return jnp.concatenate([flat[pl.ds(s, rows, stride=ROW_T), :] for s in range(ROW_T)], axis=1)


def _branch_out_kernel(b0, b1, b2, b3, g0, g1, g2, g3, x_ref, mod_ref, nf_ref, wb_ref, wo_ref,
                       xo_ref, h_ref, hp_ref):
    acc = None
    for n, (br, gl) in enumerate(((b0, g0), (b1, g1), (b2, g2), (b3, g3))):
        p = _sigmoid(gl[...].astype(F32)) * _dot(br[...].astype(BF16), wb_ref[n])
        acc = p if acc is None else acc + p
    y = _dot(acc.astype(BF16), wo_ref[...])
    m = mod_ref[0]
    x = x_ref[...] + m[:, 2 * D:3 * D] * y
    xo_ref[...] = x
    h = _rms(x, nf_ref[...]) * (1.0 + m[:, 4 * D:5 * D]) + m[:, 3 * D:4 * D]
    h_ref[...] = h
    _store_row_tiles(hp_ref, _pack_bf16_pairs(h))


def _branch_out(branches, proj, x, mod3, nf, wb, wo, grp_of_tile, tm):
    t = x.shape[0]
    gb = COL_GATE // D
    bspec = pl.BlockSpec((tm, MIX_W), lambda i: (i, 0))
    gspec = lambda n: pl.BlockSpec((tm, D), lambda i: (i, gb + n))
    row = pl.BlockSpec((tm, D), lambda i: (i, 0))
    return pl.pallas_call(
        _branch_out_kernel,
        out_shape=(jax.ShapeDtypeStruct((t, D), F32), jax.ShapeDtypeStruct((t, D), F32),
                   jax.ShapeDtypeStruct((t, ROW_T, LANE), U32)),
        grid=(t // tm,),
        in_specs=[bspec] * 4 + [gspec(n) for n in range(4)] + [
            row,
            pl.BlockSpec((1, 1, 6 * D), lambda i: (grp_of_tile(tm)(i), 0, 0)),
            pl.BlockSpec((1, D), lambda i: (0, 0)),
            pl.BlockSpec((4, MIX_W, D), lambda i: (0, 0, 0)),
            pl.BlockSpec((D, D), lambda i: (0, 0))],
        out_specs=(row, row, pl.BlockSpec((tm, ROW_T, LANE), lambda i: (i, 0, 0))),
        compiler_params=_cparams(("arbitrary",)),
    )(*branches, proj, proj, proj, proj, x, mod3, nf.reshape(1, D), wb, wo)


def _route_kernel(h_ref, wr_ref, bias_ref, idx_ref, w_ref, rank_ref, cnt_ref, run_scr, *, tm):
    @pl.when(pl.program_id(0) == 0)
    def _():
        run_scr[...] = jnp.zeros_like(run_scr)

    neg = -jnp.inf
    gsz = N_EXP // N_GRP
    scores = _sigmoid(_dot3_nt(wr_ref[...], h_ref[...]))
    choice = scores + bias_ref[:, 0:1]
    row8 = lax.broadcasted_iota(I32, (gsz, tm), 0)
    gscore = []
    for g in range(N_GRP):
        blk = choice[g * gsz:(g + 1) * gsz]
        m1 = jnp.max(blk, axis=0, keepdims=True)
        i1 = jnp.min(jnp.where(blk == m1, row8, gsz), axis=0, keepdims=True)
        m2 = jnp.max(jnp.where(row8 == i1, neg, blk), axis=0, keepdims=True)
        gscore.append(m1 + m2)
    masked = []
    for g in range(N_GRP):
        rank = jnp.zeros((1, tm), I32)
        for g2 in range(N_GRP):
            if g2 == g:
                continue
            ahead = (gscore[g2] >= gscore[g]) if g2 < g else (gscore[g2] > gscore[g])
            rank = rank + ahead.astype(I32)
        masked.append(jnp.where(rank < TOPK_GRP, choice[g * gsz:(g + 1) * gsz], neg))
    cur = jnp.concatenate(masked, axis=0)
    row = lax.broadcasted_iota(I32, (N_EXP, tm), 0)
    sel = jnp.zeros((N_EXP, tm), jnp.bool_)
    idxs, scs = [], []
    for _ in range(TOP_K):
        m = jnp.max(cur, axis=0, keepdims=True)
        ik = jnp.min(jnp.where(cur == m, row, N_EXP), axis=0, keepdims=True)
        hit = row == ik
        scs.append(jnp.sum(jnp.where(hit, scores, 0.0), axis=0, keepdims=True))
        idxs.append(ik)
        cur = jnp.where(hit, neg, cur)
        sel = sel | hit
    tot = functools.reduce(jnp.add, scs)
    self_f = jnp.where(sel, 1.0, 0.0)
    tri = (lax.broadcasted_iota(I32, (tm, tm), 0) <= lax.broadcasted_iota(I32, (tm, tm), 1))
    csum = _dot(self_f.astype(BF16), jnp.where(tri, 1.0, 0.0).astype(BF16))
    run = run_scr[:, 0:1]
    rank_all = run + csum - self_f
    ranks = [jnp.sum(jnp.where(row == ik, rank_all, 0.0), axis=0, keepdims=True) for ik in idxs]
    idx_ref[...] = jnp.concatenate(idxs, axis=0)
    w_ref[...] = jnp.concatenate([s / tot * ROUTED_SCALE for s in scs], axis=0)
    rank_ref[...] = jnp.concatenate(ranks, axis=0).astype(I32)
    new_run = run + csum[:, tm - 1:tm]
    run_scr[...] = jnp.broadcast_to(new_run, run_scr.shape)
    cnt_ref[...] = jnp.broadcast_to(new_run, cnt_ref.shape).astype(I32)


def _route(h, wr_t, bias_b, tm):
    t = h.shape[0]
    tok = pl.BlockSpec((TOP_K, tm), lambda i: (0, i))
    return pl.pallas_call(
        functools.partial(_route_kernel, tm=tm),
        out_shape=(jax.ShapeDtypeStruct((TOP_K, t), I32), jax.ShapeDtypeStruct((TOP_K, t), F32),
                   jax.ShapeDtypeStruct((TOP_K, t), I32), jax.ShapeDtypeStruct((N_EXP, LANE), I32)),
        grid=(t // tm,),
        in_specs=[pl.BlockSpec((tm, D), lambda i: (i, 0)),
                  pl.BlockSpec((N_EXP, D), lambda i: (0, 0)),
                  pl.BlockSpec((N_EXP, LANE), lambda i: (0, 0))],
        out_specs=(tok, tok, tok, pl.BlockSpec((N_EXP, LANE), lambda i: (0, 0))),
        scratch_shapes=[pltpu.VMEM((N_EXP, LANE), F32)],
        compiler_params=_cparams(("arbitrary",)),
    )(h, wr_t, bias_b)


def _slot_pos_kernel(ps_ref, idx_ref, rank_ref, pos_ref):
    idx = idx_ref[...]
    pos = rank_ref[...]
    for e in range(N_EXP):
        pos = pos + jnp.where(idx == e, ps_ref[e], 0)
    pos_ref[...] = pos


def _slot_pos(pstart, idx_t, rank_t, tm):
    t = idx_t.shape[1]
    tok = pl.BlockSpec((TOP_K, tm), lambda i, *_: (0, i))
    return pl.pallas_call(
        _slot_pos_kernel,
        out_shape=jax.ShapeDtypeStruct((TOP_K, t), I32),
        grid_spec=pltpu.PrefetchScalarGridSpec(
            num_scalar_prefetch=1, grid=(t // tm,), in_specs=[tok, tok], out_specs=tok),
        compiler_params=_cparams(("arbitrary",)),
    )(pstart, idx_t, rank_t)


def _dispatch_kernel(pstart_ref, padded_ref, pos_ref, h_ref, xs_ref, zero_scr, sem, zsem, *, tm):
    def zero_copy(e):
        start = pl.multiple_of(pstart_ref[e] + padded_ref[e] - MOE_BM, MOE_BM)
        return pltpu.make_async_copy(zero_scr, xs_ref.at[pl.ds(start, MOE_BM)], zsem)

    @pl.when(pl.program_id(0) == 0)
    def _():
        zero_scr[...] = jnp.zeros_like(zero_scr)

        def zstart(e, c):
            @pl.when(padded_ref[e] > 0)
            def _():
                zero_copy(e).start()
            return c

        def zwait(e, c):
            @pl.when(padded_ref[e] > 0)
            def _():
                zero_copy(e).wait()
            return c

        lax.fori_loop(0, N_EXP, zstart, 0)
        lax.fori_loop(0, N_EXP, zwait, 0)

    def row_copy(t, k):
        return pltpu.make_async_copy(h_ref.at[pl.ds(t, 1)], xs_ref.at[pl.ds(pos_ref[k, t], 1)], sem)

    def issue(t, c):
        for k in range(TOP_K):
            row_copy(t, k).start(priority=k % 2)
        return c

    def drain(t, c):
        for k in range(TOP_K):
            row_copy(t, k).wait()
        return c

    lax.fori_loop(0, tm, issue, 0)
    lax.fori_loop(0, tm, drain, 0)


def _dispatch(pstart, padded, pos_t, h, n_slots, tm):
    t = h.shape[0]
    smem_tok = pl.BlockSpec((TOP_K, tm), lambda i, *_: (0, i), memory_space=pltpu.SMEM)
    return pl.pallas_call(
        functools.partial(_dispatch_kernel, tm=tm),
        out_shape=jax.ShapeDtypeStruct((n_slots,) + h.shape[1:], h.dtype),
        grid_spec=pltpu.PrefetchScalarGridSpec(
            num_scalar_prefetch=2,
            grid=(t // tm,),
            in_specs=[smem_tok, pl.BlockSpec((tm,) + h.shape[1:], lambda i, *_: (i, 0, 0))],
            out_specs=pl.BlockSpec(memory_space=pl.ANY),
            scratch_shapes=[pltpu.VMEM((MOE_BM,) + h.shape[1:], h.dtype), pltpu.SemaphoreType.DMA(()),
                            pltpu.SemaphoreType.DMA(())]),
        compiler_params=_cparams(("arbitrary",)),
    )(pstart, padded, pos_t, h)


def _experts_kernel(be_ref, nu_ref, x_ref, wg_ref, wu_ref, wd_ref, y_ref, wg_s, wu_s, wd_s):
    b = pl.program_id(0)

    @pl.when(jnp.logical_or(b == 0, be_ref[b] != be_ref[jnp.maximum(b - 1, 0)]))
    def _():
        wg_s[...] = wg_ref[0, 0].astype(BF16)
        wu_s[...] = wu_ref[0, 0].astype(BF16)
        wd_s[...] = wd_ref[0, 0].astype(BF16)

    @pl.when(b < nu_ref[0])
    def _():
        x = jnp.concatenate(_unpack_bf16_pairs(_load_row_tiles(x_ref)), axis=1).astype(BF16)
        a = _silu(_dot(x, wg_s[...])) * _dot(x, wu_s[...])
        _store_row_tiles(y_ref, _pack_bf16_pairs(_dot(a.astype(BF16), wd_s[...])))


def _experts(block_e, n_used, xs, wg, wu, wd, layer):
    nb = xs.shape[0] // MOE_BM
    blk = lambda b, be, nu: (jnp.minimum(b, nu[0] - 1), 0, 0)
    wsel = lambda b, be, nu: (layer, be[b], 0, 0)
    return pl.pallas_call(
        _experts_kernel,
        out_shape=jax.ShapeDtypeStruct(xs.shape, U32),
        grid_spec=pltpu.PrefetchScalarGridSpec(
            num_scalar_prefetch=2,
            grid=(nb,),
            in_specs=[pl.BlockSpec((MOE_BM, ROW_T, LANE), blk),
                      pl.BlockSpec((1, 1, D, D_EXP), wsel),
                      pl.BlockSpec((1, 1, D, D_EXP), wsel),
                      pl.BlockSpec((1, 1, D_EXP, D), wsel)],
            out_specs=pl.BlockSpec((MOE_BM, ROW_T, LANE), blk),
            scratch_shapes=[pltpu.VMEM((D, D_EXP), BF16), pltpu.VMEM((D, D_EXP), BF16),
                            pltpu.VMEM((D_EXP, D), BF16)]),
        compiler_params=_cparams(("arbitrary",)),
    )(block_e, n_used, xs, wg, wu, wd)


def _combine_kernel(*refs, tm, final):
    it = iter(refs)
    pos_ref, w_ref, h_ref, x_ref, mod_ref, wsg_ref, wsu_ref, wsd_ref = (next(it) for _ in range(8))
    if final:
        fn_ref = next(it)
    ys_ref = next(it)
    xo_ref = next(it)
    if final:
        yo_ref = next(it)
    buf, sem = next(it), next(it)

    def row_copy(t, k):
        return pltpu.make_async_copy(ys_ref.at[pl.ds(pos_ref[k, t], 1)], buf.at[k, pl.ds(t, 1)], sem)

    def issue(t, c):
        for k in range(TOP_K):
            row_copy(t, k).start(priority=k % 2)
        return c

    def drain(t, c):
        for k in range(TOP_K):
            row_copy(t, k).wait()
        return c

    lax.fori_loop(0, tm, issue, 0)
    hb = h_ref[...].astype(BF16)
    a = _silu(_dot(hb, wsg_ref[...])) * _dot(hb, wsu_ref[...])
    shared = _dot(a.astype(BF16), wsd_ref[...])
    lax.fori_loop(0, tm, drain, 0)
    w = w_ref[...]
    lo, hi = None, None
    for k in range(TOP_K):
        yl, yh = _unpack_bf16_pairs(_load_row_tiles(buf.at[k]))
        wk = w[:, k:k + 1]
        lo = yl * wk if lo is None else lo + yl * wk
        hi = yh * wk if hi is None else hi + yh * wk
    routed = jnp.concatenate([lo, hi], axis=1)
    m = mod_ref[0]
    x = x_ref[...] + m[:, 5 * D:6 * D] * (routed + shared)
    xo_ref[...] = x
    if final:
        yo_ref[...] = _rms(x, fn_ref[...])


def _combine(pos_t, w_tok, h, x, mod3, wsg, wsu, wsd, fn, ys, grp_of_tile, tm):
    t = h.shape[0]
    final = fn is not None
    row = pl.BlockSpec((tm, D), lambda i: (i, 0))
    full = lambda shp: pl.BlockSpec(shp, lambda i: (0,) * len(shp))
    in_specs = [pl.BlockSpec((TOP_K, tm), lambda i: (0, i), memory_space=pltpu.SMEM),
                pl.BlockSpec((tm, TOP_K), lambda i: (i, 0)), row, row,
                pl.BlockSpec((1, 1, 6 * D), lambda i: (grp_of_tile(tm)(i), 0, 0)),
                full((D, D_EXP)), full((D, D_EXP)), full((D_EXP, D))]
    args = [pos_t, w_tok, h, x, mod3, wsg, wsu, wsd]
    if final:
        in_specs.append(full((1, D)))
        args.append(fn.reshape(1, D))
    in_specs.append(pl.BlockSpec(memory_space=pl.ANY))
    args.append(ys)
    out_shape = [jax.ShapeDtypeStruct((t, D), F32)]
    out_specs = [row]
    if final:
        out_shape.append(jax.ShapeDtypeStruct((t, D), F32))
        out_specs.append(row)
    return pl.pallas_call(
        functools.partial(_combine_kernel, tm=tm, final=final),
        out_shape=tuple(out_shape),
        grid=(t // tm,),
        in_specs=in_specs,
        out_specs=tuple(out_specs),
        scratch_shapes=[pltpu.VMEM((TOP_K, tm, ROW_T, LANE), U32), pltpu.SemaphoreType.DMA(())],
        compiler_params=_cparams(("arbitrary",)),
    )(*args)


_W_IN_CUTS = np.cumsum([0, 256, 160, 1536, 512, 16, 2048, 1536, 4096])


def _pack_w_in_kernel(w_ref, wa_ref, wb_ref):
    seg = lambda i: w_ref[0, :, int(_W_IN_CUTS[i]):int(_W_IN_CUTS[i + 1])]
    pad = jnp.zeros((w_ref.shape[1], 2 * LANE - 160 - 16), F32)
    wa_ref[...] = jnp.concatenate([seg(0), seg(1), seg(4), pad], axis=1).astype(BF16)
    wb_ref[...] = jnp.concatenate([seg(7), seg(2), seg(3), seg(5), seg(6)], axis=1).astype(BF16)


def _pack_w_in(w_all, layer):
    rows = 128
    return pl.pallas_call(
        _pack_w_in_kernel,
        out_shape=(jax.ShapeDtypeStruct((D, N_PROJ_A), BF16), jax.ShapeDtypeStruct((D, N_PROJ_B), BF16)),
        grid=(D // rows,),
        in_specs=[pl.BlockSpec((1, rows, w_all.shape[2]), lambda i: (layer, i, 0))],
        out_specs=(pl.BlockSpec((rows, N_PROJ_A), lambda i: (i, 0)),
                   pl.BlockSpec((rows, N_PROJ_B), lambda i: (i, 0))),
        compiler_params=_cparams(("arbitrary",)),
    )(w_all)


def _pack_w_uq(w):
    w = w.reshape(MLA_QR, MLA_H, MLA_NOPE + MLA_ROPE)
    w = jnp.pad(w, ((0, 0), (0, 0), (0, LANE - MLA_NOPE - MLA_ROPE)))
    return w.reshape(MLA_QR, MLA_H * LANE).astype(BF16)


def _pack_w_ukv(w):
    w = w.reshape(MLA_KVR, MLA_H, MLA_NOPE + MLA_V)
    k_nope = jnp.pad(w[:, :, :MLA_NOPE], ((0, 0), (0, 0), (0, LANE - MLA_NOPE)))
    eye = jnp.eye(MLA_ROPE, dtype=w.dtype)[:, None, :]
    k_pe = jnp.pad(jnp.broadcast_to(eye, (MLA_ROPE, MLA_H, MLA_ROPE)),
                   ((0, LANE - MLA_ROPE), (0, 0), (MLA_NOPE, LANE - MLA_NOPE - MLA_ROPE)))
    wk = jnp.concatenate([k_nope, k_pe], axis=0).reshape(2 * LANE, MLA_H * LANE)
    wv = w[:, :, MLA_NOPE:].reshape(MLA_KVR, MLA_H * MLA_V)
    return wk.astype(BF16), wv.astype(BF16)


def _axial_angles(n_tok, dim):
    nf = dim // 4
    inv = ROPE_BASE ** (-jnp.arange(nf, dtype=F32) / nf)
    r = jnp.repeat(jnp.arange(n_tok // GRID_W, dtype=F32), GRID_W)
    cc = jnp.tile(jnp.arange(GRID_W, dtype=F32), n_tok // GRID_W)
    ang = jnp.concatenate([r[:, None] * inv, cc[:, None] * inv], axis=-1)
    return jnp.cos(ang), jnp.sin(ang)


def _rope_tables_mla(n_tok, lane0):
    cos, sin = _axial_angles(n_tok, MLA_ROPE)
    half = MLA_ROPE // 2
    z = lambda w: jnp.zeros((n_tok, w), F32)
    o = lambda w: jnp.ones((n_tok, w), F32)
    rest = LANE - lane0 - MLA_ROPE
    c = jnp.concatenate([o(lane0), cos, cos, o(rest)], axis=1)
    s1 = jnp.concatenate([z(lane0), -sin, z(half), z(rest)], axis=1)
    s2 = jnp.concatenate([z(lane0), z(half), sin, z(rest)], axis=1)
    return c, s1, s2


def _rope_tables_ret(n_tok):
    cos, sin = _axial_angles(n_tok, RET_DK)
    return jnp.concatenate([cos, cos], axis=1), jnp.concatenate([-sin, sin], axis=1)


def _bcast_dh(a):
    return jnp.broadcast_to(a.astype(F32)[:, :, None, None], a.shape + (8, LANE))


def _forward(x_prompt, x_sample, c, cache_ckv, cache_kpe, state_dn, state_ret, c_ctx, P, tiles):
    nb, sc, _ = x_prompt.shape
    nl, sl, _ = x_sample.shape
    past = cache_ckv.shape[2]
    depth = P['w_in'].shape[0]
    t_ctx, t_lat = nb * sc, nl * sl
    t = t_ctx + t_lat
    tm = tiles['tm']
    assert t_ctx % tm == 0 and sl % tm == 0 and t_ctx % sl == 0

    assert sl % tiles['tc'] == 0 and t_ctx % tiles['tc'] == 0

    def grp_of_tile(rows_per_tile):
        nct = t_ctx // rows_per_tile
        return lambda i: jnp.where(i < nct, 0, 1 + (i - nct) // (sl // rows_per_tile))

    x = jnp.concatenate([x_prompt.reshape(t_ctx, D), x_sample.reshape(t_lat, D)], axis=0)
    ngrp = 1 + nl
    cvec = jnp.concatenate([c_ctx[None], c, jnp.zeros((-(ngrp) % 8, D), F32)], axis=0)

    tab_q = _rope_tables_mla(sl, MLA_NOPE)
    tab_k = _rope_tables_mla(sl, 0)
    tab_r = _rope_tables_ret(sl)

    ckv_l, kpe_l, dn_l, ret_l = [], [], [], []
    y_final = None
    for l in range(depth):
        mod3 = _ada(cvec, P['w_ada'], P['b_ada'][l], l).reshape(cvec.shape[0], 1, 6 * D)
        w_a, w_b = _pack_w_in(P['w_in'], l)
        tp = tiles['tp']
        proj_a = _in_proj(x, mod3, P['norm_mix'][l], w_a, grp_of_tile, tp, N_PROJ_A, F32)
        proj = _in_proj(x, mod3, P['norm_mix'][l], w_b, grp_of_tile, tp, tiles['tn'], BF16)

        w_uq = _pack_w_uq(P['mla_w_uq'][l])
        wk, wv = _pack_w_ukv(P['mla_w_ukv'][l])
        gq, gkv = P['mla_q_norm'][l], P['mla_kv_norm'][l]
        q_c = _mla_q(proj_a, gq, w_uq, 0, t_ctx, tm, None, sc)
        q_l = _mla_q(proj_a, gq, w_uq, t_ctx, t_lat, tm, tab_q, sl)
        kvb = COL_KV // (2 * LANE)
        ckv_c, kpe_c, k_c, v_c = _mla_kv(proj_a, kvb, gkv, wk, wv, 0, t_ctx, tm, True, None, sc)
        _, _, k_l, v_l = _mla_kv(proj_a, kvb, gkv, wk, wv, t_ctx, t_lat, tm, True, tab_k, sl)
        cached = jnp.concatenate([cache_ckv[:, l], cache_kpe[:, l],
                                  jnp.zeros((nl, past, 2 * LANE - MLA_KVR - MLA_ROPE), F32)], axis=-1)
        pt = min(tm, past)
        _, _, k_p, v_p = _mla_kv(cached.reshape(nl * past, 2 * LANE), 0, gkv, wk, wv, 0, nl * past, pt,
                                 False, None, past)
        r3 = lambda a, n, s: a.reshape(n, s, a.shape[-1])
        y_mla = _attn(r3(q_c, nb, sc), [r3(k_c, nb, sc)], [r3(v_c, nb, sc)], min(sc, tiles['tq']), t, 0, None)
        y_mla = _attn(r3(q_l, nl, sl), [r3(k_p, nl, past), r3(k_l, nl, sl)],
                      [r3(v_p, nl, past), r3(v_l, nl, sl)], min(sl, tiles['tq']), t, t_ctx, y_mla)

        alog_b, dtb_b = _bcast_dh(P['dn_A_log'][l]), _bcast_dh(P['dn_dt_bias'][l])
        dn_args = (proj_a, proj, P['dn_conv'][l], alog_b, dtb_b, P['dn_norm'][l])
        y_dn, dn_fin = _deltanet(*dn_args, 0, nb, sc, None, l, t, None)
        y_dn, _ = _deltanet(*dn_args, t_ctx, nl, sl, state_dn, l, t, y_dn)

        dec_b = _bcast_dh(P['ret_decay'][l])
        y_ret, ret_fin = _retention(proj, dec_b, P['ret_gn'][l], 0, nb, sc, None, None, l, t, None)
        y_ret, _ = _retention(proj, dec_b, P['ret_gn'][l], t_ctx, nl, sl, tab_r, state_ret, l, t, y_ret)

        y_sc = _sconv(proj, P['sc_conv'][l], 0, nb, sc, t, None)
        y_sc = _sconv(proj, P['sc_conv'][l], t_ctx, nl, sl, t, y_sc)

        x_mid, h2, h2p = _branch_out((y_mla, y_dn, y_ret, y_sc), proj, x, mod3, P['norm_ffn'][l],
                                P['w_branch'][l].astype(BF16), P['w_out'][l].astype(BF16), grp_of_tile,
                                tiles['tb'])

        bias_b = jnp.broadcast_to(P['router_bias'][l].astype(F32)[:, None], (N_EXP, LANE))
        idx_t, w_t, rank_t, cnt = _route(h2, P['router'][l].T, bias_b, tiles['tr'])
        counts = cnt[:, 0]
        padded = (counts + MOE_BM - 1) // MOE_BM * MOE_BM
        pad_end = jnp.cumsum(padded)
        pstart = (pad_end - padded).astype(I32)
        n_blocks = (t * TOP_K) // MOE_BM + N_EXP
        blk_row0 = jnp.arange(n_blocks, dtype=I32) * MOE_BM
        block_e = jnp.minimum(jnp.sum((pad_end[None, :] <= blk_row0[:, None]).astype(I32), axis=1), N_EXP - 1)
        n_used = (pad_end[-1:] // MOE_BM).astype(I32)
        pos_t = _slot_pos(pstart, idx_t, rank_t, tiles['tr'])
        xs = _dispatch(pstart, padded.astype(I32), pos_t, h2p, n_blocks * MOE_BM, tiles['td'])
        ys = _experts(block_e, n_used, xs, P['w_eg'], P['w_eu'], P['w_ed'], l)
        fn = P['final_norm'] if l == depth - 1 else None
        outs = _combine(pos_t, w_t.T, h2, x_mid, mod3, P['w_sg'][l].astype(BF16),
                        P['w_su'][l].astype(BF16), P['w_sd'][l].astype(BF16), fn, ys, grp_of_tile, tiles['tc'])
        x = outs[0]
        if fn is not None:
            y_final = outs[1]

        ckv_l.append(ckv_c.reshape(nb, sc, MLA_KVR))
        kpe_l.append(kpe_c.reshape(nb, sc, MLA_ROPE))
        dn_l.append(dn_fin)
        ret_l.append(ret_fin)

    y_prompt = y_final[:t_ctx].reshape(nb, sc, D)
    y_sample = y_final[t_ctx:].reshape(nl, sl, D)
    return (y_prompt, y_sample, jnp.stack(ckv_l, axis=1), jnp.stack(kpe_l, axis=1),
            jnp.stack(dn_l, axis=1), jnp.stack(ret_l, axis=1))


_TILES = dict(tm=512, tp=1024, tn=2432, tb=512, tq=256, tr=512, td=256, tc=128)


def kernel(x_prompt, x_sample, c, cache_ckv, cache_kpe, state_dn, state_ret, c_ctx, w_ada, b_ada, norm_mix, norm_ffn, w_in, mla_q_norm, mla_w_uq, mla_kv_norm, mla_w_ukv, dn_conv, dn_A_log, dn_dt_bias, dn_norm, ret_decay, ret_gn, sc_conv, w_branch, w_out, router, router_bias, w_eg, w_eu, w_ed, w_sg, w_su, w_sd, final_norm):
    P = dict(w_ada=w_ada, b_ada=b_ada, norm_mix=norm_mix, norm_ffn=norm_ffn, w_in=w_in,
             mla_q_norm=mla_q_norm, mla_w_uq=mla_w_uq, mla_kv_norm=mla_kv_norm, mla_w_ukv=mla_w_ukv,
             dn_conv=dn_conv, dn_A_log=dn_A_log, dn_dt_bias=dn_dt_bias, dn_norm=dn_norm,
             ret_decay=ret_decay, ret_gn=ret_gn, sc_conv=sc_conv, w_branch=w_branch, w_out=w_out,
             router=router, router_bias=router_bias, w_eg=w_eg, w_eu=w_eu, w_ed=w_ed,
             w_sg=w_sg, w_su=w_su, w_sd=w_sd, final_norm=final_norm)
    return _forward(x_prompt, x_sample, c, cache_ckv, cache_kpe, state_dn, state_ret, c_ctx, P, _TILES)
```
